```python
import math
import jax, jax.numpy as jnp
from jax import lax
import numpy as np

D_MODEL = 1024
BATCH = 4
SEQ = 8192
DEPTH = 4

GRID_W = 64
CTX_LEN = 256
D_FF = 2816
FFN_RES = 0.5
N_MOD = 9
RMS_EPS = 1e-6

NA_HEADS = 8
NA_HEAD_DIM = 64
NA_WIDTH = NA_HEADS * NA_HEAD_DIM
NA_KH = 8
NA_KW = 16
SC_WIDTH = D_MODEL // 2
AB_IN = 3 * NA_WIDTH + 3 * SC_WIDTH
AB_OUT = NA_WIDTH + SC_WIDTH

HY_WIDTH = D_MODEL
HY_EMB = 33
HY_BANDS = (HY_EMB - 1) // 2
HY_FFN = 64
HY_TARGET = 1e-2
HY_FAST_PCT = 0.3
HY_SLOW_PCT = 1.5

N_EVEN = (DEPTH + 1) // 2
N_ODD = DEPTH // 2

kernel_name = 'hybrid_na_shortconv_hyena_dit'


def rms_norm(x, g):
    xf = x.astype(jnp.float32)
    y = xf * lax.rsqrt(jnp.mean(xf * xf, axis=-1, keepdims=True) + RMS_EPS)
    return (y * g.astype(jnp.float32)).astype(x.dtype)


def modulated_norm(x, g, shift, scale):
    return rms_norm(x, g) * (1 + scale) + shift


def half_step_ffn(x, g_pre, g_post, shift, scale, gate, w_gate, w_up, w_down):
    h = modulated_norm(x, g_pre, shift, scale)
    y = (jax.nn.silu(h @ w_gate) * (h @ w_up)) @ w_down
    return x + FFN_RES * gate * rms_norm(y, g_post)


def short_conv(u, w):
    up = jnp.pad(u, ((0, 0), (1, 1), (0, 0)))
    return up[:, :-2] * w[0] + up[:, 1:-1] * w[1] + up[:, 2:] * w[2]


def window_index(n, k):
    start = jnp.clip(jnp.arange(n) - k // 2, 0, n - k)
    return start[:, None] + jnp.arange(k)[None, :]


def split_heads(t):
    return t.reshape(t.shape[:-1] + (NA_HEADS, NA_HEAD_DIM))


def neighbourhood_attention(q, k, v, k_ctx, v_ctx, rpb):
    b, s, h, dh = q.shape
    rows = s // GRID_W
    kh = min(NA_KH, rows)
    kw = NA_KW
    to_grid = lambda t: t.reshape(b, rows, GRID_W, h, dh)
    qg = to_grid(q) * (dh ** -0.5)
    kg, vg = to_grid(k), to_grid(v)
    row_idx = window_index(rows, kh)
    col_idx = window_index(GRID_W, kw)
    k_band = kg[:, row_idx]
    v_band = vg[:, row_idx]
    sel = (col_idx[:, :, None] == jnp.arange(GRID_W)[None, None, :]).astype(q.dtype)
    s_band = jnp.einsum('brchd,brykhd->bhrcyk', qg, k_band)
    s_win = jnp.einsum('bhrcyk,cwk->bhrcyw', s_band, sel)
    dy = row_idx - jnp.arange(rows)[:, None] + (NA_KH - 1)
    dx = col_idx - jnp.arange(GRID_W)[:, None] + (NA_KW - 1)
    bias = rpb[:, dy[:, None, :, None], dx[None, :, None, :]]
    s_win = s_win + bias[None].astype(s_win.dtype)
    s_ctx = jnp.einsum('brchd,bjhd->bhrcj', qg, k_ctx)
    logits = jnp.concatenate([s_win.reshape(b, h, rows, GRID_W, kh * kw), s_ctx], axis=-1)
    p = jax.nn.softmax(logits.astype(jnp.float32), axis=-1).astype(v.dtype)
    p_win = p[..., :kh * kw].reshape(b, h, rows, GRID_W, kh, kw)
    p_band = jnp.einsum('bhrcyw,cwk->bhrcyk', p_win, sel)
    out = (jnp.einsum('bhrcyk,brykhd->brchd', p_band, v_band)
           + jnp.einsum('bhrcj,bjhd->brchd', p[..., kh * kw:], v_ctx))
    return out.reshape(b, s, h * dh)


def context_attention(q, k, v):
    b, n, h, dh = q.shape
    logits = jnp.einsum('bihd,bjhd->bhij', q * (dh ** -0.5), k)
    p = jax.nn.softmax(logits.astype(jnp.float32), axis=-1).astype(v.dtype)
    return jnp.einsum('bhij,bjhd->bihd', p, v).reshape(b, n, h * dh)


def mixer_ab(h_lat, h_ctx, w_in, rpb, conv_w, w_out, ctx_out):
    cuts = [NA_WIDTH, 2 * NA_WIDTH, 3 * NA_WIDTH, 3 * NA_WIDTH + SC_WIDTH, 3 * NA_WIDTH + 2 * SC_WIDTH]
    q, k, v, gb, gc, xb = jnp.split(h_lat @ w_in, cuts, axis=-1)
    if ctx_out:
        q_c, k_c, v_c, gb_c, gc_c, xb_c = jnp.split(h_ctx @ w_in, cuts, axis=-1)
    else:
        k_c, v_c = jnp.split(h_ctx @ w_in[:, NA_WIDTH:3 * NA_WIDTH], 2, axis=-1)
    k_c, v_c = split_heads(k_c), split_heads(v_c)
    a_lat = neighbourhood_attention(split_heads(q), split_heads(k), split_heads(v), k_c, v_c, rpb)
    b_lat = gb * short_conv(gc * xb, conv_w)
    y_lat = jnp.concatenate([a_lat, b_lat], axis=-1) @ w_out
    if not ctx_out:
        return y_lat, None
    a_ctx = context_attention(split_heads(q_c), k_c, v_c)
    b_ctx = gb_c * short_conv(gc_c * xb_c, conv_w)
    y_ctx = jnp.concatenate([a_ctx, b_ctx], axis=-1) @ w_out
    return y_lat, y_ctx


def hyena_filters(n, w1, b1, w2, b2, w3, b3, w4, freq):
    f32 = jnp.float32
    t = jnp.linspace(0.0, 1.0, n, dtype=f32)[:, None]
    w = 2.0 * math.pi * jnp.arange(n, dtype=f32)[:, None] / n
    bands = jnp.linspace(1e-4, HY_BANDS - 1, HY_BANDS, dtype=f32)[None, :]
    z = jnp.concatenate([t, jnp.cos(bands * w), -jnp.sin(bands * w)], axis=-1)
    fr = freq.astype(f32)
    hdn = jnp.sin(fr * (z @ w1.astype(f32) + b1.astype(f32)))
    hdn = jnp.sin(fr * (hdn @ w2.astype(f32) + b2.astype(f32)))
    hdn = jnp.sin(fr * (hdn @ w3.astype(f32) + b3.astype(f32)))
    filt = (hdn @ w4.astype(f32)).reshape(n, 2, HY_WIDTH)
    max_decay = math.log(HY_TARGET) / HY_FAST_PCT
    min_decay = math.log(HY_TARGET) / HY_SLOW_PCT
    deltas = jnp.linspace(min_decay, max_decay, HY_WIDTH, dtype=f32)
    window = jnp.exp(-t * jnp.abs(deltas)[None, :])
    filt = filt * window[:, None, :]
    return filt[:, 0], filt[:, 1]


def bidirectional_long_conv(u, h_fwd, h_bwd):
    n = u.shape[1]
    k = jnp.concatenate([h_fwd, jnp.zeros((1, HY_WIDTH), jnp.float32), h_bwd[1:][::-1]], axis=0)
    u_f = jnp.fft.rfft(u.astype(jnp.float32), n=2 * n, axis=1)
    k_f = jnp.fft.rfft(k, n=2 * n, axis=0)
    y = jnp.fft.irfft(u_f * k_f[None], n=2 * n, axis=1)[:, :n]
    return y.astype(u.dtype)


def hyena_mixer(h, w_in, short_w, w1, b1, w2, b2, w3, b3, w4, freq, bias, w_out):
    n = h.shape[1]
    x0, x1, v = jnp.split(short_conv(h @ w_in, short_w), 3, axis=-1)
    h_fwd, h_bwd = hyena_filters(n, w1, b1, w2, b2, w3, b3, w4, freq)
    u = v * x1
    return (x0 * (bidirectional_long_conv(u, h_fwd, h_bwd) + u * bias)) @ w_out


def setup_inputs(seed: int = 0) -> dict:
    key = jax.random.key(seed)
    ks = jax.random.split(key, 26)
    f32 = jnp.float32
    nrm = lambda k, shape: jax.random.normal(k, shape, f32)
    lin = lambda k, shape, fan_in: nrm(k, shape) * fan_in ** -0.5
    return {
        'x': nrm(ks[0], (BATCH, SEQ, D_MODEL)),
        'c': nrm(ks[1], (BATCH, D_MODEL)),
        'ctx': nrm(ks[2], (BATCH, CTX_LEN, D_MODEL)),
        'c_ctx': nrm(ks[3], (D_MODEL,)),
        'w_mod': 0.5 * lin(ks[4], (DEPTH, D_MODEL, N_MOD * D_MODEL), D_MODEL),
        'b_mod': 0.02 * nrm(ks[5], (DEPTH, N_MOD * D_MODEL)),
        'norm_g': 1.0 + 0.05 * nrm(ks[6], (DEPTH, 6, D_MODEL)),
        'ffn_w_gate': lin(ks[7], (DEPTH, 2, D_MODEL, D_FF), D_MODEL),
        'ffn_w_up': lin(ks[8], (DEPTH, 2, D_MODEL, D_FF), D_MODEL),
        'ffn_w_down': lin(ks[9], (DEPTH, 2, D_FF, D_MODEL), D_FF),
        'ab_w_in': lin(ks[10], (N_EVEN, D_MODEL, AB_IN), D_MODEL),
        'na_rpb': 0.1 * nrm(ks[11], (N_EVEN, NA_HEADS, 2 * NA_KH - 1, 2 * NA_KW - 1)),
        'sc_conv_w': lin(ks[12], (N_EVEN, 3, SC_WIDTH), 3),
        'ab_w_out': lin(ks[13], (N_EVEN, AB_OUT, D_MODEL), AB_OUT),
        'hy_w_in': lin(ks[14], (N_ODD, D_MODEL, 3 * HY_WIDTH), D_MODEL),
        'hy_short_w': lin(ks[15], (N_ODD, 3, 3 * HY_WIDTH), 3),
        'hy_f_w1': lin(ks[16], (N_ODD, HY_EMB, HY_FFN), HY_EMB),
        'hy_f_b1': 0.02 * nrm(ks[17], (N_ODD, HY_FFN)),
        'hy_f_w2': lin(ks[18], (N_ODD, HY_FFN, HY_FFN), HY_FFN),
        'hy_f_b2': 0.02 * nrm(ks[19], (N_ODD, HY_FFN)),
        'hy_f_w3': lin(ks[20], (N_ODD, HY_FFN, HY_FFN), HY_FFN),
        'hy_f_b3': 0.02 * nrm(ks[21], (N_ODD, HY_FFN)),
        'hy_f_w4': lin(ks[22], (N_ODD, HY_FFN, 2 * HY_WIDTH), HY_FFN),
        'hy_sin_freq': 1.0 + 0.1 * nrm(ks[23], (N_ODD, HY_FFN)),
        'hy_bias': 0.5 * nrm(ks[24], (N_ODD, HY_WIDTH)),
        'hy_w_out': lin(ks[25], (N_ODD, HY_WIDTH, D_MODEL), HY_WIDTH),
    }


def reference(x, c, ctx, c_ctx, w_mod, b_mod, norm_g, ffn_w_gate, ffn_w_up, ffn_w_down,
              ab_w_in, na_rpb, sc_conv_w, ab_w_out,
              hy_w_in, hy_short_w, hy_f_w1, hy_f_b1, hy_f_w2, hy_f_b2, hy_f_w3, hy_f_b3,
              hy_f_w4, hy_sin_freq, hy_bias, hy_w_out):
    b = x.shape[0]
    last_attn = (DEPTH - 1) - (DEPTH - 1) % 2
    for i in range(DEPTH):
        j = i // 2
        g = norm_g[i]
        m_lat = (jax.nn.silu(c) @ w_mod[i] + b_mod[i]).reshape(b, N_MOD, 1, D_MODEL)
        m_ctx = (jax.nn.silu(c_ctx) @ w_mod[i] + b_mod[i]).reshape(N_MOD, D_MODEL)
        keep_ctx = i <= last_attn
        upd_ctx = i < last_attn
        x = half_step_ffn(x, g[0], g[1], m_lat[:, 0], m_lat[:, 1], m_lat[:, 2],
                          ffn_w_gate[i, 0], ffn_w_up[i, 0], ffn_w_down[i, 0])
        h_lat = modulated_norm(x, g[2], m_lat[:, 3], m_lat[:, 4])
        h_ctx = None
        if keep_ctx:
            ctx = half_step_ffn(ctx, g[0], g[1], m_ctx[0], m_ctx[1], m_ctx[2],
                                ffn_w_gate[i, 0], ffn_w_up[i, 0], ffn_w_down[i, 0])
            h_ctx = modulated_norm(ctx, g[2], m_ctx[3], m_ctx[4])
        if i % 2 == 0:
            y_lat, y_ctx = mixer_ab(h_lat, h_ctx, ab_w_in[j], na_rpb[j], sc_conv_w[j], ab_w_out[j], upd_ctx)
        else:
            hy = lambda h: hyena_mixer(h, hy_w_in[j], hy_short_w[j], hy_f_w1[j], hy_f_b1[j], hy_f_w2[j],
                                       hy_f_b2[j], hy_f_w3[j], hy_f_b3[j], hy_f_w4[j], hy_sin_freq[j],
                                       hy_bias[j], hy_w_out[j])
            y_lat = hy(h_lat)
            y_ctx = hy(h_ctx) if upd_ctx else None
        x = x + m_lat[:, 5] * rms_norm(y_lat, g[3])
        x = half_step_ffn(x, g[4], g[5], m_lat[:, 6], m_lat[:, 7], m_lat[:, 8],
                          ffn_w_gate[i, 1], ffn_w_up[i, 1], ffn_w_down[i, 1])
        if upd_ctx:
            ctx = ctx + m_ctx[5] * rms_norm(y_ctx, g[3])
            ctx = half_step_ffn(ctx, g[4], g[5], m_ctx[6], m_ctx[7], m_ctx[8],
                                ffn_w_gate[i, 1], ffn_w_up[i, 1], ffn_w_down[i, 1])
    return x
```

```python
import functools
import math

import numpy as np
import jax
import jax.numpy as jnp
from jax import lax
from jax.experimental import pallas as pl
from jax.experimental.pallas import tpu as pltpu

F32 = jnp.float32
BF16 = jnp.bfloat16

GRID_W = 64
NA_HEADS = 8
NA_HEAD_DIM = 64
NA_WIDTH = NA_HEADS * NA_HEAD_DIM
NA_KH = 8
NA_KW = 16
FFN_RES = 0.5
N_MOD = 9
RMS_EPS = 1e-6
HY_EMB = 33
HY_BANDS = (HY_EMB - 1) // 2
HY_TARGET = 1e-2
HY_FAST_PCT = 0.3
HY_SLOW_PCT = 1.5

V7X_LANES = 128
V7X_SUBLANES = 8
V7X_MXU_DIM = 256
V7X_VMEM_BYTES = 64 * 1024 * 1024

ROW_TILE = 512
FFN_CHUNK = 256
PROJ_CHUNK = 512
NA_ROWS_PER_STEP = 8
DFT_RADIX = 128
DFT_LANES = 256
DFT_K1_PER_STEP = 8
FILTER_ROWS = 512
MASK_VALUE = -1e30


def _vmem_limit(nbytes):
    return int(min(max(nbytes, 32 * 1024 * 1024), V7X_VMEM_BYTES - 8 * 1024 * 1024))


def _params(semantics, vmem_bytes):
    return pltpu.CompilerParams(dimension_semantics=semantics, vmem_limit_bytes=_vmem_limit(vmem_bytes))


def _resident(shape):
    zeros = (0,) * len(shape)
    return pl.BlockSpec(shape, lambda *_: zeros, pipeline_mode=pl.Buffered(1))


def _rms(x):
    return x * lax.rsqrt(jnp.mean(x * x, axis=-1, keepdims=True) + RMS_EPS)


def _mdot(a, b):
    return jnp.dot(a, b, preferred_element_type=F32)


def _mod_kernel(c_ref, w_ref, b_ref, o_ref):
    c = c_ref[...]
    s = (c * jax.nn.sigmoid(c)).astype(BF16)
    o_ref[...] = _mdot(s, w_ref[...].astype(BF16)) + b_ref[...]


def _modulation(cc, w_mod, b_mod):
    depth, d, nd = w_mod.shape
    return pl.pallas_call(
        _mod_kernel,
        grid=(depth, nd // d),
        in_specs=[pl.BlockSpec((V7X_SUBLANES, d), lambda i, j: (0, 0)),
                  pl.BlockSpec((None, d, d), lambda i, j: (i, 0, j)),
                  pl.BlockSpec((None, 1, d), lambda i, j: (i, 0, j))],
        out_specs=pl.BlockSpec((None, V7X_SUBLANES, d), lambda i, j: (i, 0, j)),
        out_shape=jax.ShapeDtypeStruct((depth, V7X_SUBLANES, nd), F32),
        compiler_params=_params(("parallel", "parallel"), 4 * d * d * 4),
    )(cc, w_mod, b_mod.reshape(depth, 1, nd))


def _ffn_kernel(x_ref, m_ref, g_ref, wg_ref, wu_ref, wd_ref, o_ref, acc_ref, *, mrow, grow):
    x = x_ref[...]
    shift, scale, gate = m_ref[mrow:mrow + 1, :], m_ref[mrow + 1:mrow + 2, :], m_ref[mrow + 2:mrow + 3, :]
    h = ((_rms(x) * g_ref[grow:grow + 1, :]) * (1.0 + scale) + shift).astype(BF16)
    for j in range(wg_ref.shape[1] // FFN_CHUNK):
        c0, c1 = j * FFN_CHUNK, (j + 1) * FFN_CHUNK
        g = _mdot(h, wg_ref[:, c0:c1])
        u = _mdot(h, wu_ref[:, c0:c1])
        a = (g * jax.nn.sigmoid(g) * u).astype(BF16)
        part = _mdot(a, wd_ref[c0:c1, :])
        if j == 0:
            acc_ref[...] = part
        else:
            acc_ref[...] += part
    y = acc_ref[...]
    o_ref[...] = x + FFN_RES * gate * (_rms(y) * g_ref[grow + 1:grow + 2, :])


def _ffn(x, mod_i, g, wg, wu, wd, *, mrow, grow, batch_of):
    m, d = x.shape
    f = wg.shape[1]
    tm = min(ROW_TILE, m)
    kern = functools.partial(_ffn_kernel, mrow=mrow, grow=grow)
    vmem = 3 * d * f * 2 + 4 * tm * d * 4 + tm * d * 4 + 6 * tm * FFN_CHUNK * 4 + tm * d * 8
    return pl.pallas_call(
        kern,
        grid=(m // tm,),
        in_specs=[pl.BlockSpec((tm, d), lambda i: (i, 0)),
                  pl.BlockSpec((None, N_MOD, d), lambda i: (batch_of(i, tm), 0, 0)),
                  _resident(g.shape), _resident(wg.shape), _resident(wu.shape), _resident(wd.shape)],
        out_specs=pl.BlockSpec((tm, d), lambda i: (i, 0)),
        out_shape=jax.ShapeDtypeStruct((m, d), F32),
        scratch_shapes=[pltpu.VMEM((tm, d), F32)],
        compiler_params=_params(("parallel",), vmem),
    )(x, mod_i, g, wg, wu, wd)


def _inproj_kernel(x_ref, m_ref, g_ref, w_ref, *o_refs, mrow, grow, splits):
    x = x_ref[...]
    shift, scale = m_ref[mrow:mrow + 1, :], m_ref[mrow + 1:mrow + 2, :]
    h = ((_rms(x) * g_ref[grow:grow + 1, :]) * (1.0 + scale) + shift).astype(BF16)
    for o_ref, (c0, c1) in zip(o_refs, splits):
        for c in range(c0, c1, PROJ_CHUNK):
            o_ref[:, c - c0:c - c0 + PROJ_CHUNK] = _mdot(h, w_ref[:, c:c + PROJ_CHUNK]).astype(o_ref.dtype)


def _inproj(x, mod_i, g, w, *, mrow, grow, batch_of, splits, dtypes):
    m, d = x.shape
    tm = min(ROW_TILE, m)
    kern = functools.partial(_inproj_kernel, mrow=mrow, grow=grow, splits=splits)
    out_bytes = sum((c1 - c0) * jnp.dtype(dt).itemsize for (c0, c1), dt in zip(splits, dtypes))
    vmem = w.size * 2 + 2 * tm * d * 4 + 2 * tm * out_bytes + 4 * tm * d * 4
    return pl.pallas_call(
        kern,
        grid=(m // tm,),
        in_specs=[pl.BlockSpec((tm, d), lambda i: (i, 0)),
                  pl.BlockSpec((None, N_MOD, d), lambda i: (batch_of(i, tm), 0, 0)),
                  _resident(g.shape), _resident(w.shape)],
        out_specs=[pl.BlockSpec((tm, c1 - c0), lambda i: (i, 0)) for c0, c1 in splits],
        out_shape=[jax.ShapeDtypeStruct((m, c1 - c0), dt) for (c0, c1), dt in zip(splits, dtypes)],
        compiler_params=_params(("parallel",), vmem),
    )(x, mod_i, g, w)


def _split_heads_on_rows(q, low_lanes):
    zero = jnp.zeros_like(q)
    return jnp.concatenate([jnp.where(low_lanes, q, zero), jnp.where(low_lanes, zero, q)], axis=0)


def _na_kernel(q_ref, k_ref, v_ref, kc_ref, vc_ref, bt_ref, o_ref, *, rows):
    rb = pl.program_id(2)
    low_lanes = lax.broadcasted_iota(jnp.int32, (GRID_W, V7X_LANES), 1) < NA_HEAD_DIM
    kc = kc_ref[...]
    vc = vc_ref[...]
    nt = (((1,), (1,)), ((), ()))

    def one_row(i, carry):
        r = rb * NA_ROWS_PER_STEP + i
        rs = jnp.clip(r - NA_KH // 2, 0, rows - NA_KH)
        q = q_ref[pl.ds(pl.multiple_of(i * GRID_W, GRID_W), GRID_W), :] * (NA_HEAD_DIM ** -0.5)
        q2 = _split_heads_on_rows(q, low_lanes)
        k0 = pl.multiple_of(rs * GRID_W, GRID_W)
        kk = k_ref[pl.ds(k0, NA_KH * GRID_W), :]
        vv = v_ref[pl.ds(k0, NA_KH * GRID_W), :]
        s = lax.dot_general(q2, kk, nt, preferred_element_type=F32) + bt_ref[rs - r + (NA_KH - 1)]
        sc = lax.dot_general(q2, kc, nt, preferred_element_type=F32)
        mx = jnp.maximum(jnp.max(s, axis=-1, keepdims=True), jnp.max(sc, axis=-1, keepdims=True))
        p = jnp.exp(s - mx)
        pc = jnp.exp(sc - mx)
        den = jnp.sum(p, axis=-1, keepdims=True) + jnp.sum(pc, axis=-1, keepdims=True)
        o2 = (_mdot(p.astype(BF16), vv) + _mdot(pc.astype(BF16), vc)) / den
        o = jnp.where(low_lanes, o2[:GRID_W], o2[GRID_W:])
        o_ref[pl.ds(pl.multiple_of(i * GRID_W, GRID_W), GRID_W), :] = o.astype(o_ref.dtype)
        return carry

    lax.fori_loop(0, NA_ROWS_PER_STEP, one_row, 0)


def _na_bias_table(rpb):
    h = rpb.shape[0]
    c = jnp.arange(GRID_W)[:, None]
    kc = jnp.arange(GRID_W)[None, :]
    start = jnp.clip(c - NA_KW // 2, 0, GRID_W - NA_KW)
    valid = (kc >= start) & (kc < start + NA_KW)
    idx = jnp.clip(kc - c + (NA_KW - 1), 0, 2 * NA_KW - 2)
    dense = jnp.where(valid[None, None], rpb[:, :, idx], MASK_VALUE)
    off = jnp.arange(NA_KH)[:, None] + jnp.arange(NA_KH)[None, :]
    tab = dense[:, off]
    tab = tab.transpose(0, 1, 3, 2, 4).reshape(h, NA_KH, GRID_W, NA_KH * GRID_W)
    tab = tab.reshape(h // 2, 2, NA_KH, GRID_W, NA_KH * GRID_W).transpose(0, 2, 1, 3, 4)
    return tab.reshape(h // 2, NA_KH, 2 * GRID_W, NA_KH * GRID_W)


def _neighbourhood_attention(qkv, qkv_c, rpb, batch, seq, ctx_len):
    rows = seq // GRID_W
    assert rows >= NA_KH and rows % NA_ROWS_PER_STEP == 0
    nq = NA_WIDTH // V7X_LANES
    qkv3 = qkv.reshape(batch, seq, 3 * NA_WIDTH)
    qkvc3 = qkv_c.reshape(batch, ctx_len, 3 * NA_WIDTH)
    tq = NA_ROWS_PER_STEP * GRID_W
    table = _na_bias_table(rpb)
    vmem = 4 * seq * V7X_LANES * 2 + 2 * table[0].size * 4 + 16 * 1024 * 1024
    out = pl.pallas_call(
        functools.partial(_na_kernel, rows=rows),
        grid=(batch, nq, rows // NA_ROWS_PER_STEP),
        in_specs=[pl.BlockSpec((None, tq, V7X_LANES), lambda b, h, r: (b, r, h)),
                  pl.BlockSpec((None, seq, V7X_LANES), lambda b, h, r: (b, 0, nq + h)),
                  pl.BlockSpec((None, seq, V7X_LANES), lambda b, h, r: (b, 0, 2 * nq + h)),
                  pl.BlockSpec((None, ctx_len, V7X_LANES), lambda b, h, r: (b, 0, nq + h)),
                  pl.BlockSpec((None, ctx_len, V7X_LANES), lambda b, h, r: (b, 0, 2 * nq + h)),
                  pl.BlockSpec((None,) + table.shape[1:], lambda b, h, r: (h, 0, 0, 0))],
        out_specs=pl.BlockSpec((None, tq, V7X_LANES), lambda b, h, r: (b, r, h)),
        out_shape=jax.ShapeDtypeStruct((batch, seq, NA_WIDTH), BF16),
        compiler_params=_params(("parallel", "parallel", "arbitrary"), vmem),
    )(qkv3, qkv3, qkv3, qkvc3, qkvc3, table)
    return out.reshape(batch * seq, NA_WIDTH)


def _ctx_attn_kernel(q_ref, k_ref, v_ref, o_ref):
    n = q_ref.shape[0]
    low_lanes = lax.broadcasted_iota(jnp.int32, (n, V7X_LANES), 1) < NA_HEAD_DIM
    q2 = _split_heads_on_rows(q_ref[...] * (NA_HEAD_DIM ** -0.5), low_lanes)
    s = lax.dot_general(q2, k_ref[...], (((1,), (1,)), ((), ())), preferred_element_type=F32)
    p = jnp.exp(s - jnp.max(s, axis=-1, keepdims=True))
    o2 = _mdot(p.astype(BF16), v_ref[...]) / jnp.sum(p, axis=-1, keepdims=True)
    o_ref[...] = jnp.where(low_lanes, o2[:n], o2[n:]).astype(o_ref.dtype)


def _context_attention(qkv_c, batch, ctx_len):
    nq = NA_WIDTH // V7X_LANES
    qkvc3 = qkv_c.reshape(batch, ctx_len, 3 * NA_WIDTH)
    out = pl.pallas_call(
        _ctx_attn_kernel,
        grid=(batch, nq),
        in_specs=[pl.BlockSpec((None, ctx_len, V7X_LANES), lambda b, h: (b, 0, h)),
                  pl.BlockSpec((None, ctx_len, V7X_LANES), lambda b, h: (b, 0, nq + h)),
                  pl.BlockSpec((None, ctx_len, V7X_LANES), lambda b, h: (b, 0, 2 * nq + h))],
        out_specs=pl.BlockSpec((None, ctx_len, V7X_LANES), lambda b, h: (b, 0, h)),
        out_shape=jax.ShapeDtypeStruct((batch, ctx_len, NA_WIDTH), BF16),
        compiler_params=_params(("parallel", "parallel"), 0),
    )(qkvc3, qkvc3, qkvc3)
    return out.reshape(batch * ctx_len, NA_WIDTH)


def _conv3(p, prev_row, next_row, w0, w1, w2, block, seq_len):
    tm = p.shape[0]
    row = lax.broadcasted_iota(jnp.int32, p.shape, 0)
    pos = (block * tm + row) & (seq_len - 1)
    down = jnp.where(row == 0, prev_row, pltpu.roll(p, 1, 0))
    down = jnp.where(pos == 0, 0.0, down)
    up = jnp.where(row == tm - 1, next_row, pltpu.roll(p, tm - 1, 0))
    up = jnp.where(pos == seq_len - 1, 0.0, up)
    return down * w0 + p * w1 + up * w2


def _halo_specs(tm, m, width):
    blocks = m // V7X_SUBLANES
    per = tm // V7X_SUBLANES
    prev = pl.BlockSpec((V7X_SUBLANES, width), lambda i: (jnp.maximum(i * per - 1, 0), 0))
    nxt = pl.BlockSpec((V7X_SUBLANES, width), lambda i: (jnp.minimum((i + 1) * per, blocks - 1), 0))
    return prev, nxt


def _outproj_ab_kernel(a_ref, gcx_ref, prev_ref, next_ref, cw_ref, w_ref, x_ref, m_ref, g_ref, o_ref,
                       *, seq_len, mrow, grow):
    na = a_ref.shape[1]
    sw = cw_ref.shape[1]
    last = V7X_SUBLANES - 1
    p = gcx_ref[:, sw:2 * sw] * gcx_ref[:, 2 * sw:3 * sw]
    p_prev = prev_ref[last:last + 1, sw:2 * sw] * prev_ref[last:last + 1, 2 * sw:3 * sw]
    p_next = next_ref[0:1, sw:2 * sw] * next_ref[0:1, 2 * sw:3 * sw]
    conv = _conv3(p, p_prev, p_next, cw_ref[0:1, :], cw_ref[1:2, :], cw_ref[2:3, :], pl.program_id(0), seq_len)
    b = (gcx_ref[:, 0:sw] * conv).astype(BF16)
    y = _mdot(a_ref[...], w_ref[0:na, :]) + _mdot(b, w_ref[na:, :])
    o_ref[...] = x_ref[...] + m_ref[mrow:mrow + 1, :] * (_rms(y) * g_ref[grow:grow + 1, :])


def _outproj_plain_kernel(z_ref, w_ref, x_ref, m_ref, g_ref, o_ref, *, mrow, grow):
    y = _mdot(z_ref[...].astype(BF16), w_ref[...])
    o_ref[...] = x_ref[...] + m_ref[mrow:mrow + 1, :] * (_rms(y) * g_ref[grow:grow + 1, :])


def _outproj_ab(a, gcx, conv_w, w, x, mod_i, g, *, seq_len, mrow, grow, batch_of):
    m, d = x.shape
    tm = min(ROW_TILE, m)
    assert seq_len & (seq_len - 1) == 0
    prev, nxt = _halo_specs(tm, m, gcx.shape[1])
    kern = functools.partial(_outproj_ab_kernel, seq_len=seq_len, mrow=mrow, grow=grow)
    vmem = 2 * tm * (a.shape[1] * 2 + gcx.shape[1] * 4 + 2 * d * 4) + w.size * 2 + 8 * tm * d * 4
    return pl.pallas_call(
        kern,
        grid=(m // tm,),
        in_specs=[pl.BlockSpec((tm, a.shape[1]), lambda i: (i, 0)),
                  pl.BlockSpec((tm, gcx.shape[1]), lambda i: (i, 0)),
                  prev, nxt, _resident(conv_w.shape), _resident(w.shape),
                  pl.BlockSpec((tm, d), lambda i: (i, 0)),
                  pl.BlockSpec((None, N_MOD, d), lambda i: (batch_of(i, tm), 0, 0)),
                  _resident(g.shape)],
        out_specs=pl.BlockSpec((tm, d), lambda i: (i, 0)),
        out_shape=jax.ShapeDtypeStruct((m, d), F32),
        compiler_params=_params(("parallel",), vmem),
    )(a, gcx, gcx, gcx, conv_w, w, x, mod_i, g)


def _outproj_plain(z, w, x, mod_i, g, *, mrow, grow, batch_of):
    m, d = x.shape
    tm = min(ROW_TILE, m)
    kern = functools.partial(_outproj_plain_kernel, mrow=mrow, grow=grow)
    vmem = 2 * tm * (z.shape[1] * 4 + 2 * d * 4) + w.size * 2 + 8 * tm * d * 4
    return pl.pallas_call(
        kern,
        grid=(m // tm,),
        in_specs=[pl.BlockSpec((tm, z.shape[1]), lambda i: (i, 0)),
                  _resident(w.shape),
                  pl.BlockSpec((tm, d), lambda i: (i, 0)),
                  pl.BlockSpec((None, N_MOD, d), lambda i: (batch_of(i, tm), 0, 0)),
                  _resident(g.shape)],
        out_specs=pl.BlockSpec((tm, d), lambda i: (i, 0)),
        out_shape=jax.ShapeDtypeStruct((m, d), F32),
        compiler_params=_params(("parallel",), vmem),
    )(z, w, x, mod_i, g)


def _hyena_pre_kernel(pj_ref, prev_ref, next_ref, w_ref, x0_ref, u_ref, *, seq_len):
    c = x0_ref.shape[1]
    last = V7X_SUBLANES - 1
    blk = pl.program_id(0)

    def part(k):
        cols = slice(k * c, (k + 1) * c)
        return _conv3(pj_ref[:, cols], prev_ref[last:last + 1, cols], next_ref[0:1, cols],
                      w_ref[0:1, cols], w_ref[1:2, cols], w_ref[2:3, cols], blk, seq_len)

    x0_ref[...] = part(0)
    u_ref[...] = part(2) * part(1)


def _hyena_pre(proj, short_w, seq_len):
    m, w3 = proj.shape
    c = w3 // 3
    tm = min(ROW_TILE // 2, m)
    assert seq_len & (seq_len - 1) == 0
    prev, nxt = _halo_specs(tm, m, w3)
    return pl.pallas_call(
        functools.partial(_hyena_pre_kernel, seq_len=seq_len),
        grid=(m // tm,),
        in_specs=[pl.BlockSpec((tm, w3), lambda i: (i, 0)), prev, nxt, _resident(short_w.shape)],
        out_specs=[pl.BlockSpec((tm, c), lambda i: (i, 0))] * 2,
        out_shape=[jax.ShapeDtypeStruct((m, c), F32)] * 2,
        compiler_params=_params(("parallel",), 2 * tm * w3 * 4 + 4 * tm * c * 4 + 8 * tm * c * 4),
    )(proj, proj, proj, short_w)


def _filter_kernel(z_ref, w1_ref, b1_ref, w2_ref, b2_ref, w3_ref, b3_ref, w4_ref, fr_ref, dec_ref, o_ref,
                   *, length):
    hi = lax.Precision.HIGHEST
    z = z_ref[...]
    fr = fr_ref[...]
    h = jnp.sin(fr * (jnp.dot(z, w1_ref[...], precision=hi, preferred_element_type=F32) + b1_ref[...]))
    h = jnp.sin(fr * (jnp.dot(h, w2_ref[...], precision=hi, preferred_element_type=F32) + b2_ref[...]))
    h = jnp.sin(fr * (jnp.dot(h, w3_ref[...], precision=hi, preferred_element_type=F32) + b3_ref[...]))
    filt = jnp.dot(h, w4_ref[...], precision=hi, preferred_element_type=F32)
    out = filt * jnp.exp(-z[:, 0:1] * dec_ref[...])
    row = pl.program_id(0) * z.shape[0] + lax.broadcasted_iota(jnp.int32, out.shape, 0)
    o_ref[...] = jnp.where(row == length, 0.0, out)


@functools.lru_cache(maxsize=None)
def _filter_positions(length):
    t = np.linspace(0.0, 1.0, length)[:, None]
    w = 2.0 * math.pi * np.arange(length)[:, None] / length
    bands = np.linspace(1e-4, HY_BANDS - 1, HY_BANDS)[None, :]
    z = np.concatenate([t, np.cos(bands * w), -np.sin(bands * w)], axis=-1)
    z2 = np.concatenate([z, z[0:1], z[:0:-1]], axis=0)
    out = np.zeros((2 * length, V7X_LANES), np.float32)
    out[:, :HY_EMB] = z2
    return out


def _hyena_filter(length, w1, b1, w2, b2, w3, b3, w4, freq):
    c = w4.shape[1] // 2
    ffn = w2.shape[0]
    tr = min(FILTER_ROWS, length)
    per_half = length // tr
    zpos = jnp.asarray(_filter_positions(length))
    w1p = jnp.pad(w1, ((0, V7X_LANES - w1.shape[0]), (0, 0)))
    decay = np.abs(np.linspace(math.log(HY_TARGET) / HY_SLOW_PCT, math.log(HY_TARGET) / HY_FAST_PCT, c))
    decay = jnp.asarray(decay[None, :], F32)
    row = lambda v: v.reshape(1, -1)
    small = lambda shape: pl.BlockSpec(shape, lambda i: (0, 0))
    return pl.pallas_call(
        functools.partial(_filter_kernel, length=length),
        grid=(2 * per_half,),
        in_specs=[pl.BlockSpec((tr, V7X_LANES), lambda i: (i, 0)),
                  small((V7X_LANES, ffn)), small((1, ffn)), small((ffn, ffn)), small((1, ffn)),
                  small((ffn, ffn)), small((1, ffn)),
                  pl.BlockSpec((ffn, c), lambda i: (0, i // per_half)),
                  small((1, ffn)), small((1, c))],
        out_specs=pl.BlockSpec((tr, c), lambda i: (i, 0)),
        out_shape=jax.ShapeDtypeStruct((2 * length, c), F32),
        compiler_params=_params(("parallel",), 0),
    )(zpos, w1p, row(b1), w2, row(b2), w3, row(b3), w4, row(freq), decay)


def _complex_as_real(m):
    return np.block([[m.real, -m.imag], [m.imag, m.real]])


@functools.lru_cache(maxsize=None)
def _dft_tables(length):
    n = 2 * length
    n1 = n // DFT_RADIX
    half = n1 // 2
    eye = np.eye(V7X_SUBLANES)
    k1 = np.arange(n1)
    f1 = np.exp(-2j * np.pi * np.outer(k1, k1) / n1)
    first_data = np.kron(_complex_as_real(f1[:, :half]), eye)
    first_filt = np.kron(np.concatenate([f1.real, f1.imag], axis=0), eye)
    inv1 = np.exp(2j * np.pi * np.outer(np.arange(half), k1) / n1) / n
    last = np.kron(_complex_as_real(inv1), eye)
    i2 = np.arange(DFT_RADIX)
    phase = (np.outer(i2, i2)[None] / DFT_RADIX + (i2[None, None, :] * k1[:, None, None]) / n)
    second = np.exp(-2j * np.pi * phase)
    fwd = np.stack([_complex_as_real(second[k]) for k in range(n1)])
    inv = np.stack([_complex_as_real(np.conj(second[k]).T) for k in range(n1)])
    return tuple(np.asarray(a, np.float32) for a in (first_data, first_filt, last, fwd, inv))


def _kron_kernel(mat_ref, x_ref, o_ref):
    lanes = x_ref.shape[-1]
    x = x_ref[...].reshape(-1, lanes).astype(BF16)
    o_ref[...] = _mdot(mat_ref[...], x).reshape(o_ref.shape)


def _kron_out_kernel(mat_ref, b_ref, x0_ref, u_ref, bias_ref, o_ref):
    lanes = b_ref.shape[-1]
    y = _mdot(mat_ref[...], b_ref[...].reshape(-1, lanes).astype(BF16)).reshape(o_ref.shape)
    o_ref[...] = x0_ref[...] * (y + u_ref[...] * bias_ref[...])


def _second_stage_kernel(a_ref, g_ref, gi_ref, kf_ref, o_ref):
    lanes = a_ref.shape[-1]
    r = DFT_RADIX
    for j in range(a_ref.shape[1]):
        spec = _mdot(g_ref[j], a_ref[:, j].reshape(2 * r, lanes).astype(BF16))
        xr, xi = spec[:r], spec[r:]
        kr, ki = kf_ref[0, j], kf_ref[1, j]
        prod = jnp.concatenate([xr * kr - xi * ki, xr * ki + xi * kr], axis=0).astype(BF16)
        o_ref[:, j] = _mdot(gi_ref[j], prod).reshape(2, r, lanes)


def _second_stage_filter_kernel(a_ref, g_ref, o_ref):
    lanes = a_ref.shape[-1]
    for j in range(a_ref.shape[1]):
        o_ref[:, j] = _mdot(g_ref[j], a_ref[:, j].reshape(2 * DFT_RADIX, lanes).astype(BF16)).reshape(
            2, DFT_RADIX, lanes)


def _long_conv_gate(u, x0, kfilt, bias, batch, length):
    c = u.shape[1]
    pairs = batch // 2
    n1 = 2 * length // DFT_RADIX
    half = n1 // 2
    r, s8, cb, kb = DFT_RADIX, V7X_SUBLANES, min(DFT_LANES, c), DFT_K1_PER_STEP
    assert batch % 2 == 0 and n1 % kb == 0 and c % cb == 0
    first_data, first_filt, last, fwd, inv = (jnp.asarray(t, BF16) for t in _dft_tables(length))
    groups = r // s8
    big = 48 * 1024 * 1024

    u5 = u.reshape(pairs, 2, half, r, c)
    x05 = x0.reshape(pairs, 2, half, r, c)
    a_data = pl.pallas_call(
        _kron_kernel,
        grid=(pairs, groups, c // cb),
        in_specs=[_resident(first_data.shape),
                  pl.BlockSpec((None, 2, half, s8, cb), lambda p, g, l: (p, 0, 0, g, l))],
        out_specs=pl.BlockSpec((None, 2, n1, s8, cb), lambda p, g, l: (p, 0, 0, g, l)),
        out_shape=jax.ShapeDtypeStruct((pairs, 2, n1, r, c), F32),
        compiler_params=_params(("parallel", "parallel", "parallel"), big),
    )(first_data, u5)

    a_filt = pl.pallas_call(
        _kron_kernel,
        grid=(groups, c // cb),
        in_specs=[_resident(first_filt.shape),
                  pl.BlockSpec((n1, s8, cb), lambda g, l: (0, g, l))],
        out_specs=pl.BlockSpec((2, n1, s8, cb), lambda g, l: (0, 0, g, l)),
        out_shape=jax.ShapeDtypeStruct((2, n1, r, c), F32),
        compiler_params=_params(("parallel", "parallel"), big),
    )(first_filt, kfilt.reshape(n1, r, c))

    k_spec = pl.pallas_call(
        _second_stage_filter_kernel,
        grid=(c // cb, n1 // kb),
        in_specs=[pl.BlockSpec((2, kb, r, cb), lambda l, k: (0, k, 0, l)),
                  pl.BlockSpec((kb, 2 * r, 2 * r), lambda l, k: (k, 0, 0))],
        out_specs=pl.BlockSpec((2, kb, r, cb), lambda l, k: (0, k, 0, l)),
        out_shape=jax.ShapeDtypeStruct((2, n1, r, c), F32),
        compiler_params=_params(("parallel", "parallel"), big),
    )(a_filt, fwd)

    b_data = pl.pallas_call(
        _second_stage_kernel,
        grid=(c // cb, n1 // kb, pairs),
        in_specs=[pl.BlockSpec((None, 2, kb, r, cb), lambda l, k, p: (p, 0, k, 0, l)),
                  pl.BlockSpec((kb, 2 * r, 2 * r), lambda l, k, p: (k, 0, 0)),
                  pl.BlockSpec((kb, 2 * r, 2 * r), lambda l, k, p: (k, 0, 0)),
                  pl.BlockSpec((2, kb, r, cb), lambda l, k, p: (0, k, 0, l))],
        out_specs=pl.BlockSpec((None, 2, kb, r, cb), lambda l, k, p: (p, 0, k, 0, l)),
        out_shape=jax.ShapeDtypeStruct((pairs, 2, n1, r, c), F32),
        compiler_params=_params(("parallel", "parallel", "arbitrary"), big),
    )(a_data, fwd, inv, k_spec)

    z = pl.pallas_call(
        _kron_out_kernel,
        grid=(pairs, groups, c // cb),
        in_specs=[_resident(last.shape),
                  pl.BlockSpec((None, 2, n1, s8, cb), lambda p, g, l: (p, 0, 0, g, l)),
                  pl.BlockSpec((None, 2, half, s8, cb), lambda p, g, l: (p, 0, 0, g, l)),
                  pl.BlockSpec((None, 2, half, s8, cb), lambda p, g, l: (p, 0, 0, g, l)),
                  pl.BlockSpec((1, cb), lambda p, g, l: (0, l))],
        out_specs=pl.BlockSpec((None, 2, half, s8, cb), lambda p, g, l: (p, 0, 0, g, l)),
        out_shape=jax.ShapeDtypeStruct((pairs, 2, half, r, c), F32),
        compiler_params=_params(("parallel", "parallel", "parallel"), big),
    )(last, b_data, x05, u5, bias.reshape(1, c))
    return z.reshape(batch * length, c)


@functools.lru_cache(maxsize=None)
def _dense_dft_tables(length):
    n = 2 * length
    idx = np.arange(n)
    f = np.exp(-2j * np.pi * np.outer(idx, idx) / n)
    fwd_data = _complex_as_real(f[:, :length])
    fwd_filt = np.concatenate([f.real, f.imag], axis=0)
    inv = _complex_as_real(np.conj(f[:length, :]) / n)
    return tuple(np.asarray(a, np.float32) for a in (fwd_data, fwd_filt, inv))


def _short_conv_gate_kernel(fd_ref, ff_ref, inv_ref, u_ref, x0_ref, k_ref, bias_ref, o_ref):
    n = k_ref.shape[0]
    u = u_ref[...]
    spec = _mdot(fd_ref[...], u.astype(BF16))
    kf = _mdot(ff_ref[...], k_ref[...].astype(BF16))
    xr, xi, kr, ki = spec[:n], spec[n:], kf[:n], kf[n:]
    prod = jnp.concatenate([xr * kr - xi * ki, xr * ki + xi * kr], axis=0).astype(BF16)
    o_ref[...] = x0_ref[...] * (_mdot(inv_ref[...], prod) + u * bias_ref[...])


def _long_conv_gate_short(u, x0, kfilt, bias, batch, length):
    c = u.shape[1]
    pairs = batch // 2
    cb = min(DFT_LANES, c)
    fwd_data, fwd_filt, inv = (jnp.asarray(t, BF16) for t in _dense_dft_tables(length))
    pair_spec = pl.BlockSpec((None, 2 * length, cb), lambda p, l: (p, 0, l))
    z = pl.pallas_call(
        _short_conv_gate_kernel,
        grid=(pairs, c // cb),
        in_specs=[_resident(fwd_data.shape), _resident(fwd_filt.shape), _resident(inv.shape),
                  pair_spec, pair_spec,
                  pl.BlockSpec((2 * length, cb), lambda p, l: (0, l)),
                  pl.BlockSpec((1, cb), lambda p, l: (0, l))],
        out_specs=pair_spec,
        out_shape=jax.ShapeDtypeStruct((pairs, 2 * length, c), F32),
        compiler_params=_params(("parallel", "parallel"), 0),
    )(fwd_data, fwd_filt, inv, u.reshape(pairs, 2 * length, c), x0.reshape(pairs, 2 * length, c),
      kfilt, bias.reshape(1, c))
    return z.reshape(batch * length, c)


def kernel(x, c, ctx, c_ctx, w_mod, b_mod, norm_g, ffn_w_gate, ffn_w_up, ffn_w_down, ab_w_in, na_rpb,
           sc_conv_w, ab_w_out, hy_w_in, hy_short_w, hy_f_w1, hy_f_b1, hy_f_w2, hy_f_b2, hy_f_w3, hy_f_b3,
           hy_f_w4, hy_sin_freq, hy_bias, hy_w_out):
    batch, seq, d = x.shape
    ctx_len = ctx.shape[1]
    depth = w_mod.shape[0]
    assert batch + 1 <= V7X_SUBLANES and seq % ROW_TILE == 0
    last_attn = (depth - 1) - (depth - 1) % 2
    sc_width = sc_conv_w.shape[-1]

    cc = jnp.zeros((V7X_SUBLANES, d), F32).at[:batch].set(c).at[batch].set(c_ctx)
    mod = _modulation(cc, w_mod, b_mod).reshape(depth, V7X_SUBLANES, N_MOD, d)
    lat_batch = lambda i, tm: i // (seq // tm)
    ctx_batch = lambda i, tm: batch

    xs = x.reshape(batch * seq, d)
    cs = ctx.reshape(batch * ctx_len, d)
    qkv_cols = (0, 3 * NA_WIDTH)
    gcx_cols = (3 * NA_WIDTH, 3 * NA_WIDTH + 3 * sc_width)

    for i in range(depth):
        j = i // 2
        g = norm_g[i]
        m_i = mod[i]
        keep_ctx = i <= last_attn
        upd_ctx = i < last_attn
        ffn_w = lambda k: (ffn_w_gate[i, k].astype(BF16), ffn_w_up[i, k].astype(BF16),
                           ffn_w_down[i, k].astype(BF16))

        w1 = ffn_w(0)
        xs = _ffn(xs, m_i, g, *w1, mrow=0, grow=0, batch_of=lat_batch)
        if keep_ctx:
            cs = _ffn(cs, m_i, g, *w1, mrow=0, grow=0, batch_of=ctx_batch)

        if i % 2 == 0:
            w_in = ab_w_in[j].astype(BF16)
            w_out = ab_w_out[j].astype(BF16)
            qkv, gcx = _inproj(xs, m_i, g, w_in, mrow=3, grow=2, batch_of=lat_batch,
                               splits=(qkv_cols, gcx_cols), dtypes=(BF16, F32))
            if upd_ctx:
                qkv_c, gcx_c = _inproj(cs, m_i, g, w_in, mrow=3, grow=2, batch_of=ctx_batch,
                                       splits=(qkv_cols, gcx_cols), dtypes=(BF16, F32))
            else:
                (qkv_c,) = _inproj(cs, m_i, g, w_in, mrow=3, grow=2, batch_of=ctx_batch,
                                   splits=(qkv_cols,), dtypes=(BF16,))
            a_lat = _neighbourhood_attention(qkv, qkv_c, na_rpb[j], batch, seq, ctx_len)
            xs = _outproj_ab(a_lat, gcx, sc_conv_w[j], w_out, xs, m_i, g, seq_len=seq, mrow=5, grow=3,
                             batch_of=lat_batch)
            if upd_ctx:
                a_ctx = _context_attention(qkv_c, batch, ctx_len)
                cs = _outproj_ab(a_ctx, gcx_c, sc_conv_w[j], w_out, cs, m_i, g, seq_len=ctx_len, mrow=5,
                                 grow=3, batch_of=ctx_batch)
        else:
            w_in = hy_w_in[j].astype(BF16)
            w_out = hy_w_out[j].astype(BF16)
            width = w_in.shape[1]
            filt = lambda n: _hyena_filter(n, hy_f_w1[j], hy_f_b1[j], hy_f_w2[j], hy_f_b2[j], hy_f_w3[j],
                                           hy_f_b3[j], hy_f_w4[j], hy_sin_freq[j])
            (proj,) = _inproj(xs, m_i, g, w_in, mrow=3, grow=2, batch_of=lat_batch,
                              splits=((0, width),), dtypes=(F32,))
            x0, u = _hyena_pre(proj, hy_short_w[j], seq)
            z = _long_conv_gate(u, x0, filt(seq), hy_bias[j], batch, seq)
            xs = _outproj_plain(z, w_out, xs, m_i, g, mrow=5, grow=3, batch_of=lat_batch)
            if upd_ctx:
                (proj_c,) = _inproj(cs, m_i, g, w_in, mrow=3, grow=2, batch_of=ctx_batch,
                                    splits=((0, width),), dtypes=(F32,))
                x0_c, u_c = _hyena_pre(proj_c, hy_short_w[j], ctx_len)
                z_c = _long_conv_gate_short(u_c, x0_c, filt(ctx_len), hy_bias[j], batch, ctx_len)
                cs = _outproj_plain(z_c, w_out, cs, m_i, g, mrow=5, grow=3, batch_of=ctx_batch)

        w2 = ffn_w(1)
        xs = _ffn(xs, m_i, g, *w2, mrow=6, grow=4, batch_of=lat_batch)
        if upd_ctx:
            cs = _ffn(cs, m_i, g, *w2, mrow=6, grow=4, batch_of=ctx_batch)

    return xs.reshape(batch, seq, d)
```

```python
import functools
import math

import numpy as np
import jax
import jax.numpy as jnp
from jax import lax
from jax.experimental import pallas as pl
from jax.experimental.pallas import tpu as pltpu

F32 = jnp.float32
BF16 = jnp.bfloat16

GRID_W = 64
NA_HEADS = 8
NA_HEAD_DIM = 64
NA_WIDTH = NA_HEADS * NA_HEAD_DIM
NA_KH = 8
NA_KW = 16
FFN_RES = 0.5
N_MOD = 9
RMS_EPS = 1e-6
HY_EMB = 33
HY_BANDS = (HY_EMB - 1) // 2
HY_TARGET = 1e-2
HY_FAST_PCT = 0.3
HY_SLOW_PCT = 1.5

V7X_LANES = 128
V7X_SUBLANES = 8
V7X_MXU_DIM = 256
V7X_VMEM_BYTES = 64 * 1024 * 1024

ROW_TILE = 512
FFN_CHUNK = 256
PROJ_CHUNK = 512
NA_ROWS_PER_STEP = 8
DFT_RADIX = 128
DFT_LANES = 256
DFT_K1_PER_STEP = 8
FILTER_ROWS = 512
MASK_VALUE = -1e30


def _vmem_limit(nbytes):
    return int(min(max(nbytes, 32 * 1024 * 1024), V7X_VMEM_BYTES - 8 * 1024 * 1024))


def _params(semantics, vmem_bytes):
    return pltpu.CompilerParams(dimension_semantics=semantics, vmem_limit_bytes=_vmem_limit(vmem_bytes))


def _resident(shape):
    zeros = (0,) * len(shape)
    return pl.BlockSpec(shape, lambda *_: zeros, pipeline_mode=pl.Buffered(1))


def _rms(x):
    return x * lax.rsqrt(jnp.mean(x * x, axis=-1, keepdims=True) + RMS_EPS)


def _mdot(a, b):
    return jnp.dot(a, b, preferred_element_type=F32)


def _mod_kernel(c_ref, w_ref, b_ref, o_ref):
    c = c_ref[...]
    s = (c * jax.nn.sigmoid(c)).astype(BF16)
    o_ref[...] = _mdot(s, w_ref[...].astype(BF16)) + b_ref[...]


def _modulation(cc, w_mod, b_mod):
    depth, d, nd = w_mod.shape
    return pl.pallas_call(
        _mod_kernel,
        grid=(depth, nd // d),
        in_specs=[pl.BlockSpec((V7X_SUBLANES, d), lambda i, j: (0, 0)),
                  pl.BlockSpec((None, d, d), lambda i, j: (i, 0, j)),
                  pl.BlockSpec((None, 1, d), lambda i, j: (i, 0, j))],
        out_specs=pl.BlockSpec((None, V7X_SUBLANES, d), lambda i, j: (i, 0, j)),
        out_shape=jax.ShapeDtypeStruct((depth, V7X_SUBLANES, nd), F32),
        name="modulation",
        compiler_params=_params(("parallel", "parallel"), 4 * d * d * 4),
    )(cc, w_mod, b_mod.reshape(depth, 1, nd))


def _ffn_kernel(x_ref, m_ref, g_ref, wg_ref, wu_ref, wd_ref, o_ref, acc_ref, *, mrow, grow):
    x = x_ref[...]
    shift, scale, gate = m_ref[mrow:mrow + 1, :], m_ref[mrow + 1:mrow + 2, :], m_ref[mrow + 2:mrow + 3, :]
    h = ((_rms(x) * g_ref[grow:grow + 1, :]) * (1.0 + scale) + shift).astype(BF16)
    for j in range(wg_ref.shape[1] // FFN_CHUNK):
        c0, c1 = j * FFN_CHUNK, (j + 1) * FFN_CHUNK
        g = _mdot(h, wg_ref[:, c0:c1])
        u = _mdot(h, wu_ref[:, c0:c1])
        a = (g * jax.nn.sigmoid(g) * u).astype(BF16)
        part = _mdot(a, wd_ref[c0:c1, :])
        if j == 0:
            acc_ref[...] = part
        else:
            acc_ref[...] += part
    y = acc_ref[...]
    o_ref[...] = x + FFN_RES * gate * (_rms(y) * g_ref[grow + 1:grow + 2, :])


def _ffn(x, mod_i, g, wg, wu, wd, *, mrow, grow, batch_of):
    m, d = x.shape
    f = wg.shape[1]
    tm = min(ROW_TILE, m)
    kern = functools.partial(_ffn_kernel, mrow=mrow, grow=grow)
    vmem = 3 * d * f * 2 + 4 * tm * d * 4 + tm * d * 4 + 6 * tm * FFN_CHUNK * 4 + tm * d * 8
    return pl.pallas_call(
        kern,
        grid=(m // tm,),
        in_specs=[pl.BlockSpec((tm, d), lambda i: (i, 0)),
                  pl.BlockSpec((None, N_MOD, d), lambda i: (batch_of(i, tm), 0, 0)),
                  _resident(g.shape), _resident(wg.shape), _resident(wu.shape), _resident(wd.shape)],
        out_specs=pl.BlockSpec((tm, d), lambda i: (i, 0)),
        out_shape=jax.ShapeDtypeStruct((m, d), F32),
        scratch_shapes=[pltpu.VMEM((tm, d), F32)],
        name="ffn",
        compiler_params=_params(("parallel",), vmem),
    )(x, mod_i, g, wg, wu, wd)


def _inproj_kernel(x_ref, m_ref, g_ref, w_ref, *o_refs, mrow, grow, splits):
    x = x_ref[...]
    shift, scale = m_ref[mrow:mrow + 1, :], m_ref[mrow + 1:mrow + 2, :]
    h = ((_rms(x) * g_ref[grow:grow + 1, :]) * (1.0 + scale) + shift).astype(BF16)
    for o_ref, (c0, c1) in zip(o_refs, splits):
        for c in range(c0, c1, PROJ_CHUNK):
            o_ref[:, c - c0:c - c0 + PROJ_CHUNK] = _mdot(h, w_ref[:, c:c + PROJ_CHUNK]).astype(o_ref.dtype)


def _inproj(x, mod_i, g, w, *, mrow, grow, batch_of, splits, dtypes):
    m, d = x.shape
    tm = min(ROW_TILE, m)
    kern = functools.partial(_inproj_kernel, mrow=mrow, grow=grow, splits=splits)
    out_bytes = sum((c1 - c0) * jnp.dtype(dt).itemsize for (c0, c1), dt in zip(splits, dtypes))
    vmem = w.size * 2 + 2 * tm * d * 4 + 2 * tm * out_bytes + 4 * tm * d * 4
    return pl.pallas_call(
        kern,
        grid=(m // tm,),
        in_specs=[pl.BlockSpec((tm, d), lambda i: (i, 0)),
                  pl.BlockSpec((None, N_MOD, d), lambda i: (batch_of(i, tm), 0, 0)),
                  _resident(g.shape), _resident(w.shape)],
        out_specs=[pl.BlockSpec((tm, c1 - c0), lambda i: (i, 0)) for c0, c1 in splits],
        out_shape=[jax.ShapeDtypeStruct((m, c1 - c0), dt) for (c0, c1), dt in zip(splits, dtypes)],
        name="inproj",
        compiler_params=_params(("parallel",), vmem),
    )(x, mod_i, g, w)


def _split_heads_on_rows(q, low_lanes):
    zero = jnp.zeros_like(q)
    return jnp.concatenate([jnp.where(low_lanes, q, zero), jnp.where(low_lanes, zero, q)], axis=0)


def _na_kernel(q_ref, k_ref, v_ref, kc_ref, vc_ref, bt_ref, o_ref, *, rows):
    rb = pl.program_id(2)
    low_lanes = lax.broadcasted_iota(jnp.int32, (GRID_W, V7X_LANES), 1) < NA_HEAD_DIM
    kc = kc_ref[...]
    vc = vc_ref[...]
    nt = (((1,), (1,)), ((), ()))

    for i in range(NA_ROWS_PER_STEP):
        r = rb * NA_ROWS_PER_STEP + i
        rs = jnp.clip(r - NA_KH // 2, 0, rows - NA_KH)
        q = q_ref[i * GRID_W:(i + 1) * GRID_W, :] * (NA_HEAD_DIM ** -0.5)
        q2 = _split_heads_on_rows(q, low_lanes)
        k0 = pl.multiple_of(rs * GRID_W, GRID_W)
        kk = k_ref[pl.ds(k0, NA_KH * GRID_W), :]
        vv = v_ref[pl.ds(k0, NA_KH * GRID_W), :]
        s = lax.dot_general(q2, kk, nt, preferred_element_type=F32) + bt_ref[rs - r + (NA_KH - 1)]
        sc = lax.dot_general(q2, kc, nt, preferred_element_type=F32)
        mx = jnp.maximum(jnp.max(s, axis=-1, keepdims=True), jnp.max(sc, axis=-1, keepdims=True))
        p = jnp.exp(s - mx)
        pc = jnp.exp(sc - mx)
        den = jnp.sum(p, axis=-1, keepdims=True) + jnp.sum(pc, axis=-1, keepdims=True)
        o2 = (_mdot(p.astype(BF16), vv) + _mdot(pc.astype(BF16), vc)) / den
        o = jnp.where(low_lanes, o2[:GRID_W], o2[GRID_W:])
        o_ref[i * GRID_W:(i + 1) * GRID_W, :] = o.astype(o_ref.dtype)


def _na_bias_table(rpb):
    h = rpb.shape[0]
    c = jnp.arange(GRID_W)[:, None]
    kc = jnp.arange(GRID_W)[None, :]
    start = jnp.clip(c - NA_KW // 2, 0, GRID_W - NA_KW)
    valid = (kc >= start) & (kc < start + NA_KW)
    idx = jnp.clip(kc - c + (NA_KW - 1), 0, 2 * NA_KW - 2)
    dense = jnp.where(valid[None, None], rpb[:, :, idx], MASK_VALUE)
    off = jnp.arange(NA_KH)[:, None] + jnp.arange(NA_KH)[None, :]
    tab = dense[:, off]
    tab = tab.transpose(0, 1, 3, 2, 4).reshape(h, NA_KH, GRID_W, NA_KH * GRID_W)
    tab = tab.reshape(h // 2, 2, NA_KH, GRID_W, NA_KH * GRID_W).transpose(0, 2, 1, 3, 4)
    return tab.reshape(h // 2, NA_KH, 2 * GRID_W, NA_KH * GRID_W)


def _neighbourhood_attention(qkv, qkv_c, rpb, batch, seq, ctx_len):
    rows = seq // GRID_W
    assert rows >= NA_KH and rows % NA_ROWS_PER_STEP == 0
    nq = NA_WIDTH // V7X_LANES
    qkv3 = qkv.reshape(batch, seq, 3 * NA_WIDTH)
    qkvc3 = qkv_c.reshape(batch, ctx_len, 3 * NA_WIDTH)
    tq = NA_ROWS_PER_STEP * GRID_W
    table = _na_bias_table(rpb)
    vmem = 4 * seq * V7X_LANES * 2 + 2 * table[0].size * 4 + 16 * 1024 * 1024
    out = pl.pallas_call(
        functools.partial(_na_kernel, rows=rows),
        grid=(batch, nq, rows // NA_ROWS_PER_STEP),
        in_specs=[pl.BlockSpec((None, tq, V7X_LANES), lambda b, h, r: (b, r, h)),
                  pl.BlockSpec((None, seq, V7X_LANES), lambda b, h, r: (b, 0, nq + h)),
                  pl.BlockSpec((None, seq, V7X_LANES), lambda b, h, r: (b, 0, 2 * nq + h)),
                  pl.BlockSpec((None, ctx_len, V7X_LANES), lambda b, h, r: (b, 0, nq + h)),
                  pl.BlockSpec((None, ctx_len, V7X_LANES), lambda b, h, r: (b, 0, 2 * nq + h)),
                  pl.BlockSpec((None,) + table.shape[1:], lambda b, h, r: (h, 0, 0, 0))],
        out_specs=pl.BlockSpec((None, tq, V7X_LANES), lambda b, h, r: (b, r, h)),
        out_shape=jax.ShapeDtypeStruct((batch, seq, NA_WIDTH), BF16),
        name="na_attn",
        compiler_params=_params(("parallel", "parallel", "arbitrary"), vmem),
    )(qkv3, qkv3, qkv3, qkvc3, qkvc3, table)
    return out.reshape(batch * seq, NA_WIDTH)


def _ctx_attn_kernel(q_ref, k_ref, v_ref, o_ref):
    n = q_ref.shape[0]
    low_lanes = lax.broadcasted_iota(jnp.int32, (n, V7X_LANES), 1) < NA_HEAD_DIM
    q2 = _split_heads_on_rows(q_ref[...] * (NA_HEAD_DIM ** -0.5), low_lanes)
    s = lax.dot_general(q2, k_ref[...], (((1,), (1,)), ((), ())), preferred_element_type=F32)
    p = jnp.exp(s - jnp.max(s, axis=-1, keepdims=True))
    o2 = _mdot(p.astype(BF16), v_ref[...]) / jnp.sum(p, axis=-1, keepdims=True)
    o_ref[...] = jnp.where(low_lanes, o2[:n], o2[n:]).astype(o_ref.dtype)


def _context_attention(qkv_c, batch, ctx_len):
    nq = NA_WIDTH // V7X_LANES
    qkvc3 = qkv_c.reshape(batch, ctx_len, 3 * NA_WIDTH)
    out = pl.pallas_call(
        _ctx_attn_kernel,
        grid=(batch, nq),
        in_specs=[pl.BlockSpec((None, ctx_len, V7X_LANES), lambda b, h: (b, 0, h)),
                  pl.BlockSpec((None, ctx_len, V7X_LANES), lambda b, h: (b, 0, nq + h)),
                  pl.BlockSpec((None, ctx_len, V7X_LANES), lambda b, h: (b, 0, 2 * nq + h))],
        out_specs=pl.BlockSpec((None, ctx_len, V7X_LANES), lambda b, h: (b, 0, h)),
        out_shape=jax.ShapeDtypeStruct((batch, ctx_len, NA_WIDTH), BF16),
        name="ctx_attn",
        compiler_params=_params(("parallel", "parallel"), 0),
    )(qkvc3, qkvc3, qkvc3)
    return out.reshape(batch * ctx_len, NA_WIDTH)


def _conv3(p, prev_row, next_row, w0, w1, w2, block, seq_len):
    tm = p.shape[0]
    row = lax.broadcasted_iota(jnp.int32, p.shape, 0)
    pos = (block * tm + row) & (seq_len - 1)
    down = jnp.where(row == 0, prev_row, pltpu.roll(p, 1, 0))
    down = jnp.where(pos == 0, 0.0, down)
    up = jnp.where(row == tm - 1, next_row, pltpu.roll(p, tm - 1, 0))
    up = jnp.where(pos == seq_len - 1, 0.0, up)
    return down * w0 + p * w1 + up * w2


def _halo_specs(tm, m, width):
    blocks = m // V7X_SUBLANES
    per = tm // V7X_SUBLANES
    prev = pl.BlockSpec((V7X_SUBLANES, width), lambda i: (jnp.maximum(i * per - 1, 0), 0))
    nxt = pl.BlockSpec((V7X_SUBLANES, width), lambda i: (jnp.minimum((i + 1) * per, blocks - 1), 0))
    return prev, nxt


def _outproj_ab_kernel(a_ref, gcx_ref, prev_ref, next_ref, cw_ref, w_ref, x_ref, m_ref, g_ref, o_ref,
                       *, seq_len, mrow, grow):
    na = a_ref.shape[1]
    sw = cw_ref.shape[1]
    last = V7X_SUBLANES - 1
    p = gcx_ref[:, sw:2 * sw] * gcx_ref[:, 2 * sw:3 * sw]
    p_prev = prev_ref[last:last + 1, sw:2 * sw] * prev_ref[last:last + 1, 2 * sw:3 * sw]
    p_next = next_ref[0:1, sw:2 * sw] * next_ref[0:1, 2 * sw:3 * sw]
    conv = _conv3(p, p_prev, p_next, cw_ref[0:1, :], cw_ref[1:2, :], cw_ref[2:3, :], pl.program_id(0), seq_len)
    b = (gcx_ref[:, 0:sw] * conv).astype(BF16)
    y = _mdot(a_ref[...], w_ref[0:na, :]) + _mdot(b, w_ref[na:, :])
    o_ref[...] = x_ref[...] + m_ref[mrow:mrow + 1, :] * (_rms(y) * g_ref[grow:grow + 1, :])


def _outproj_plain_kernel(z_ref, w_ref, x_ref, m_ref, g_ref, o_ref, *, mrow, grow):
    y = _mdot(z_ref[...].astype(BF16), w_ref[...])
    o_ref[...] = x_ref[...] + m_ref[mrow:mrow + 1, :] * (_rms(y) * g_ref[grow:grow + 1, :])


def _outproj_ab(a, gcx, conv_w, w, x, mod_i, g, *, seq_len, mrow, grow, batch_of):
    m, d = x.shape
    tm = min(ROW_TILE, m)
    assert seq_len & (seq_len - 1) == 0
    prev, nxt = _halo_specs(tm, m, gcx.shape[1])
    kern = functools.partial(_outproj_ab_kernel, seq_len=seq_len, mrow=mrow, grow=grow)
    vmem = 2 * tm * (a.shape[1] * 2 + gcx.shape[1] * 4 + 2 * d * 4) + w.size * 2 + 8 * tm * d * 4
    return pl.pallas_call(
        kern,
        grid=(m // tm,),
        in_specs=[pl.BlockSpec((tm, a.shape[1]), lambda i: (i, 0)),
                  pl.BlockSpec((tm, gcx.shape[1]), lambda i: (i, 0)),
                  prev, nxt, _resident(conv_w.shape), _resident(w.shape),
                  pl.BlockSpec((tm, d), lambda i: (i, 0)),
                  pl.BlockSpec((None, N_MOD, d), lambda i: (batch_of(i, tm), 0, 0)),
                  _resident(g.shape)],
        out_specs=pl.BlockSpec((tm, d), lambda i: (i, 0)),
        out_shape=jax.ShapeDtypeStruct((m, d), F32),
        name="outproj_ab",
        compiler_params=_params(("parallel",), vmem),
    )(a, gcx, gcx, gcx, conv_w, w, x, mod_i, g)


def _outproj_plain(z, w, x, mod_i, g, *, mrow, grow, batch_of):
    m, d = x.shape
    tm = min(ROW_TILE, m)
    kern = functools.partial(_outproj_plain_kernel, mrow=mrow, grow=grow)
    vmem = 2 * tm * (z.shape[1] * 4 + 2 * d * 4) + w.size * 2 + 8 * tm * d * 4
    return pl.pallas_call(
        kern,
        grid=(m // tm,),
        in_specs=[pl.BlockSpec((tm, z.shape[1]), lambda i: (i, 0)),
                  _resident(w.shape),
                  pl.BlockSpec((tm, d), lambda i: (i, 0)),
                  pl.BlockSpec((None, N_MOD, d), lambda i: (batch_of(i, tm), 0, 0)),
                  _resident(g.shape)],
        out_specs=pl.BlockSpec((tm, d), lambda i: (i, 0)),
        out_shape=jax.ShapeDtypeStruct((m, d), F32),
        name="outproj_plain",
        compiler_params=_params(("parallel",), vmem),
    )(z, w, x, mod_i, g)


def _hyena_pre_kernel(pj_ref, prev_ref, next_ref, w_ref, x0_ref, u_ref, *, seq_len):
    c = x0_ref.shape[1]
    last = V7X_SUBLANES - 1
    blk = pl.program_id(0)

    def part(k):
        cols = slice(k * c, (k + 1) * c)
        return _conv3(pj_ref[:, cols], prev_ref[last:last + 1, cols], next_ref[0:1, cols],
                      w_ref[0:1, cols], w_ref[1:2, cols], w_ref[2:3, cols], blk, seq_len)

    x0_ref[...] = part(0)
    u_ref[...] = part(2) * part(1)


def _hyena_pre(proj, short_w, seq_len):
    m, w3 = proj.shape
    c = w3 // 3
    tm = min(ROW_TILE // 2, m)
    assert seq_len & (seq_len - 1) == 0
    prev, nxt = _halo_specs(tm, m, w3)
    return pl.pallas_call(
        functools.partial(_hyena_pre_kernel, seq_len=seq_len),
        grid=(m // tm,),
        in_specs=[pl.BlockSpec((tm, w3), lambda i: (i, 0)), prev, nxt, _resident(short_w.shape)],
        out_specs=[pl.BlockSpec((tm, c), lambda i: (i, 0))] * 2,
        out_shape=[jax.ShapeDtypeStruct((m, c), F32)] * 2,
        name="hyena_pre",
        compiler_params=_params(("parallel",), 2 * tm * w3 * 4 + 4 * tm * c * 4 + 8 * tm * c * 4),
    )(proj, proj, proj, short_w)


def _filter_kernel(z_ref, w1_ref, b1_ref, w2_ref, b2_ref, w3_ref, b3_ref, w4_ref, fr_ref, dec_ref, o_ref,
                   *, length):
    hi = lax.Precision.HIGHEST
    z = z_ref[...]
    fr = fr_ref[...]
    h = jnp.sin(fr * (jnp.dot(z, w1_ref[...], precision=hi, preferred_element_type=F32) + b1_ref[...]))
    h = jnp.sin(fr * (jnp.dot(h, w2_ref[...], precision=hi, preferred_element_type=F32) + b2_ref[...]))
    h = jnp.sin(fr * (jnp.dot(h, w3_ref[...], precision=hi, preferred_element_type=F32) + b3_ref[...]))
    filt = jnp.dot(h, w4_ref[...], precision=hi, preferred_element_type=F32)
    out = filt * jnp.exp(-z[:, 0:1] * dec_ref[...])
    row = pl.program_id(0) * z.shape[0] + lax.broadcasted_iota(jnp.int32, out.shape, 0)
    o_ref[...] = jnp.where(row == length, 0.0, out)


@functools.lru_cache(maxsize=None)
def _filter_positions(length):
    t = np.linspace(0.0, 1.0, length)[:, None]
    w = 2.0 * math.pi * np.arange(length)[:, None] / length
    bands = np.linspace(1e-4, HY_BANDS - 1, HY_BANDS)[None, :]
    z = np.concatenate([t, np.cos(bands * w), -np.sin(bands * w)], axis=-1)
    z2 = np.concatenate([z, z[0:1], z[:0:-1]], axis=0)
    out = np.zeros((2 * length, V7X_LANES), np.float32)
    out[:, :HY_EMB] = z2
    return out


def _hyena_filter(length, w1, b1, w2, b2, w3, b3, w4, freq):
    c = w4.shape[1] // 2
    ffn = w2.shape[0]
    tr = min(FILTER_ROWS, length)
    per_half = length // tr
    zpos = jnp.asarray(_filter_positions(length))
    w1p = jnp.pad(w1, ((0, V7X_LANES - w1.shape[0]), (0, 0)))
    decay = np.abs(np.linspace(math.log(HY_TARGET) / HY_SLOW_PCT, math.log(HY_TARGET) / HY_FAST_PCT, c))
    decay = jnp.asarray(decay[None, :], F32)
    row = lambda v: v.reshape(1, -1)
    small = lambda shape: pl.BlockSpec(shape, lambda i: (0, 0))
    return pl.pallas_call(
        functools.partial(_filter_kernel, length=length),
        grid=(2 * per_half,),
        in_specs=[pl.BlockSpec((tr, V7X_LANES), lambda i: (i, 0)),
                  small((V7X_LANES, ffn)), small((1, ffn)), small((ffn, ffn)), small((1, ffn)),
                  small((ffn, ffn)), small((1, ffn)),
                  pl.BlockSpec((ffn, c), lambda i: (0, i // per_half)),
                  small((1, ffn)), small((1, c))],
        out_specs=pl.BlockSpec((tr, c), lambda i: (i, 0)),
        out_shape=jax.ShapeDtypeStruct((2 * length, c), F32),
        name="hyena_filter",
        compiler_params=_params(("parallel",), 0),
    )(zpos, w1p, row(b1), w2, row(b2), w3, row(b3), w4, row(freq), decay)


def _complex_as_real(m):
    return np.block([[m.real, -m.imag], [m.imag, m.real]])


@functools.lru_cache(maxsize=None)
def _dft_tables(length):
    n = 2 * length
    n1 = n // DFT_RADIX
    half = n1 // 2
    eye = np.eye(V7X_SUBLANES)
    k1 = np.arange(n1)
    f1 = np.exp(-2j * np.pi * np.outer(k1, k1) / n1)
    first_data = np.kron(_complex_as_real(f1[:, :half]), eye)
    first_filt = np.kron(np.concatenate([f1.real, f1.imag], axis=0), eye)
    inv1 = np.exp(2j * np.pi * np.outer(np.arange(half), k1) / n1) / n
    last = np.kron(_complex_as_real(inv1), eye)
    i2 = np.arange(DFT_RADIX)
    phase = (np.outer(i2, i2)[None] / DFT_RADIX + (i2[None, None, :] * k1[:, None, None]) / n)
    second = np.exp(-2j * np.pi * phase)
    fwd = np.stack([_complex_as_real(second[k]) for k in range(n1)])
    inv = np.stack([_complex_as_real(np.conj(second[k]).T) for k in range(n1)])
    return tuple(np.asarray(a, np.float32) for a in (first_data, first_filt, last, fwd, inv))


def _kron_kernel(mat_ref, x_ref, o_ref):
    lanes = x_ref.shape[-1]
    x = x_ref[...].reshape(-1, lanes).astype(BF16)
    o_ref[...] = _mdot(mat_ref[...], x).reshape(o_ref.shape)


def _kron_out_kernel(mat_ref, b_ref, x0_ref, u_ref, bias_ref, o_ref):
    lanes = b_ref.shape[-1]
    y = _mdot(mat_ref[...], b_ref[...].reshape(-1, lanes).astype(BF16)).reshape(o_ref.shape)
    o_ref[...] = x0_ref[...] * (y + u_ref[...] * bias_ref[...])


def _second_stage_kernel(a_ref, g_ref, gi_ref, kf_ref, o_ref):
    lanes = a_ref.shape[-1]
    r = DFT_RADIX
    for j in range(a_ref.shape[1]):
        spec = _mdot(g_ref[j], a_ref[:, j].reshape(2 * r, lanes).astype(BF16))
        xr, xi = spec[:r], spec[r:]
        kr, ki = kf_ref[0, j], kf_ref[1, j]
        prod = jnp.concatenate([xr * kr - xi * ki, xr * ki + xi * kr], axis=0).astype(BF16)
        o_ref[:, j] = _mdot(gi_ref[j], prod).reshape(2, r, lanes)


def _second_stage_filter_kernel(a_ref, g_ref, o_ref):
    lanes = a_ref.shape[-1]
    for j in range(a_ref.shape[1]):
        o_ref[:, j] = _mdot(g_ref[j], a_ref[:, j].reshape(2 * DFT_RADIX, lanes).astype(BF16)).reshape(
            2, DFT_RADIX, lanes)


def _long_conv_gate(u, x0, kfilt, bias, batch, length):
    c = u.shape[1]
    pairs = batch // 2
    n1 = 2 * length // DFT_RADIX
    half = n1 // 2
    r, s8, cb, kb = DFT_RADIX, V7X_SUBLANES, min(DFT_LANES, c), DFT_K1_PER_STEP
    assert batch % 2 == 0 and n1 % kb == 0 and c % cb == 0
    first_data, first_filt, last, fwd, inv = (jnp.asarray(t, BF16) for t in _dft_tables(length))
    groups = r // s8
    big = 48 * 1024 * 1024

    u5 = u.reshape(pairs, 2, half, r, c)
    x05 = x0.reshape(pairs, 2, half, r, c)
    a_data = pl.pallas_call(
        _kron_kernel,
        grid=(pairs, groups, c // cb),
        in_specs=[_resident(first_data.shape),
                  pl.BlockSpec((None, 2, half, s8, cb), lambda p, g, l: (p, 0, 0, g, l))],
        out_specs=pl.BlockSpec((None, 2, n1, s8, cb), lambda p, g, l: (p, 0, 0, g, l)),
        out_shape=jax.ShapeDtypeStruct((pairs, 2, n1, r, c), F32),
        name="dft_first",
        compiler_params=_params(("parallel", "parallel", "parallel"), big),
    )(first_data, u5)

    a_filt = pl.pallas_call(
        _kron_kernel,
        grid=(groups, c // cb),
        in_specs=[_resident(first_filt.shape),
                  pl.BlockSpec((n1, s8, cb), lambda g, l: (0, g, l))],
        out_specs=pl.BlockSpec((2, n1, s8, cb), lambda g, l: (0, 0, g, l)),
        out_shape=jax.ShapeDtypeStruct((2, n1, r, c), F32),
        name="dft_first_filter",
        compiler_params=_params(("parallel", "parallel"), big),
    )(first_filt, kfilt.reshape(n1, r, c))

    k_spec = pl.pallas_call(
        _second_stage_filter_kernel,
        grid=(c // cb, n1 // kb),
        in_specs=[pl.BlockSpec((2, kb, r, cb), lambda l, k: (0, k, 0, l)),
                  pl.BlockSpec((kb, 2 * r, 2 * r), lambda l, k: (k, 0, 0))],
        out_specs=pl.BlockSpec((2, kb, r, cb), lambda l, k: (0, k, 0, l)),
        out_shape=jax.ShapeDtypeStruct((2, n1, r, c), F32),
        name="dft_second_filter",
        compiler_params=_params(("parallel", "parallel"), big),
    )(a_filt, fwd)

    b_data = pl.pallas_call(
        _second_stage_kernel,
        grid=(c // cb, n1 // kb, pairs),
        in_specs=[pl.BlockSpec((None, 2, kb, r, cb), lambda l, k, p: (p, 0, k, 0, l)),
                  pl.BlockSpec((kb, 2 * r, 2 * r), lambda l, k, p: (k, 0, 0)),
                  pl.BlockSpec((kb, 2 * r, 2 * r), lambda l, k, p: (k, 0, 0)),
                  pl.BlockSpec((2, kb, r, cb), lambda l, k, p: (0, k, 0, l))],
        out_specs=pl.BlockSpec((None, 2, kb, r, cb), lambda l, k, p: (p, 0, k, 0, l)),
        out_shape=jax.ShapeDtypeStruct((pairs, 2, n1, r, c), F32),
        name="dft_second",
        compiler_params=_params(("parallel", "parallel", "arbitrary"), big),
    )(a_data, fwd, inv, k_spec)

    z = pl.pallas_call(
        _kron_out_kernel,
        grid=(pairs, groups, c // cb),
        in_specs=[_resident(last.shape),
                  pl.BlockSpec((None, 2, n1, s8, cb), lambda p, g, l: (p, 0, 0, g, l)),
                  pl.BlockSpec((None, 2, half, s8, cb), lambda p, g, l: (p, 0, 0, g, l)),
                  pl.BlockSpec((None, 2, half, s8, cb), lambda p, g, l: (p, 0, 0, g, l)),
                  pl.BlockSpec((1, cb), lambda p, g, l: (0, l))],
        out_specs=pl.BlockSpec((None, 2, half, s8, cb), lambda p, g, l: (p, 0, 0, g, l)),
        out_shape=jax.ShapeDtypeStruct((pairs, 2, half, r, c), F32),
        name="dft_last",
        compiler_params=_params(("parallel", "parallel", "parallel"), big),
    )(last, b_data, x05, u5, bias.reshape(1, c))
    return z.reshape(batch * length, c)


@functools.lru_cache(maxsize=None)
def _dense_dft_tables(length):
    n = 2 * length
    idx = np.arange(n)
    f = np.exp(-2j * np.pi * np.outer(idx, idx) / n)
    fwd_data = _complex_as_real(f[:, :length])
    fwd_filt = np.concatenate([f.real, f.imag], axis=0)
    inv = _complex_as_real(np.conj(f[:length, :]) / n)
    return tuple(np.asarray(a, np.float32) for a in (fwd_data, fwd_filt, inv))


def _short_conv_gate_kernel(fd_ref, ff_ref, inv_ref, u_ref, x0_ref, k_ref, bias_ref, o_ref):
    n = k_ref.shape[0]
    u = u_ref[...]
    spec = _mdot(fd_ref[...], u.astype(BF16))
    kf = _mdot(ff_ref[...], k_ref[...].astype(BF16))
    xr, xi, kr, ki = spec[:n], spec[n:], kf[:n], kf[n:]
    prod = jnp.concatenate([xr * kr - xi * ki, xr * ki + xi * kr], axis=0).astype(BF16)
    o_ref[...] = x0_ref[...] * (_mdot(inv_ref[...], prod) + u * bias_ref[...])


def _long_conv_gate_short(u, x0, kfilt, bias, batch, length):
    c = u.shape[1]
    pairs = batch // 2
    cb = min(DFT_LANES, c)
    fwd_data, fwd_filt, inv = (jnp.asarray(t, BF16) for t in _dense_dft_tables(length))
    pair_spec = pl.BlockSpec((None, 2 * length, cb), lambda p, l: (p, 0, l))
    z = pl.pallas_call(
        _short_conv_gate_kernel,
        grid=(pairs, c // cb),
        in_specs=[_resident(fwd_data.shape), _resident(fwd_filt.shape), _resident(inv.shape),
                  pair_spec, pair_spec,
                  pl.BlockSpec((2 * length, cb), lambda p, l: (0, l)),
                  pl.BlockSpec((1, cb), lambda p, l: (0, l))],
        out_specs=pair_spec,
        out_shape=jax.ShapeDtypeStruct((pairs, 2 * length, c), F32),
        name="short_conv_gate",
        compiler_params=_params(("parallel", "parallel"), 0),
    )(fwd_data, fwd_filt, inv, u.reshape(pairs, 2 * length, c), x0.reshape(pairs, 2 * length, c),
      kfilt, bias.reshape(1, c))
    return z.reshape(batch * length, c)


def kernel(x, c, ctx, c_ctx, w_mod, b_mod, norm_g, ffn_w_gate, ffn_w_up, ffn_w_down, ab_w_in, na_rpb,
           sc_conv_w, ab_w_out, hy_w_in, hy_short_w, hy_f_w1, hy_f_b1, hy_f_w2, hy_f_b2, hy_f_w3, hy_f_b3,
           hy_f_w4, hy_sin_freq, hy_bias, hy_w_out):
    batch, seq, d = x.shape
    ctx_len = ctx.shape[1]
    depth = w_mod.shape[0]
    assert batch + 1 <= V7X_SUBLANES and seq % ROW_TILE == 0
    last_attn = (depth - 1) - (depth - 1) % 2
    sc_width = sc_conv_w.shape[-1]

    cc = jnp.zeros((V7X_SUBLANES, d), F32).at[:batch].set(c).at[batch].set(c_ctx)
    mod = _modulation(cc, w_mod, b_mod).reshape(depth, V7X_SUBLANES, N_MOD, d)
    lat_batch = lambda i, tm: i // (seq // tm)
    ctx_batch = lambda i, tm: batch

    xs = x.reshape(batch * seq, d)
    cs = ctx.reshape(batch * ctx_len, d)
    qkv_cols = (0, 3 * NA_WIDTH)
    gcx_cols = (3 * NA_WIDTH, 3 * NA_WIDTH + 3 * sc_width)

    for i in range(depth):
        j = i // 2
        g = norm_g[i]
        m_i = mod[i]
        keep_ctx = i <= last_attn
        upd_ctx = i < last_attn
        ffn_w = lambda k: (ffn_w_gate[i, k].astype(BF16), ffn_w_up[i, k].astype(BF16),
                           ffn_w_down[i, k].astype(BF16))

        w1 = ffn_w(0)
        xs = _ffn(xs, m_i, g, *w1, mrow=0, grow=0, batch_of=lat_batch)
        if keep_ctx:
            cs = _ffn(cs, m_i, g, *w1, mrow=0, grow=0, batch_of=ctx_batch)

        if i % 2 == 0:
            w_in = ab_w_in[j].astype(BF16)
            w_out = ab_w_out[j].astype(BF16)
            qkv, gcx = _inproj(xs, m_i, g, w_in, mrow=3, grow=2, batch_of=lat_batch,
                               splits=(qkv_cols, gcx_cols), dtypes=(BF16, F32))
            if upd_ctx:
                qkv_c, gcx_c = _inproj(cs, m_i, g, w_in, mrow=3, grow=2, batch_of=ctx_batch,
                                       splits=(qkv_cols, gcx_cols), dtypes=(BF16, F32))
            else:
                (qkv_c,) = _inproj(cs, m_i, g, w_in, mrow=3, grow=2, batch_of=ctx_batch,
                                   splits=(qkv_cols,), dtypes=(BF16,))
            a_lat = _neighbourhood_attention(qkv, qkv_c, na_rpb[j], batch, seq, ctx_len)
            xs = _outproj_ab(a_lat, gcx, sc_conv_w[j], w_out, xs, m_i, g, seq_len=seq, mrow=5, grow=3,
                             batch_of=lat_batch)
            if upd_ctx:
                a_ctx = _context_attention(qkv_c, batch, ctx_len)
                cs = _outproj_ab(a_ctx, gcx_c, sc_conv_w[j], w_out, cs, m_i, g, seq_len=ctx_len, mrow=5,
                                 grow=3, batch_of=ctx_batch)
        else:
            w_in = hy_w_in[j].astype(BF16)
            w_out = hy_w_out[j].astype(BF16)
            width = w_in.shape[1]
            filt = lambda n: _hyena_filter(n, hy_f_w1[j], hy_f_b1[j], hy_f_w2[j], hy_f_b2[j], hy_f_w3[j],
                                           hy_f_b3[j], hy_f_w4[j], hy_sin_freq[j])
            (proj,) = _inproj(xs, m_i, g, w_in, mrow=3, grow=2, batch_of=lat_batch,
                              splits=((0, width),), dtypes=(F32,))
            x0, u = _hyena_pre(proj, hy_short_w[j], seq)
            z = _long_conv_gate(u, x0, filt(seq), hy_bias[j], batch, seq)
            xs = _outproj_plain(z, w_out, xs, m_i, g, mrow=5, grow=3, batch_of=lat_batch)
            if upd_ctx:
                (proj_c,) = _inproj(cs, m_i, g, w_in, mrow=3, grow=2, batch_of=ctx_batch,
                                    splits=((0, width),), dtypes=(F32,))
                x0_c, u_c = _hyena_pre(proj_c, hy_short_w[j], ctx_len)
                z_c = _long_conv_gate_short(u_c, x0_c, filt(ctx_len), hy_bias[j], batch, ctx_len)
                cs = _outproj_plain(z_c, w_out, cs, m_i, g, mrow=5, grow=3, batch_of=ctx_batch)

        w2 = ffn_w(1)
        xs = _ffn(xs, m_i, g, *w2, mrow=6, grow=4, batch_of=lat_batch)
        if upd_ctx:
            cs = _ffn(cs, m_i, g, *w2, mrow=6, grow=4, batch_of=ctx_batch)

    return xs.reshape(batch, seq, d)
```

```python
import functools
import math

import numpy as np
import jax
import jax.numpy as jnp
from jax import lax
from jax.experimental import pallas as pl
from jax.experimental.pallas import tpu as pltpu

F32 = jnp.float32
BF16 = jnp.bfloat16

GRID_W = 64
NA_HEADS = 8
NA_HEAD_DIM = 64
NA_WIDTH = NA_HEADS * NA_HEAD_DIM
NA_KH = 8
NA_KW = 16
FFN_RES = 0.5
N_MOD = 9
RMS_EPS = 1e-6
HY_EMB = 33
HY_BANDS = (HY_EMB - 1) // 2
HY_TARGET = 1e-2
HY_FAST_PCT = 0.3
HY_SLOW_PCT = 1.5

V7X_LANES = 128
V7X_SUBLANES = 8
V7X_MXU_DIM = 256
V7X_VMEM_BYTES = 64 * 1024 * 1024

ROW_TILE = 512
FFN_CHUNK = 256
PROJ_CHUNK = 512
NA_ROWS_PER_STEP = 8
DFT_RADIX = 128
DFT_LANES = 256
DFT_K1_PER_STEP = 8
FILTER_ROWS = 512
MASK_VALUE = -1e30


def _vmem_limit(nbytes):
    return int(min(max(nbytes, 32 * 1024 * 1024), V7X_VMEM_BYTES - 8 * 1024 * 1024))


def _params(semantics, vmem_bytes):
    return pltpu.CompilerParams(dimension_semantics=semantics, vmem_limit_bytes=_vmem_limit(vmem_bytes))


def _resident(shape):
    zeros = (0,) * len(shape)
    return pl.BlockSpec(shape, lambda *_: zeros, pipeline_mode=pl.Buffered(1))


def _rms(x):
    return x * lax.rsqrt(jnp.mean(x * x, axis=-1, keepdims=True) + RMS_EPS)


def _mdot(a, b):
    return jnp.dot(a, b, preferred_element_type=F32)


def _mod_kernel(c_ref, w_ref, b_ref, o_ref):
    c = c_ref[...]
    s = (c * jax.nn.sigmoid(c)).astype(BF16)
    o_ref[...] = _mdot(s, w_ref[...].astype(BF16)) + b_ref[...]


def _modulation(cc, w_mod, b_mod):
    depth, d, nd = w_mod.shape
    return pl.pallas_call(
        _mod_kernel,
        grid=(depth, nd // d),
        in_specs=[pl.BlockSpec((V7X_SUBLANES, d), lambda i, j: (0, 0)),
                  pl.BlockSpec((None, d, d), lambda i, j: (i, 0, j)),
                  pl.BlockSpec((None, 1, d), lambda i, j: (i, 0, j))],
        out_specs=pl.BlockSpec((None, V7X_SUBLANES, d), lambda i, j: (i, 0, j)),
        out_shape=jax.ShapeDtypeStruct((depth, V7X_SUBLANES, nd), F32),
        name="modulation",
        compiler_params=_params(("parallel", "parallel"), 4 * d * d * 4),
    )(cc, w_mod, b_mod.reshape(depth, 1, nd))


def _ffn_kernel(x_ref, m_ref, g_ref, wg_ref, wu_ref, wd_ref, o_ref, acc_ref, *, mrow, grow):
    x = x_ref[...]
    shift, scale, gate = m_ref[mrow:mrow + 1, :], m_ref[mrow + 1:mrow + 2, :], m_ref[mrow + 2:mrow + 3, :]
    h = ((_rms(x) * g_ref[grow:grow + 1, :]) * (1.0 + scale) + shift).astype(BF16)
    for j in range(wg_ref.shape[1] // FFN_CHUNK):
        c0, c1 = j * FFN_CHUNK, (j + 1) * FFN_CHUNK
        g = _mdot(h, wg_ref[:, c0:c1])
        u = _mdot(h, wu_ref[:, c0:c1])
        a = (g * jax.nn.sigmoid(g) * u).astype(BF16)
        part = _mdot(a, wd_ref[c0:c1, :])
        if j == 0:
            acc_ref[...] = part
        else:
            acc_ref[...] += part
    y = acc_ref[...]
    o_ref[...] = x + FFN_RES * gate * (_rms(y) * g_ref[grow + 1:grow + 2, :])


def _ffn(x, mod_i, g, wg, wu, wd, *, mrow, grow, batch_of):
    m, d = x.shape
    f = wg.shape[1]
    tm = min(ROW_TILE, m)
    kern = functools.partial(_ffn_kernel, mrow=mrow, grow=grow)
    vmem = 3 * d * f * 2 + 4 * tm * d * 4 + tm * d * 4 + 6 * tm * FFN_CHUNK * 4 + tm * d * 8
    return pl.pallas_call(
        kern,
        grid=(m // tm,),
        in_specs=[pl.BlockSpec((tm, d), lambda i: (i, 0)),
                  pl.BlockSpec((None, N_MOD, d), lambda i: (batch_of(i, tm), 0, 0)),
                  _resident(g.shape), _resident(wg.shape), _resident(wu.shape), _resident(wd.shape)],
        out_specs=pl.BlockSpec((tm, d), lambda i: (i, 0)),
        out_shape=jax.ShapeDtypeStruct((m, d), F32),
        scratch_shapes=[pltpu.VMEM((tm, d), F32)],
        name="ffn",
        compiler_params=_params(("parallel",), vmem),
    )(x, mod_i, g, wg, wu, wd)


def _conv3_rows(pe, w0, w1, w2, tm):
    n = pe.shape[0]
    lo = V7X_SUBLANES
    down = pltpu.roll(pe, 1, 0)[lo:lo + tm]
    up = pltpu.roll(pe, n - 1, 0)[lo:lo + tm]
    return down * w0 + pe[lo:lo + tm] * w1 + up * w2


def _normed_with_halo(x_ref, prev_ref, next_ref, m_ref, g_ref, mrow, grow, seq_len):
    tm = x_ref.shape[0]
    lo = V7X_SUBLANES
    shift, scale = m_ref[mrow:mrow + 1, :], m_ref[mrow + 1:mrow + 2, :]
    xe = jnp.concatenate([prev_ref[...], x_ref[...], next_ref[...]], axis=0)
    he = (_rms(xe) * g_ref[grow:grow + 1, :]) * (1.0 + scale) + shift
    start = (pl.program_id(0) * tm) & (seq_len - 1)
    keep_prev = (start != 0).astype(F32)
    keep_next = (start + tm != seq_len).astype(F32)
    row = lax.broadcasted_iota(jnp.int32, (tm + 2 * lo, 1), 0)
    keep = jnp.where(row < lo, keep_prev, jnp.where(row >= lo + tm, keep_next, 1.0))
    return he[lo:lo + tm].astype(BF16), (he * keep).astype(BF16)


def _inproj_ab_kernel(x_ref, prev_ref, next_ref, m_ref, g_ref, w_ref, cw_ref, qkv_ref, *b_refs,
                      mrow, grow, seq_len):
    tm = x_ref.shape[0]
    h, he = _normed_with_halo(x_ref, prev_ref, next_ref, m_ref, g_ref, mrow, grow, seq_len)
    nq = qkv_ref.shape[1]
    for c in range(0, nq, PROJ_CHUNK):
        qkv_ref[:, c:c + PROJ_CHUNK] = _mdot(h, w_ref[:, c:c + PROJ_CHUNK]).astype(qkv_ref.dtype)
    if b_refs:
        (b_ref,) = b_refs
        sw = cw_ref.shape[1]
        gate = _mdot(h, w_ref[:, nq:nq + sw])
        p = _mdot(he, w_ref[:, nq + sw:nq + 2 * sw]) * _mdot(he, w_ref[:, nq + 2 * sw:nq + 3 * sw])
        conv = _conv3_rows(p, cw_ref[0:1, :], cw_ref[1:2, :], cw_ref[2:3, :], tm)
        b_ref[...] = (gate * conv).astype(b_ref.dtype)


def _inproj_hyena_kernel(x_ref, prev_ref, next_ref, m_ref, g_ref, w_ref, cw_ref, x0_ref, u_ref,
                         *, mrow, grow, seq_len):
    tm = x_ref.shape[0]
    _, he = _normed_with_halo(x_ref, prev_ref, next_ref, m_ref, g_ref, mrow, grow, seq_len)
    c = x0_ref.shape[1]

    def conv_cols(c0):
        cols = slice(c0, c0 + PROJ_CHUNK)
        return _conv3_rows(_mdot(he, w_ref[:, cols]), cw_ref[0:1, cols], cw_ref[1:2, cols], cw_ref[2:3, cols], tm)

    for c0 in range(0, c, PROJ_CHUNK):
        x0_ref[:, c0:c0 + PROJ_CHUNK] = conv_cols(c0)
        u_ref[:, c0:c0 + PROJ_CHUNK] = conv_cols(2 * c + c0) * conv_cols(c + c0)


def _halo_specs(tm, m, width):
    blocks = m // V7X_SUBLANES
    per = tm // V7X_SUBLANES
    prev = pl.BlockSpec((V7X_SUBLANES, width), lambda i: (jnp.maximum(i * per - 1, 0), 0))
    nxt = pl.BlockSpec((V7X_SUBLANES, width), lambda i: (jnp.minimum((i + 1) * per, blocks - 1), 0))
    return prev, nxt


def _inproj_call(kern, name, x, mod_i, g, w, conv_w, outs, *, mrow, grow, seq_len, batch_of):
    m, d = x.shape
    tm = min(ROW_TILE, seq_len)
    assert seq_len & (seq_len - 1) == 0 and seq_len % tm == 0 and m % seq_len == 0
    prev, nxt = _halo_specs(tm, m, d)
    out_bytes = sum(cols * jnp.dtype(dt).itemsize for cols, dt in outs)
    vmem = w.size * 2 + 2 * tm * d * 4 + 2 * tm * out_bytes + 6 * tm * d * 4
    return pl.pallas_call(
        functools.partial(kern, mrow=mrow, grow=grow, seq_len=seq_len),
        grid=(m // tm,),
        in_specs=[pl.BlockSpec((tm, d), lambda i: (i, 0)), prev, nxt,
                  pl.BlockSpec((None, N_MOD, d), lambda i: (batch_of(i, tm), 0, 0)),
                  _resident(g.shape), _resident(w.shape), _resident(conv_w.shape)],
        out_specs=[pl.BlockSpec((tm, cols), lambda i: (i, 0)) for cols, _ in outs],
        out_shape=[jax.ShapeDtypeStruct((m, cols), dt) for cols, dt in outs],
        name=name,
        compiler_params=_params(("parallel",), vmem),
    )(x, x, x, mod_i, g, w, conv_w)


def _split_heads_on_rows(q, low_lanes):
    zero = jnp.zeros_like(q)
    return jnp.concatenate([jnp.where(low_lanes, q, zero), jnp.where(low_lanes, zero, q)], axis=0)


def _na_kernel(q_ref, k_ref, v_ref, kc_ref, vc_ref, bt_ref, o_ref, *, rows):
    rb = pl.program_id(2)
    low_lanes = lax.broadcasted_iota(jnp.int32, (GRID_W, V7X_LANES), 1) < NA_HEAD_DIM
    kc = kc_ref[...]
    vc = vc_ref[...]
    nt = (((1,), (1,)), ((), ()))

    for i in range(NA_ROWS_PER_STEP):
        r = rb * NA_ROWS_PER_STEP + i
        rs = jnp.clip(r - NA_KH // 2, 0, rows - NA_KH)
        q = q_ref[i * GRID_W:(i + 1) * GRID_W, :] * (NA_HEAD_DIM ** -0.5)
        q2 = _split_heads_on_rows(q, low_lanes)
        k0 = pl.multiple_of(rs * GRID_W, GRID_W)
        kk = k_ref[pl.ds(k0, NA_KH * GRID_W), :]
        vv = v_ref[pl.ds(k0, NA_KH * GRID_W), :]
        s = lax.dot_general(q2, kk, nt, preferred_element_type=F32) + bt_ref[rs - r + (NA_KH - 1)]
        sc = lax.dot_general(q2, kc, nt, preferred_element_type=F32)
        mx = jnp.maximum(jnp.max(s, axis=-1, keepdims=True), jnp.max(sc, axis=-1, keepdims=True))
        p = jnp.exp(s - mx)
        pc = jnp.exp(sc - mx)
        den = jnp.sum(p, axis=-1, keepdims=True) + jnp.sum(pc, axis=-1, keepdims=True)
        o2 = (_mdot(p.astype(BF16), vv) + _mdot(pc.astype(BF16), vc)) / den
        o = jnp.where(low_lanes, o2[:GRID_W], o2[GRID_W:])
        o_ref[i * GRID_W:(i + 1) * GRID_W, :] = o.astype(o_ref.dtype)


def _na_bias_table(rpb):
    h = rpb.shape[0]
    c = jnp.arange(GRID_W)[:, None]
    kc = jnp.arange(GRID_W)[None, :]
    start = jnp.clip(c - NA_KW // 2, 0, GRID_W - NA_KW)
    valid = (kc >= start) & (kc < start + NA_KW)
    idx = jnp.clip(kc - c + (NA_KW - 1), 0, 2 * NA_KW - 2)
    dense = jnp.where(valid[None, None], rpb[:, :, idx], MASK_VALUE)
    off = jnp.arange(NA_KH)[:, None] + jnp.arange(NA_KH)[None, :]
    tab = dense[:, off]
    tab = tab.transpose(0, 1, 3, 2, 4).reshape(h, NA_KH, GRID_W, NA_KH * GRID_W)
    tab = tab.reshape(h // 2, 2, NA_KH, GRID_W, NA_KH * GRID_W).transpose(0, 2, 1, 3, 4)
    return tab.reshape(h // 2, NA_KH, 2 * GRID_W, NA_KH * GRID_W)


def _neighbourhood_attention(qkv, qkv_c, rpb, batch, seq, ctx_len):
    rows = seq // GRID_W
    assert rows >= NA_KH and rows % NA_ROWS_PER_STEP == 0
    nq = NA_WIDTH // V7X_LANES
    qkv3 = qkv.reshape(batch, seq, 3 * NA_WIDTH)
    qkvc3 = qkv_c.reshape(batch, ctx_len, 3 * NA_WIDTH)
    tq = NA_ROWS_PER_STEP * GRID_W
    table = _na_bias_table(rpb)
    vmem = 4 * seq * V7X_LANES * 2 + 2 * table[0].size * 4 + 16 * 1024 * 1024
    out = pl.pallas_call(
        functools.partial(_na_kernel, rows=rows),
        grid=(batch, nq, rows // NA_ROWS_PER_STEP),
        in_specs=[pl.BlockSpec((None, tq, V7X_LANES), lambda b, h, r: (b, r, h)),
                  pl.BlockSpec((None, seq, V7X_LANES), lambda b, h, r: (b, 0, nq + h)),
                  pl.BlockSpec((None, seq, V7X_LANES), lambda b, h, r: (b, 0, 2 * nq + h)),
                  pl.BlockSpec((None, ctx_len, V7X_LANES), lambda b, h, r: (b, 0, nq + h)),
                  pl.BlockSpec((None, ctx_len, V7X_LANES), lambda b, h, r: (b, 0, 2 * nq + h)),
                  pl.BlockSpec((None,) + table.shape[1:], lambda b, h, r: (h, 0, 0, 0))],
        out_specs=pl.BlockSpec((None, tq, V7X_LANES), lambda b, h, r: (b, r, h)),
        out_shape=jax.ShapeDtypeStruct((batch, seq, NA_WIDTH), BF16),
        name="na_attn",
        compiler_params=_params(("parallel", "parallel", "arbitrary"), vmem),
    )(qkv3, qkv3, qkv3, qkvc3, qkvc3, table)
    return out.reshape(batch * seq, NA_WIDTH)


def _ctx_attn_kernel(q_ref, k_ref, v_ref, o_ref):
    n = q_ref.shape[0]
    low_lanes = lax.broadcasted_iota(jnp.int32, (n, V7X_LANES), 1) < NA_HEAD_DIM
    q2 = _split_heads_on_rows(q_ref[...] * (NA_HEAD_DIM ** -0.5), low_lanes)
    s = lax.dot_general(q2, k_ref[...], (((1,), (1,)), ((), ())), preferred_element_type=F32)
    p = jnp.exp(s - jnp.max(s, axis=-1, keepdims=True))
    o2 = _mdot(p.astype(BF16), v_ref[...]) / jnp.sum(p, axis=-1, keepdims=True)
    o_ref[...] = jnp.where(low_lanes, o2[:n], o2[n:]).astype(o_ref.dtype)


def _context_attention(qkv_c, batch, ctx_len):
    nq = NA_WIDTH // V7X_LANES
    qkvc3 = qkv_c.reshape(batch, ctx_len, 3 * NA_WIDTH)
    out = pl.pallas_call(
        _ctx_attn_kernel,
        grid=(batch, nq),
        in_specs=[pl.BlockSpec((None, ctx_len, V7X_LANES), lambda b, h: (b, 0, h)),
                  pl.BlockSpec((None, ctx_len, V7X_LANES), lambda b, h: (b, 0, nq + h)),
                  pl.BlockSpec((None, ctx_len, V7X_LANES), lambda b, h: (b, 0, 2 * nq + h))],
        out_specs=pl.BlockSpec((None, ctx_len, V7X_LANES), lambda b, h: (b, 0, h)),
        out_shape=jax.ShapeDtypeStruct((batch, ctx_len, NA_WIDTH), BF16),
        name="ctx_attn",
        compiler_params=_params(("parallel", "parallel"), 0),
    )(qkvc3, qkvc3, qkvc3)
    return out.reshape(batch * ctx_len, NA_WIDTH)


def _outproj_kernel(*refs, mrow, grow):
    *part_refs, w_ref, x_ref, m_ref, g_ref, o_ref = refs
    y, r0 = None, 0
    for p_ref in part_refs:
        r1 = r0 + p_ref.shape[1]
        term = _mdot(p_ref[...].astype(BF16), w_ref[r0:r1, :])
        y = term if y is None else y + term
        r0 = r1
    o_ref[...] = x_ref[...] + m_ref[mrow:mrow + 1, :] * (_rms(y) * g_ref[grow:grow + 1, :])


def _outproj(parts, w, x, mod_i, g, *, mrow, grow, batch_of):
    m, d = x.shape
    tm = min(ROW_TILE, m)
    part_bytes = sum(p.shape[1] * p.dtype.itemsize for p in parts)
    vmem = 2 * tm * (part_bytes + 2 * d * 4) + w.size * 2 + 8 * tm * d * 4
    return pl.pallas_call(
        functools.partial(_outproj_kernel, mrow=mrow, grow=grow),
        grid=(m // tm,),
        in_specs=[pl.BlockSpec((tm, p.shape[1]), lambda i: (i, 0)) for p in parts] + [
            _resident(w.shape),
            pl.BlockSpec((tm, d), lambda i: (i, 0)),
            pl.BlockSpec((None, N_MOD, d), lambda i: (batch_of(i, tm), 0, 0)),
            _resident(g.shape)],
        out_specs=pl.BlockSpec((tm, d), lambda i: (i, 0)),
        out_shape=jax.ShapeDtypeStruct((m, d), F32),
        name="outproj",
        compiler_params=_params(("parallel",), vmem),
    )(*parts, w, x, mod_i, g)


def _filter_kernel(z_ref, w1_ref, b1_ref, w2_ref, b2_ref, w3_ref, b3_ref, w4_ref, fr_ref, dec_ref, o_ref,
                   *, length):
    hi = lax.Precision.HIGHEST
    z = z_ref[...]
    fr = fr_ref[...]
    h = jnp.sin(fr * (jnp.dot(z, w1_ref[...], precision=hi, preferred_element_type=F32) + b1_ref[...]))
    h = jnp.sin(fr * (jnp.dot(h, w2_ref[...], precision=hi, preferred_element_type=F32) + b2_ref[...]))
    h = jnp.sin(fr * (jnp.dot(h, w3_ref[...], precision=hi, preferred_element_type=F32) + b3_ref[...]))
    filt = _mdot(h.astype(BF16), w4_ref[...].astype(BF16))
    out = filt * jnp.exp(-z[:, 0:1] * dec_ref[...])
    row = pl.program_id(0) * z.shape[0] + lax.broadcasted_iota(jnp.int32, out.shape, 0)
    o_ref[...] = jnp.where(row == length, 0.0, out)


@functools.lru_cache(maxsize=None)
def _filter_positions(length):
    t = np.linspace(0.0, 1.0, length)[:, None]
    w = 2.0 * math.pi * np.arange(length)[:, None] / length
    bands = np.linspace(1e-4, HY_BANDS - 1, HY_BANDS)[None, :]
    z = np.concatenate([t, np.cos(bands * w), -np.sin(bands * w)], axis=-1)
    z2 = np.concatenate([z, z[0:1], z[:0:-1]], axis=0)
    out = np.zeros((2 * length, V7X_LANES), np.float32)
    out[:, :HY_EMB] = z2
    return out


def _hyena_filter(length, w1, b1, w2, b2, w3, b3, w4, freq):
    c = w4.shape[1] // 2
    ffn = w2.shape[0]
    tr = min(FILTER_ROWS, length)
    per_half = length // tr
    zpos = jnp.asarray(_filter_positions(length))
    w1p = jnp.pad(w1, ((0, V7X_LANES - w1.shape[0]), (0, 0)))
    decay = np.abs(np.linspace(math.log(HY_TARGET) / HY_SLOW_PCT, math.log(HY_TARGET) / HY_FAST_PCT, c))
    decay = jnp.asarray(decay[None, :], F32)
    row = lambda v: v.reshape(1, -1)
    small = lambda shape: pl.BlockSpec(shape, lambda i: (0, 0))
    return pl.pallas_call(
        functools.partial(_filter_kernel, length=length),
        grid=(2 * per_half,),
        in_specs=[pl.BlockSpec((tr, V7X_LANES), lambda i: (i, 0)),
                  small((V7X_LANES, ffn)), small((1, ffn)), small((ffn, ffn)), small((1, ffn)),
                  small((ffn, ffn)), small((1, ffn)),
                  pl.BlockSpec((ffn, c), lambda i: (0, i // per_half)),
                  small((1, ffn)), small((1, c))],
        out_specs=pl.BlockSpec((tr, c), lambda i: (i, 0)),
        out_shape=jax.ShapeDtypeStruct((2 * length, c), F32),
        name="hyena_filter",
        compiler_params=_params(("parallel",), 0),
    )(zpos, w1p, row(b1), w2, row(b2), w3, row(b3), w4, row(freq), decay)


def _complex_as_real(m):
    return np.block([[m.real, -m.imag], [m.imag, m.real]])


@functools.lru_cache(maxsize=None)
def _dft_tables(length):
    n = 2 * length
    n1 = n // DFT_RADIX
    half = n1 // 2
    eye = np.eye(V7X_SUBLANES)
    k1 = np.arange(n1)
    f1 = np.exp(-2j * np.pi * np.outer(k1, k1) / n1)
    first_data = np.kron(_complex_as_real(f1[:, :half]), eye)
    first_filt = np.kron(np.concatenate([f1.real, f1.imag], axis=0), eye)
    inv1 = np.exp(2j * np.pi * np.outer(np.arange(half), k1) / n1) / n
    last = np.kron(_complex_as_real(inv1), eye)
    i2 = np.arange(DFT_RADIX)
    phase = (np.outer(i2, i2)[None] / DFT_RADIX + (i2[None, None, :] * k1[:, None, None]) / n)
    second = np.exp(-2j * np.pi * phase)
    fwd = np.stack([_complex_as_real(second[k]) for k in range(n1)])
    inv = np.stack([_complex_as_real(np.conj(second[k]).T) for k in range(n1)])
    return tuple(np.asarray(a, np.float32) for a in (first_data, first_filt, last, fwd, inv))


def _kron_kernel(mat_ref, x_ref, o_ref):
    lanes = x_ref.shape[-1]
    x = x_ref[...].reshape(-1, lanes).astype(BF16)
    o_ref[...] = _mdot(mat_ref[...], x).reshape(o_ref.shape)


def _kron_out_kernel(mat_ref, b_ref, x0_ref, u_ref, bias_ref, o_ref):
    lanes = b_ref.shape[-1]
    y = _mdot(mat_ref[...], b_ref[...].reshape(-1, lanes).astype(BF16)).reshape(o_ref.shape)
    o_ref[...] = x0_ref[...] * (y + u_ref[...] * bias_ref[...])


def _second_stage_kernel(a_ref, g_ref, gi_ref, kf_ref, o_ref):
    lanes = a_ref.shape[-1]
    r = DFT_RADIX
    for j in range(a_ref.shape[1]):
        spec = _mdot(g_ref[j], a_ref[:, j].reshape(2 * r, lanes).astype(BF16))
        xr, xi = spec[:r], spec[r:]
        kr, ki = kf_ref[0, j], kf_ref[1, j]
        prod = jnp.concatenate([xr * kr - xi * ki, xr * ki + xi * kr], axis=0).astype(BF16)
        o_ref[:, j] = _mdot(gi_ref[j], prod).reshape(2, r, lanes)


def _second_stage_filter_kernel(a_ref, g_ref, o_ref):
    lanes = a_ref.shape[-1]
    for j in range(a_ref.shape[1]):
        o_ref[:, j] = _mdot(g_ref[j], a_ref[:, j].reshape(2 * DFT_RADIX, lanes).astype(BF16)).reshape(
            2, DFT_RADIX, lanes)


def _long_conv_gate(u, x0, kfilt, bias, batch, length):
    c = u.shape[1]
    pairs = batch // 2
    n1 = 2 * length // DFT_RADIX
    half = n1 // 2
    r, s8, cb, kb = DFT_RADIX, V7X_SUBLANES, min(DFT_LANES, c), DFT_K1_PER_STEP
    assert batch % 2 == 0 and n1 % kb == 0 and c % cb == 0
    first_data, first_filt, last, fwd, inv = (jnp.asarray(t, BF16) for t in _dft_tables(length))
    groups = r // s8
    big = 48 * 1024 * 1024

    u5 = u.reshape(pairs, 2, half, r, c)
    x05 = x0.reshape(pairs, 2, half, r, c)
    a_data = pl.pallas_call(
        _kron_kernel,
        grid=(pairs, groups, c // cb),
        in_specs=[_resident(first_data.shape),
                  pl.BlockSpec((None, 2, half, s8, cb), lambda p, g, l: (p, 0, 0, g, l))],
        out_specs=pl.BlockSpec((None, 2, n1, s8, cb), lambda p, g, l: (p, 0, 0, g, l)),
        out_shape=jax.ShapeDtypeStruct((pairs, 2, n1, r, c), F32),
        name="dft_first",
        compiler_params=_params(("parallel", "parallel", "parallel"), big),
    )(first_data, u5)

    a_filt = pl.pallas_call(
        _kron_kernel,
        grid=(groups, c // cb),
        in_specs=[_resident(first_filt.shape),
                  pl.BlockSpec((n1, s8, cb), lambda g, l: (0, g, l))],
        out_specs=pl.BlockSpec((2, n1, s8, cb), lambda g, l: (0, 0, g, l)),
        out_shape=jax.ShapeDtypeStruct((2, n1, r, c), F32),
        name="dft_first_filter",
        compiler_params=_params(("parallel", "parallel"), big),
    )(first_filt, kfilt.reshape(n1, r, c))

    k_spec = pl.pallas_call(
        _second_stage_filter_kernel,
        grid=(c // cb, n1 // kb),
        in_specs=[pl.BlockSpec((2, kb, r, cb), lambda l, k: (0, k, 0, l)),
                  pl.BlockSpec((kb, 2 * r, 2 * r), lambda l, k: (k, 0, 0))],
        out_specs=pl.BlockSpec((2, kb, r, cb), lambda l, k: (0, k, 0, l)),
        out_shape=jax.ShapeDtypeStruct((2, n1, r, c), F32),
        name="dft_second_filter",
        compiler_params=_params(("parallel", "parallel"), big),
    )(a_filt, fwd)

    b_data = pl.pallas_call(
        _second_stage_kernel,
        grid=(c // cb, n1 // kb, pairs),
        in_specs=[pl.BlockSpec((None, 2, kb, r, cb), lambda l, k, p: (p, 0, k, 0, l)),
                  pl.BlockSpec((kb, 2 * r, 2 * r), lambda l, k, p: (k, 0, 0)),
                  pl.BlockSpec((kb, 2 * r, 2 * r), lambda l, k, p: (k, 0, 0)),
                  pl.BlockSpec((2, kb, r, cb), lambda l, k, p: (0, k, 0, l))],
        out_specs=pl.BlockSpec((None, 2, kb, r, cb), lambda l, k, p: (p, 0, k, 0, l)),
        out_shape=jax.ShapeDtypeStruct((pairs, 2, n1, r, c), F32),
        name="dft_second",
        compiler_params=_params(("parallel", "parallel", "arbitrary"), big),
    )(a_data, fwd, inv, k_spec)

    z = pl.pallas_call(
        _kron_out_kernel,
        grid=(pairs, groups, c // cb),
        in_specs=[_resident(last.shape),
                  pl.BlockSpec((None, 2, n1, s8, cb), lambda p, g, l: (p, 0, 0, g, l)),
                  pl.BlockSpec((None, 2, half, s8, cb), lambda p, g, l: (p, 0, 0, g, l)),
                  pl.BlockSpec((None, 2, half, s8, cb), lambda p, g, l: (p, 0, 0, g, l)),
                  pl.BlockSpec((1, cb), lambda p, g, l: (0, l))],
        out_specs=pl.BlockSpec((None, 2, half, s8, cb), lambda p, g, l: (p, 0, 0, g, l)),
        out_shape=jax.ShapeDtypeStruct((pairs, 2, half, r, c), F32),
        name="dft_last",
        compiler_params=_params(("parallel", "parallel", "parallel"), big),
    )(last, b_data, x05, u5, bias.reshape(1, c))
    return z.reshape(batch * length, c)


@functools.lru_cache(maxsize=None)
def _dense_dft_tables(length):
    n = 2 * length
    idx = np.arange(n)
    f = np.exp(-2j * np.pi * np.outer(idx, idx) / n)
    fwd_data = _complex_as_real(f[:, :length])
    fwd_filt = np.concatenate([f.real, f.imag], axis=0)
    inv = _complex_as_real(np.conj(f[:length, :]) / n)
    return tuple(np.asarray(a, np.float32) for a in (fwd_data, fwd_filt, inv))


def _short_conv_gate_kernel(fd_ref, ff_ref, inv_ref, u_ref, x0_ref, k_ref, bias_ref, o_ref):
    n = k_ref.shape[0]
    u = u_ref[...]
    spec = _mdot(fd_ref[...], u.astype(BF16))
    kf = _mdot(ff_ref[...], k_ref[...].astype(BF16))
    xr, xi, kr, ki = spec[:n], spec[n:], kf[:n], kf[n:]
    prod = jnp.concatenate([xr * kr - xi * ki, xr * ki + xi * kr], axis=0).astype(BF16)
    o_ref[...] = x0_ref[...] * (_mdot(inv_ref[...], prod) + u * bias_ref[...])


def _long_conv_gate_short(u, x0, kfilt, bias, batch, length):
    c = u.shape[1]
    pairs = batch // 2
    cb = min(DFT_LANES, c)
    fwd_data, fwd_filt, inv = (jnp.asarray(t, BF16) for t in _dense_dft_tables(length))
    pair_spec = pl.BlockSpec((None, 2 * length, cb), lambda p, l: (p, 0, l))
    z = pl.pallas_call(
        _short_conv_gate_kernel,
        grid=(pairs, c // cb),
        in_specs=[_resident(fwd_data.shape), _resident(fwd_filt.shape), _resident(inv.shape),
                  pair_spec, pair_spec,
                  pl.BlockSpec((2 * length, cb), lambda p, l: (0, l)),
                  pl.BlockSpec((1, cb), lambda p, l: (0, l))],
        out_specs=pair_spec,
        out_shape=jax.ShapeDtypeStruct((pairs, 2 * length, c), F32),
        name="short_conv_gate",
        compiler_params=_params(("parallel", "parallel"), 0),
    )(fwd_data, fwd_filt, inv, u.reshape(pairs, 2 * length, c), x0.reshape(pairs, 2 * length, c),
      kfilt, bias.reshape(1, c))
    return z.reshape(batch * length, c)


def kernel(x, c, ctx, c_ctx, w_mod, b_mod, norm_g, ffn_w_gate, ffn_w_up, ffn_w_down, ab_w_in, na_rpb,
           sc_conv_w, ab_w_out, hy_w_in, hy_short_w, hy_f_w1, hy_f_b1, hy_f_w2, hy_f_b2, hy_f_w3, hy_f_b3,
           hy_f_w4, hy_sin_freq, hy_bias, hy_w_out):
    batch, seq, d = x.shape
    ctx_len = ctx.shape[1]
    depth = w_mod.shape[0]
    assert batch + 1 <= V7X_SUBLANES and seq % ROW_TILE == 0
    last_attn = (depth - 1) - (depth - 1) % 2
    sc_width = sc_conv_w.shape[-1]

    cc = jnp.zeros((V7X_SUBLANES, d), F32).at[:batch].set(c).at[batch].set(c_ctx)
    mod = _modulation(cc, w_mod, b_mod).reshape(depth, V7X_SUBLANES, N_MOD, d)
    lat_batch = lambda i, tm: i // (seq // tm)
    ctx_batch = lambda i, tm: batch

    xs = x.reshape(batch * seq, d)
    cs = ctx.reshape(batch * ctx_len, d)

    for i in range(depth):
        j = i // 2
        g = norm_g[i]
        m_i = mod[i]
        keep_ctx = i <= last_attn
        upd_ctx = i < last_attn
        ffn_w = lambda k: (ffn_w_gate[i, k].astype(BF16), ffn_w_up[i, k].astype(BF16),
                           ffn_w_down[i, k].astype(BF16))

        w1 = ffn_w(0)
        xs = _ffn(xs, m_i, g, *w1, mrow=0, grow=0, batch_of=lat_batch)
        if keep_ctx:
            cs = _ffn(cs, m_i, g, *w1, mrow=0, grow=0, batch_of=ctx_batch)

        if i % 2 == 0:
            w_in = ab_w_in[j].astype(BF16)
            w_out = ab_w_out[j].astype(BF16)
            qkv_out, b_out = (3 * NA_WIDTH, BF16), (sc_width, BF16)
            in_ab = functools.partial(_inproj_call, _inproj_ab_kernel, "inproj_ab", mrow=3, grow=2)
            qkv, b_lat = in_ab(xs, m_i, g, w_in, sc_conv_w[j], [qkv_out, b_out], seq_len=seq,
                               batch_of=lat_batch)
            if upd_ctx:
                qkv_c, b_ctx = in_ab(cs, m_i, g, w_in, sc_conv_w[j], [qkv_out, b_out], seq_len=ctx_len,
                                     batch_of=ctx_batch)
            else:
                (qkv_c,) = in_ab(cs, m_i, g, w_in, sc_conv_w[j], [qkv_out], seq_len=ctx_len,
                                 batch_of=ctx_batch)
            a_lat = _neighbourhood_attention(qkv, qkv_c, na_rpb[j], batch, seq, ctx_len)
            xs = _outproj([a_lat, b_lat], w_out, xs, m_i, g, mrow=5, grow=3, batch_of=lat_batch)
            if upd_ctx:
                a_ctx = _context_attention(qkv_c, batch, ctx_len)
                cs = _outproj([a_ctx, b_ctx], w_out, cs, m_i, g, mrow=5, grow=3, batch_of=ctx_batch)
        else:
            w_in = hy_w_in[j].astype(BF16)
            w_out = hy_w_out[j].astype(BF16)
            hw = w_out.shape[0]
            filt = lambda n: _hyena_filter(n, hy_f_w1[j], hy_f_b1[j], hy_f_w2[j], hy_f_b2[j], hy_f_w3[j],
                                           hy_f_b3[j], hy_f_w4[j], hy_sin_freq[j])
            in_hy = functools.partial(_inproj_call, _inproj_hyena_kernel, "inproj_hyena", mrow=3, grow=2)
            x0, u = in_hy(xs, m_i, g, w_in, hy_short_w[j], [(hw, F32), (hw, F32)], seq_len=seq,
                          batch_of=lat_batch)
            z = _long_conv_gate(u, x0, filt(seq), hy_bias[j], batch, seq)
            xs = _outproj([z], w_out, xs, m_i, g, mrow=5, grow=3, batch_of=lat_batch)
            if upd_ctx:
                x0_c, u_c = in_hy(cs, m_i, g, w_in, hy_short_w[j], [(hw, F32), (hw, F32)], seq_len=ctx_len,
                                  batch_of=ctx_batch)
                z_c = _long_conv_gate_short(u_c, x0_c, filt(ctx_len), hy_bias[j], batch, ctx_len)
                cs = _outproj([z_c], w_out, cs, m_i, g, mrow=5, grow=3, batch_of=ctx_batch)

        w2 = ffn_w(1)
        xs = _ffn(xs, m_i, g, *w2, mrow=6, grow=4, batch_of=lat_batch)
        if upd_ctx:
            cs = _ffn(cs, m_i, g, *w2, mrow=6, grow=4, batch_of=ctx_batch)

    return xs.reshape(batch, seq, d)
```

```python
import functools
import math

import numpy as np
import jax
import jax.numpy as jnp
from jax import lax
from jax.experimental import pallas as pl
from jax.experimental.pallas import tpu as pltpu

F32 = jnp.float32
BF16 = jnp.bfloat16

GRID_W = 64
NA_HEADS = 8
NA_HEAD_DIM = 64
NA_WIDTH = NA_HEADS * NA_HEAD_DIM
NA_KH = 8
NA_KW = 16
FFN_RES = 0.5
N_MOD = 9
RMS_EPS = 1e-6
HY_EMB = 33
HY_BANDS = (HY_EMB - 1) // 2
HY_TARGET = 1e-2
HY_FAST_PCT = 0.3
HY_SLOW_PCT = 1.5

V7X_LANES = 128
V7X_SUBLANES = 8
V7X_MXU_DIM = 256
V7X_VMEM_BYTES = 64 * 1024 * 1024

ROW_TILE = 512
FFN_CHUNK = 256
FFN_SUBBLOCKS = 1
PROJ_CHUNK = 512
NA_ROWS_PER_STEP = 8
DFT_RADIX = 128
DFT_LANES = 256
DFT_K1_PER_STEP = 8
FILTER_ROWS = 512
MASK_VALUE = -1e30


def _vmem_limit(nbytes):
    return int(min(max(nbytes, 32 * 1024 * 1024), V7X_VMEM_BYTES - 8 * 1024 * 1024))


def _params(semantics, vmem_bytes):
    return pltpu.CompilerParams(dimension_semantics=semantics, vmem_limit_bytes=_vmem_limit(vmem_bytes))


def _resident(shape):
    zeros = (0,) * len(shape)
    return pl.BlockSpec(shape, lambda *_: zeros, pipeline_mode=pl.Buffered(1))


def _rms(x):
    return x * lax.rsqrt(jnp.mean(x * x, axis=-1, keepdims=True) + RMS_EPS)


def _mdot(a, b):
    return jnp.dot(a, b, preferred_element_type=F32)


def _mod_kernel(c_ref, w_ref, b_ref, o_ref):
    c = c_ref[...]
    s = (c * jax.nn.sigmoid(c)).astype(BF16)
    o_ref[...] = _mdot(s, w_ref[...].astype(BF16)) + b_ref[...]


def _modulation(cc, w_mod, b_mod):
    depth, d, nd = w_mod.shape
    return pl.pallas_call(
        _mod_kernel,
        grid=(depth, nd // d),
        in_specs=[pl.BlockSpec((V7X_SUBLANES, d), lambda i, j: (0, 0)),
                  pl.BlockSpec((None, d, d), lambda i, j: (i, 0, j)),
                  pl.BlockSpec((None, 1, d), lambda i, j: (i, 0, j))],
        out_specs=pl.BlockSpec((None, V7X_SUBLANES, d), lambda i, j: (i, 0, j)),
        out_shape=jax.ShapeDtypeStruct((depth, V7X_SUBLANES, nd), F32),
        name="modulation",
        compiler_params=_params(("parallel", "parallel"), 4 * d * d * 4),
    )(cc, w_mod, b_mod.reshape(depth, 1, nd))


def _ffn_kernel(*refs, n_parts, mix_mrow, mix_grow, mrow, grow):
    part_refs = refs[:n_parts]
    w_out_ref = refs[n_parts] if n_parts else None
    x_ref, m_ref, g_ref, wg_ref, wu_ref, wd_ref, o_ref, acc_ref = refs[n_parts + bool(n_parts):]
    shift, scale, gate = m_ref[mrow:mrow + 1, :], m_ref[mrow + 1:mrow + 2, :], m_ref[mrow + 2:mrow + 3, :]
    hm = x_ref.shape[0] // FFN_SUBBLOCKS
    subs = [slice(s * hm, (s + 1) * hm) for s in range(FFN_SUBBLOCKS)]
    xs, hs = [], []
    for rows in subs:
        x = x_ref[rows, :]
        if n_parts:
            y, r0 = None, 0
            for p_ref in part_refs:
                r1 = r0 + p_ref.shape[1]
                term = _mdot(p_ref[rows, :].astype(BF16), w_out_ref[r0:r1, :])
                y = term if y is None else y + term
                r0 = r1
            x = x + m_ref[mix_mrow:mix_mrow + 1, :] * (_rms(y) * g_ref[mix_grow:mix_grow + 1, :])
        xs.append(x)
        hs.append(((_rms(x) * g_ref[grow:grow + 1, :]) * (1.0 + scale) + shift).astype(BF16))
    for j in range(wg_ref.shape[1] // FFN_CHUNK):
        c0, c1 = j * FFN_CHUNK, (j + 1) * FFN_CHUNK
        for rows, h in zip(subs, hs):
            g = _mdot(h, wg_ref[:, c0:c1])
            u = _mdot(h, wu_ref[:, c0:c1])
            a = (g * jax.nn.sigmoid(g) * u).astype(BF16)
            part = _mdot(a, wd_ref[c0:c1, :])
            if j == 0:
                acc_ref[rows, :] = part
            else:
                acc_ref[rows, :] += part
    for rows, x in zip(subs, xs):
        y = acc_ref[rows, :]
        o_ref[rows, :] = x + FFN_RES * gate * (_rms(y) * g_ref[grow + 1:grow + 2, :])


def _layer_weight(w, index):
    lead = len(index)
    return pl.BlockSpec((None,) * lead + w.shape[lead:], lambda *_: tuple(index) + (0, 0),
                        pipeline_mode=pl.Buffered(1))


def _ffn(x, mod_i, g, wg, wu, wd, layer, *, mrow, grow, batch_of, mixer=None, mix_mrow=None, mix_grow=None):
    m, d = x.shape
    f = wg.shape[-1]
    tm = min(ROW_TILE, m)
    parts, w_out = mixer if mixer else ((), None)
    kern = functools.partial(_ffn_kernel, n_parts=len(parts), mix_mrow=mix_mrow, mix_grow=mix_grow,
                             mrow=mrow, grow=grow)
    part_bytes = sum(p.shape[1] * p.dtype.itemsize for p in parts)
    vmem = (3 * d * f * 2 + 4 * tm * d * 4 + tm * d * 4 + 6 * tm * FFN_CHUNK * 4 + tm * d * 12
            + 2 * tm * part_bytes + (d * d * 2 if parts else 0))
    mix_specs = [pl.BlockSpec((tm, p.shape[1]), lambda i: (i, 0)) for p in parts]
    if parts:
        mix_specs.append(_resident(w_out.shape))
    return pl.pallas_call(
        kern,
        grid=(m // tm,),
        in_specs=mix_specs + [
            pl.BlockSpec((tm, d), lambda i: (i, 0)),
            pl.BlockSpec((None, N_MOD, d), lambda i: (batch_of(i, tm), 0, 0)),
            _resident(g.shape), _layer_weight(wg, layer), _layer_weight(wu, layer), _layer_weight(wd, layer)],
        out_specs=pl.BlockSpec((tm, d), lambda i: (i, 0)),
        out_shape=jax.ShapeDtypeStruct((m, d), F32),
        scratch_shapes=[pltpu.VMEM((tm, d), F32)],
        name="ffn_mix" if parts else "ffn",
        compiler_params=_params(("parallel",), vmem),
    )(*parts, *([w_out] if parts else []), x, mod_i, g, wg, wu, wd)


def _conv3_rows(pe, w0, w1, w2, tm):
    n = pe.shape[0]
    lo = V7X_SUBLANES
    down = pltpu.roll(pe, 1, 0)[lo:lo + tm]
    up = pltpu.roll(pe, n - 1, 0)[lo:lo + tm]
    return down * w0 + pe[lo:lo + tm] * w1 + up * w2


def _normed_with_halo(x_ref, prev_ref, next_ref, m_ref, g_ref, mrow, grow, seq_len):
    tm = x_ref.shape[0]
    lo = V7X_SUBLANES
    shift, scale = m_ref[mrow:mrow + 1, :], m_ref[mrow + 1:mrow + 2, :]
    xe = jnp.concatenate([prev_ref[...], x_ref[...], next_ref[...]], axis=0)
    he = (_rms(xe) * g_ref[grow:grow + 1, :]) * (1.0 + scale) + shift
    start = (pl.program_id(0) * tm) & (seq_len - 1)
    keep_prev = (start != 0).astype(F32)
    keep_next = (start + tm != seq_len).astype(F32)
    row = lax.broadcasted_iota(jnp.int32, (tm + 2 * lo, 1), 0)
    keep = jnp.where(row < lo, keep_prev, jnp.where(row >= lo + tm, keep_next, 1.0))
    return he[lo:lo + tm].astype(BF16), (he * keep).astype(BF16)


def _inproj_ab_kernel(x_ref, prev_ref, next_ref, m_ref, g_ref, w_ref, cw_ref, qkv_ref, *b_refs,
                      mrow, grow, seq_len):
    tm = x_ref.shape[0]
    h, he = _normed_with_halo(x_ref, prev_ref, next_ref, m_ref, g_ref, mrow, grow, seq_len)
    nq = qkv_ref.shape[1]
    for c in range(0, nq, PROJ_CHUNK):
        qkv_ref[:, c:c + PROJ_CHUNK] = _mdot(h, w_ref[:, c:c + PROJ_CHUNK]).astype(qkv_ref.dtype)
    if b_refs:
        (b_ref,) = b_refs
        sw = cw_ref.shape[1]
        gate = _mdot(h, w_ref[:, nq:nq + sw])
        p = _mdot(he, w_ref[:, nq + sw:nq + 2 * sw]) * _mdot(he, w_ref[:, nq + 2 * sw:nq + 3 * sw])
        conv = _conv3_rows(p, cw_ref[0:1, :], cw_ref[1:2, :], cw_ref[2:3, :], tm)
        b_ref[...] = (gate * conv).astype(b_ref.dtype)


def _inproj_hyena_kernel(x_ref, prev_ref, next_ref, m_ref, g_ref, w_ref, cw_ref, x0_ref, u_ref,
                         *, mrow, grow, seq_len):
    tm = x_ref.shape[0]
    _, he = _normed_with_halo(x_ref, prev_ref, next_ref, m_ref, g_ref, mrow, grow, seq_len)
    c = x0_ref.shape[1]

    def conv_cols(c0):
        cols = slice(c0, c0 + PROJ_CHUNK)
        return _conv3_rows(_mdot(he, w_ref[:, cols]), cw_ref[0:1, cols], cw_ref[1:2, cols], cw_ref[2:3, cols], tm)

    for c0 in range(0, c, PROJ_CHUNK):
        x0_ref[:, c0:c0 + PROJ_CHUNK] = conv_cols(c0)
        u_ref[:, c0:c0 + PROJ_CHUNK] = conv_cols(2 * c + c0) * conv_cols(c + c0)


def _halo_specs(tm, m, width):
    blocks = m // V7X_SUBLANES
    per = tm // V7X_SUBLANES
    prev = pl.BlockSpec((V7X_SUBLANES, width), lambda i: (jnp.maximum(i * per - 1, 0), 0))
    nxt = pl.BlockSpec((V7X_SUBLANES, width), lambda i: (jnp.minimum((i + 1) * per, blocks - 1), 0))
    return prev, nxt


def _inproj_call(kern, name, x, mod_i, g, w, conv_w, outs, *, mrow, grow, seq_len, batch_of):
    m, d = x.shape
    tm = min(ROW_TILE, seq_len)
    assert seq_len & (seq_len - 1) == 0 and seq_len % tm == 0 and m % seq_len == 0
    prev, nxt = _halo_specs(tm, m, d)
    out_bytes = sum(cols * jnp.dtype(dt).itemsize for cols, dt in outs)
    vmem = w.size * 2 + 2 * tm * d * 4 + 2 * tm * out_bytes + 6 * tm * d * 4
    return pl.pallas_call(
        functools.partial(kern, mrow=mrow, grow=grow, seq_len=seq_len),
        grid=(m // tm,),
        in_specs=[pl.BlockSpec((tm, d), lambda i: (i, 0)), prev, nxt,
                  pl.BlockSpec((None, N_MOD, d), lambda i: (batch_of(i, tm), 0, 0)),
                  _resident(g.shape), _resident(w.shape), _resident(conv_w.shape)],
        out_specs=[pl.BlockSpec((tm, cols), lambda i: (i, 0)) for cols, _ in outs],
        out_shape=[jax.ShapeDtypeStruct((m, cols), dt) for cols, dt in outs],
        name=name,
        compiler_params=_params(("parallel",), vmem),
    )(x, x, x, mod_i, g, w, conv_w)


def _split_heads_on_rows(q, low_lanes):
    zero = jnp.zeros_like(q)
    return jnp.concatenate([jnp.where(low_lanes, q, zero), jnp.where(low_lanes, zero, q)], axis=0)


def _na_kernel(q_ref, k_ref, v_ref, kc_ref, vc_ref, bt_ref, o_ref, *, rows):
    rb = pl.program_id(2)
    low_lanes = lax.broadcasted_iota(jnp.int32, (GRID_W, V7X_LANES), 1) < NA_HEAD_DIM
    kc = kc_ref[...]
    vc = vc_ref[...]
    nt = (((1,), (1,)), ((), ()))

    for i in range(NA_ROWS_PER_STEP):
        r = rb * NA_ROWS_PER_STEP + i
        rs = jnp.clip(r - NA_KH // 2, 0, rows - NA_KH)
        q = q_ref[i * GRID_W:(i + 1) * GRID_W, :] * (NA_HEAD_DIM ** -0.5)
        q2 = _split_heads_on_rows(q, low_lanes)
        k0 = pl.multiple_of(rs * GRID_W, GRID_W)
        kk = k_ref[pl.ds(k0, NA_KH * GRID_W), :]
        vv = v_ref[pl.ds(k0, NA_KH * GRID_W), :]
        s = lax.dot_general(q2, kk, nt, preferred_element_type=F32) + bt_ref[rs - r + (NA_KH - 1)]
        sc = lax.dot_general(q2, kc, nt, preferred_element_type=F32)
        mx = jnp.maximum(jnp.max(s, axis=-1, keepdims=True), jnp.max(sc, axis=-1, keepdims=True))
        p = jnp.exp(s - mx)
        pc = jnp.exp(sc - mx)
        den = jnp.sum(p, axis=-1, keepdims=True) + jnp.sum(pc, axis=-1, keepdims=True)
        o2 = (_mdot(p.astype(BF16), vv) + _mdot(pc.astype(BF16), vc)) / den
        o = jnp.where(low_lanes, o2[:GRID_W], o2[GRID_W:])
        o_ref[i * GRID_W:(i + 1) * GRID_W, :] = o.astype(o_ref.dtype)


def _na_bias_table(rpb):
    h = rpb.shape[0]
    c = jnp.arange(GRID_W)[:, None]
    kc = jnp.arange(GRID_W)[None, :]
    start = jnp.clip(c - NA_KW // 2, 0, GRID_W - NA_KW)
    valid = (kc >= start) & (kc < start + NA_KW)
    idx = jnp.clip(kc - c + (NA_KW - 1), 0, 2 * NA_KW - 2)
    dense = jnp.where(valid[None, None], rpb[:, :, idx], MASK_VALUE)
    off = jnp.arange(NA_KH)[:, None] + jnp.arange(NA_KH)[None, :]
    tab = dense[:, off]
    tab = tab.transpose(0, 1, 3, 2, 4).reshape(h, NA_KH, GRID_W, NA_KH * GRID_W)
    tab = tab.reshape(h // 2, 2, NA_KH, GRID_W, NA_KH * GRID_W).transpose(0, 2, 1, 3, 4)
    return tab.reshape(h // 2, NA_KH, 2 * GRID_W, NA_KH * GRID_W)


def _neighbourhood_attention(qkv, qkv_c, rpb, batch, seq, ctx_len):
    rows = seq // GRID_W
    assert rows >= NA_KH and rows % NA_ROWS_PER_STEP == 0
    nq = NA_WIDTH // V7X_LANES
    qkv3 = qkv.reshape(batch, seq, 3 * NA_WIDTH)
    qkvc3 = qkv_c.reshape(batch, ctx_len, 3 * NA_WIDTH)
    tq = NA_ROWS_PER_STEP * GRID_W
    table = _na_bias_table(rpb)
    vmem = 4 * seq * V7X_LANES * 2 + 2 * table[0].size * 4 + 16 * 1024 * 1024
    out = pl.pallas_call(
        functools.partial(_na_kernel, rows=rows),
        grid=(batch, nq, rows // NA_ROWS_PER_STEP),
        in_specs=[pl.BlockSpec((None, tq, V7X_LANES), lambda b, h, r: (b, r, h)),
                  pl.BlockSpec((None, seq, V7X_LANES), lambda b, h, r: (b, 0, nq + h)),
                  pl.BlockSpec((None, seq, V7X_LANES), lambda b, h, r: (b, 0, 2 * nq + h)),
                  pl.BlockSpec((None, ctx_len, V7X_LANES), lambda b, h, r: (b, 0, nq + h)),
                  pl.BlockSpec((None, ctx_len, V7X_LANES), lambda b, h, r: (b, 0, 2 * nq + h)),
                  pl.BlockSpec((None,) + table.shape[1:], lambda b, h, r: (h, 0, 0, 0))],
        out_specs=pl.BlockSpec((None, tq, V7X_LANES), lambda b, h, r: (b, r, h)),
        out_shape=jax.ShapeDtypeStruct((batch, seq, NA_WIDTH), BF16),
        name="na_attn",
        compiler_params=_params(("parallel", "parallel", "arbitrary"), vmem),
    )(qkv3, qkv3, qkv3, qkvc3, qkvc3, table)
    return out.reshape(batch * seq, NA_WIDTH)


def _ctx_attn_kernel(q_ref, k_ref, v_ref, o_ref):
    n = q_ref.shape[0]
    low_lanes = lax.broadcasted_iota(jnp.int32, (n, V7X_LANES), 1) < NA_HEAD_DIM
    q2 = _split_heads_on_rows(q_ref[...] * (NA_HEAD_DIM ** -0.5), low_lanes)
    s = lax.dot_general(q2, k_ref[...], (((1,), (1,)), ((), ())), preferred_element_type=F32)
    p = jnp.exp(s - jnp.max(s, axis=-1, keepdims=True))
    o2 = _mdot(p.astype(BF16), v_ref[...]) / jnp.sum(p, axis=-1, keepdims=True)
    o_ref[...] = jnp.where(low_lanes, o2[:n], o2[n:]).astype(o_ref.dtype)


def _context_attention(qkv_c, batch, ctx_len):
    nq = NA_WIDTH // V7X_LANES
    qkvc3 = qkv_c.reshape(batch, ctx_len, 3 * NA_WIDTH)
    out = pl.pallas_call(
        _ctx_attn_kernel,
        grid=(batch, nq),
        in_specs=[pl.BlockSpec((None, ctx_len, V7X_LANES), lambda b, h: (b, 0, h)),
                  pl.BlockSpec((None, ctx_len, V7X_LANES), lambda b, h: (b, 0, nq + h)),
                  pl.BlockSpec((None, ctx_len, V7X_LANES), lambda b, h: (b, 0, 2 * nq + h))],
        out_specs=pl.BlockSpec((None, ctx_len, V7X_LANES), lambda b, h: (b, 0, h)),
        out_shape=jax.ShapeDtypeStruct((batch, ctx_len, NA_WIDTH), BF16),
        name="ctx_attn",
        compiler_params=_params(("parallel", "parallel"), 0),
    )(qkvc3, qkvc3, qkvc3)
    return out.reshape(batch * ctx_len, NA_WIDTH)


def _filter_kernel(z_ref, w1_ref, b1_ref, w2_ref, b2_ref, w3_ref, b3_ref, w4_ref, fr_ref, dec_ref, o_ref,
                   *, length):
    hi = lax.Precision.HIGHEST
    z = z_ref[...]
    fr = fr_ref[...]
    h = jnp.sin(fr * (jnp.dot(z, w1_ref[...], precision=hi, preferred_element_type=F32) + b1_ref[...]))
    h = jnp.sin(fr * (jnp.dot(h, w2_ref[...], precision=hi, preferred_element_type=F32) + b2_ref[...]))
    h = jnp.sin(fr * (jnp.dot(h, w3_ref[...], precision=hi, preferred_element_type=F32) + b3_ref[...]))
    filt = _mdot(h.astype(BF16), w4_ref[...].astype(BF16))
    out = filt * jnp.exp(-z[:, 0:1] * dec_ref[...])
    row = pl.program_id(0) * z.shape[0] + lax.broadcasted_iota(jnp.int32, out.shape, 0)
    o_ref[...] = jnp.where(row == length, 0.0, out)


@functools.lru_cache(maxsize=None)
def _filter_positions(length):
    t = np.linspace(0.0, 1.0, length)[:, None]
    w = 2.0 * math.pi * np.arange(length)[:, None] / length
    bands = np.linspace(1e-4, HY_BANDS - 1, HY_BANDS)[None, :]
    z = np.concatenate([t, np.cos(bands * w), -np.sin(bands * w)], axis=-1)
    z2 = np.concatenate([z, z[0:1], z[:0:-1]], axis=0)
    out = np.zeros((2 * length, V7X_LANES), np.float32)
    out[:, :HY_EMB] = z2
    return out


def _hyena_filter(length, w1, b1, w2, b2, w3, b3, w4, freq):
    c = w4.shape[1] // 2
    ffn = w2.shape[0]
    tr = min(FILTER_ROWS, length)
    per_half = length // tr
    zpos = jnp.asarray(_filter_positions(length))
    w1p = jnp.pad(w1, ((0, V7X_LANES - w1.shape[0]), (0, 0)))
    decay = np.abs(np.linspace(math.log(HY_TARGET) / HY_SLOW_PCT, math.log(HY_TARGET) / HY_FAST_PCT, c))
    decay = jnp.asarray(decay[None, :], F32)
    row = lambda v: v.reshape(1, -1)
    small = lambda shape: pl.BlockSpec(shape, lambda i: (0, 0))
    return pl.pallas_call(
        functools.partial(_filter_kernel, length=length),
        grid=(2 * per_half,),
        in_specs=[pl.BlockSpec((tr, V7X_LANES), lambda i: (i, 0)),
                  small((V7X_LANES, ffn)), small((1, ffn)), small((ffn, ffn)), small((1, ffn)),
                  small((ffn, ffn)), small((1, ffn)),
                  pl.BlockSpec((ffn, c), lambda i: (0, i // per_half)),
                  small((1, ffn)), small((1, c))],
        out_specs=pl.BlockSpec((tr, c), lambda i: (i, 0)),
        out_shape=jax.ShapeDtypeStruct((2 * length, c), F32),
        name="hyena_filter",
        compiler_params=_params(("parallel",), 0),
    )(zpos, w1p, row(b1), w2, row(b2), w3, row(b3), w4, row(freq), decay)


def _complex_as_real(m):
    return np.block([[m.real, -m.imag], [m.imag, m.real]])


@functools.lru_cache(maxsize=None)
def _dft_tables(length):
    n = 2 * length
    n1 = n // DFT_RADIX
    half = n1 // 2
    eye = np.eye(V7X_SUBLANES)
    k1 = np.arange(n1)
    f1 = np.exp(-2j * np.pi * np.outer(k1, k1) / n1)
    first_data = np.kron(_complex_as_real(f1[:, :half]), eye)
    first_filt = np.kron(np.concatenate([f1.real, f1.imag], axis=0), eye)
    inv1 = np.exp(2j * np.pi * np.outer(np.arange(half), k1) / n1) / n
    last = np.kron(_complex_as_real(inv1), eye)
    i2 = np.arange(DFT_RADIX)
    phase = (np.outer(i2, i2)[None] / DFT_RADIX + (i2[None, None, :] * k1[:, None, None]) / n)
    second = np.exp(-2j * np.pi * phase)
    fwd = np.stack([_complex_as_real(second[k]) for k in range(n1)])
    inv = np.stack([_complex_as_real(np.conj(second[k]).T) for k in range(n1)])
    return tuple(np.asarray(a, np.float32) for a in (first_data, first_filt, last, fwd, inv))


def _kron_kernel(mat_ref, x_ref, o_ref):
    lanes = x_ref.shape[-1]
    x = x_ref[...].reshape(-1, lanes).astype(BF16)
    o_ref[...] = _mdot(mat_ref[...], x).reshape(o_ref.shape)


def _kron_out_kernel(mat_ref, b_ref, x0_ref, u_ref, bias_ref, o_ref):
    lanes = b_ref.shape[-1]
    y = _mdot(mat_ref[...], b_ref[...].reshape(-1, lanes).astype(BF16)).reshape(o_ref.shape)
    o_ref[...] = x0_ref[...] * (y + u_ref[...] * bias_ref[...])


def _second_stage_kernel(a_ref, g_ref, gi_ref, kf_ref, o_ref):
    lanes = a_ref.shape[-1]
    r = DFT_RADIX
    for j in range(a_ref.shape[1]):
        spec = _mdot(g_ref[j], a_ref[:, j].reshape(2 * r, lanes).astype(BF16))
        xr, xi = spec[:r], spec[r:]
        kr, ki = kf_ref[0, j], kf_ref[1, j]
        prod = jnp.concatenate([xr * kr - xi * ki, xr * ki + xi * kr], axis=0).astype(BF16)
        o_ref[:, j] = _mdot(gi_ref[j], prod).reshape(2, r, lanes)


def _second_stage_filter_kernel(a_ref, g_ref, o_ref):
    lanes = a_ref.shape[-1]
    for j in range(a_ref.shape[1]):
        o_ref[:, j] = _mdot(g_ref[j], a_ref[:, j].reshape(2 * DFT_RADIX, lanes).astype(BF16)).reshape(
            2, DFT_RADIX, lanes)


def _long_conv_gate(u, x0, kfilt, bias, batch, length):
    c = u.shape[1]
    pairs = batch // 2
    n1 = 2 * length // DFT_RADIX
    half = n1 // 2
    r, s8, cb, kb = DFT_RADIX, V7X_SUBLANES, min(DFT_LANES, c), DFT_K1_PER_STEP
    assert batch % 2 == 0 and n1 % kb == 0 and c % cb == 0
    first_data, first_filt, last, fwd, inv = (jnp.asarray(t, BF16) for t in _dft_tables(length))
    groups = r // s8
    big = 48 * 1024 * 1024

    u5 = u.reshape(pairs, 2, half, r, c)
    x05 = x0.reshape(pairs, 2, half, r, c)
    a_data = pl.pallas_call(
        _kron_kernel,
        grid=(pairs, groups, c // cb),
        in_specs=[_resident(first_data.shape),
                  pl.BlockSpec((None, 2, half, s8, cb), lambda p, g, l: (p, 0, 0, g, l))],
        out_specs=pl.BlockSpec((None, 2, n1, s8, cb), lambda p, g, l: (p, 0, 0, g, l)),
        out_shape=jax.ShapeDtypeStruct((pairs, 2, n1, r, c), F32),
        name="dft_first",
        compiler_params=_params(("parallel", "parallel", "parallel"), big),
    )(first_data, u5)

    a_filt = pl.pallas_call(
        _kron_kernel,
        grid=(groups, c // cb),
        in_specs=[_resident(first_filt.shape),
                  pl.BlockSpec((n1, s8, cb), lambda g, l: (0, g, l))],
        out_specs=pl.BlockSpec((2, n1, s8, cb), lambda g, l: (0, 0, g, l)),
        out_shape=jax.ShapeDtypeStruct((2, n1, r, c), F32),
        name="dft_first_filter",
        compiler_params=_params(("parallel", "parallel"), big),
    )(first_filt, kfilt.reshape(n1, r, c))

    k_spec = pl.pallas_call(
        _second_stage_filter_kernel,
        grid=(c // cb, n1 // kb),
        in_specs=[pl.BlockSpec((2, kb, r, cb), lambda l, k: (0, k, 0, l)),
                  pl.BlockSpec((kb, 2 * r, 2 * r), lambda l, k: (k, 0, 0))],
        out_specs=pl.BlockSpec((2, kb, r, cb), lambda l, k: (0, k, 0, l)),
        out_shape=jax.ShapeDtypeStruct((2, n1, r, c), F32),
        name="dft_second_filter",
        compiler_params=_params(("parallel", "parallel"), big),
    )(a_filt, fwd)

    b_data = pl.pallas_call(
        _second_stage_kernel,
        grid=(c // cb, n1 // kb, pairs),
        in_specs=[pl.BlockSpec((None, 2, kb, r, cb), lambda l, k, p: (p, 0, k, 0, l)),
                  pl.BlockSpec((kb, 2 * r, 2 * r), lambda l, k, p: (k, 0, 0)),
                  pl.BlockSpec((kb, 2 * r, 2 * r), lambda l, k, p: (k, 0, 0)),
                  pl.BlockSpec((2, kb, r, cb), lambda l, k, p: (0, k, 0, l))],
        out_specs=pl.BlockSpec((None, 2, kb, r, cb), lambda l, k, p: (p, 0, k, 0, l)),
        out_shape=jax.ShapeDtypeStruct((pairs, 2, n1, r, c), F32),
        name="dft_second",
        compiler_params=_params(("parallel", "parallel", "arbitrary"), big),
    )(a_data, fwd, inv, k_spec)

    z = pl.pallas_call(
        _kron_out_kernel,
        grid=(pairs, groups, c // cb),
        in_specs=[_resident(last.shape),
                  pl.BlockSpec((None, 2, n1, s8, cb), lambda p, g, l: (p, 0, 0, g, l)),
                  pl.BlockSpec((None, 2, half, s8, cb), lambda p, g, l: (p, 0, 0, g, l)),
                  pl.BlockSpec((None, 2, half, s8, cb), lambda p, g, l: (p, 0, 0, g, l)),
                  pl.BlockSpec((1, cb), lambda p, g, l: (0, l))],
        out_specs=pl.BlockSpec((None, 2, half, s8, cb), lambda p, g, l: (p, 0, 0, g, l)),
        out_shape=jax.ShapeDtypeStruct((pairs, 2, half, r, c), F32),
        name="dft_last",
        compiler_params=_params(("parallel", "parallel", "parallel"), big),
    )(last, b_data, x05, u5, bias.reshape(1, c))
    return z.reshape(batch * length, c)


@functools.lru_cache(maxsize=None)
def _dense_dft_tables(length):
    n = 2 * length
    idx = np.arange(n)
    f = np.exp(-2j * np.pi * np.outer(idx, idx) / n)
    fwd_data = _complex_as_real(f[:, :length])
    fwd_filt = np.concatenate([f.real, f.imag], axis=0)
    inv = _complex_as_real(np.conj(f[:length, :]) / n)
    return tuple(np.asarray(a, np.float32) for a in (fwd_data, fwd_filt, inv))


def _short_conv_gate_kernel(fd_ref, ff_ref, inv_ref, u_ref, x0_ref, k_ref, bias_ref, o_ref):
    n = k_ref.shape[0]
    u = u_ref[...]
    spec = _mdot(fd_ref[...], u.astype(BF16))
    kf = _mdot(ff_ref[...], k_ref[...].astype(BF16))
    xr, xi, kr, ki = spec[:n], spec[n:], kf[:n], kf[n:]
    prod = jnp.concatenate([xr * kr - xi * ki, xr * ki + xi * kr], axis=0).astype(BF16)
    o_ref[...] = x0_ref[...] * (_mdot(inv_ref[...], prod) + u * bias_ref[...])


def _long_conv_gate_short(u, x0, kfilt, bias, batch, length):
    c = u.shape[1]
    pairs = batch // 2
    cb = min(DFT_LANES, c)
    fwd_data, fwd_filt, inv = (jnp.asarray(t, BF16) for t in _dense_dft_tables(length))
    pair_spec = pl.BlockSpec((None, 2 * length, cb), lambda p, l: (p, 0, l))
    z = pl.pallas_call(
        _short_conv_gate_kernel,
        grid=(pairs, c // cb),
        in_specs=[_resident(fwd_data.shape), _resident(fwd_filt.shape), _resident(inv.shape),
                  pair_spec, pair_spec,
                  pl.BlockSpec((2 * length, cb), lambda p, l: (0, l)),
                  pl.BlockSpec((1, cb), lambda p, l: (0, l))],
        out_specs=pair_spec,
        out_shape=jax.ShapeDtypeStruct((pairs, 2 * length, c), F32),
        name="short_conv_gate",
        compiler_params=_params(("parallel", "parallel"), 0),
    )(fwd_data, fwd_filt, inv, u.reshape(pairs, 2 * length, c), x0.reshape(pairs, 2 * length, c),
      kfilt, bias.reshape(1, c))
    return z.reshape(batch * length, c)


def kernel(x, c, ctx, c_ctx, w_mod, b_mod, norm_g, ffn_w_gate, ffn_w_up, ffn_w_down, ab_w_in, na_rpb,
           sc_conv_w, ab_w_out, hy_w_in, hy_short_w, hy_f_w1, hy_f_b1, hy_f_w2, hy_f_b2, hy_f_w3, hy_f_b3,
           hy_f_w4, hy_sin_freq, hy_bias, hy_w_out):
    batch, seq, d = x.shape
    ctx_len = ctx.shape[1]
    depth = w_mod.shape[0]
    assert batch + 1 <= V7X_SUBLANES and seq % ROW_TILE == 0
    last_attn = (depth - 1) - (depth - 1) % 2
    sc_width = sc_conv_w.shape[-1]

    cc = jnp.zeros((V7X_SUBLANES, d), F32).at[:batch].set(c).at[batch].set(c_ctx)
    mod = _modulation(cc, w_mod, b_mod).reshape(depth, V7X_SUBLANES, N_MOD, d)
    lat_batch = lambda i, tm: i // (seq // tm)
    ctx_batch = lambda i, tm: batch

    xs = x.reshape(batch * seq, d)
    cs = ctx.reshape(batch * ctx_len, d)
    ffn_w = (ffn_w_gate.astype(BF16), ffn_w_up.astype(BF16), ffn_w_down.astype(BF16))

    for i in range(depth):
        j = i // 2
        g = norm_g[i]
        m_i = mod[i]
        keep_ctx = i <= last_attn
        upd_ctx = i < last_attn

        xs = _ffn(xs, m_i, g, *ffn_w, (i, 0), mrow=0, grow=0, batch_of=lat_batch)
        if keep_ctx:
            cs = _ffn(cs, m_i, g, *ffn_w, (i, 0), mrow=0, grow=0, batch_of=ctx_batch)

        if i % 2 == 0:
            w_in = ab_w_in[j].astype(BF16)
            w_out = ab_w_out[j].astype(BF16)
            qkv_out, b_out = (3 * NA_WIDTH, BF16), (sc_width, BF16)
            in_ab = functools.partial(_inproj_call, _inproj_ab_kernel, "inproj_ab", mrow=3, grow=2)
            qkv, b_lat = in_ab(xs, m_i, g, w_in, sc_conv_w[j], [qkv_out, b_out], seq_len=seq,
                               batch_of=lat_batch)
            if upd_ctx:
                qkv_c, b_ctx = in_ab(cs, m_i, g, w_in, sc_conv_w[j], [qkv_out, b_out], seq_len=ctx_len,
                                     batch_of=ctx_batch)
            else:
                (qkv_c,) = in_ab(cs, m_i, g, w_in, sc_conv_w[j], [qkv_out], seq_len=ctx_len,
                                 batch_of=ctx_batch)
            a_lat = _neighbourhood_attention(qkv, qkv_c, na_rpb[j], batch, seq, ctx_len)
            mix_lat = ([a_lat, b_lat], w_out)
            if upd_ctx:
                mix_ctx = ([_context_attention(qkv_c, batch, ctx_len), b_ctx], w_out)
        else:
            w_in = hy_w_in[j].astype(BF16)
            w_out = hy_w_out[j].astype(BF16)
            hw = w_out.shape[0]
            filt = lambda n: _hyena_filter(n, hy_f_w1[j], hy_f_b1[j], hy_f_w2[j], hy_f_b2[j], hy_f_w3[j],
                                           hy_f_b3[j], hy_f_w4[j], hy_sin_freq[j])
            in_hy = functools.partial(_inproj_call, _inproj_hyena_kernel, "inproj_hyena", mrow=3, grow=2)
            x0, u = in_hy(xs, m_i, g, w_in, hy_short_w[j], [(hw, F32), (hw, F32)], seq_len=seq,
                          batch_of=lat_batch)
            mix_lat = ([_long_conv_gate(u, x0, filt(seq), hy_bias[j], batch, seq)], w_out)
            if upd_ctx:
                x0_c, u_c = in_hy(cs, m_i, g, w_in, hy_short_w[j], [(hw, F32), (hw, F32)], seq_len=ctx_len,
                                  batch_of=ctx_batch)
                mix_ctx = ([_long_conv_gate_short(u_c, x0_c, filt(ctx_len), hy_bias[j], batch, ctx_len)], w_out)

        xs = _ffn(xs, m_i, g, *ffn_w, (i, 1), mrow=6, grow=4, batch_of=lat_batch, mixer=mix_lat,
                  mix_mrow=5, mix_grow=3)
        if upd_ctx:
            cs = _ffn(cs, m_i, g, *ffn_w, (i, 1), mrow=6, grow=4, batch_of=ctx_batch, mixer=mix_ctx,
                      mix_mrow=5, mix_grow=3)

    return xs.reshape(batch, seq, d)
```

```python
import functools
import math

import numpy as np
import jax
import jax.numpy as jnp
from jax import lax
from jax.experimental import pallas as pl
from jax.experimental.pallas import tpu as pltpu

F32 = jnp.float32
BF16 = jnp.bfloat16

GRID_W = 64
NA_HEADS = 8
NA_HEAD_DIM = 64
NA_WIDTH = NA_HEADS * NA_HEAD_DIM
NA_KH = 8
NA_KW = 16
FFN_RES = 0.5
N_MOD = 9
RMS_EPS = 1e-6
HY_EMB = 33
HY_BANDS = (HY_EMB - 1) // 2
HY_TARGET = 1e-2
HY_FAST_PCT = 0.3
HY_SLOW_PCT = 1.5

V7X_LANES = 128
V7X_SUBLANES = 8
V7X_MXU_DIM = 256
V7X_VMEM_BYTES = 64 * 1024 * 1024

ROW_TILE = 512
FFN_CHUNK = 256
FFN_SUBBLOCKS = 1
PROJ_CHUNK = 512
NA_ROWS_PER_STEP = 8
DFT_RADIX = 128
DFT_LANES = 256
DFT_K1_PER_STEP = 8
FILTER_ROWS = 512
MASK_VALUE = -1e30


def _vmem_limit(nbytes):
    return int(min(max(nbytes, 32 * 1024 * 1024), V7X_VMEM_BYTES - 8 * 1024 * 1024))


def _params(semantics, vmem_bytes):
    return pltpu.CompilerParams(dimension_semantics=semantics, vmem_limit_bytes=_vmem_limit(vmem_bytes))


def _resident(shape):
    zeros = (0,) * len(shape)
    return pl.BlockSpec(shape, lambda *_: zeros, pipeline_mode=pl.Buffered(1))


def _rms(x):
    return x * lax.rsqrt(jnp.mean(x * x, axis=-1, keepdims=True) + RMS_EPS)


def _mdot(a, b):
    return jnp.dot(a, b, preferred_element_type=F32)


def _mod_kernel(c_ref, w_ref, b_ref, o_ref):
    c = c_ref[...]
    s = (c * jax.nn.sigmoid(c)).astype(BF16)
    o_ref[...] = _mdot(s, w_ref[...].astype(BF16)) + b_ref[...]


def _modulation(cc, w_mod, b_mod):
    depth, d, nd = w_mod.shape
    return pl.pallas_call(
        _mod_kernel,
        grid=(depth, nd // d),
        in_specs=[pl.BlockSpec((V7X_SUBLANES, d), lambda i, j: (0, 0)),
                  pl.BlockSpec((None, d, d), lambda i, j: (i, 0, j)),
                  pl.BlockSpec((None, 1, d), lambda i, j: (i, 0, j))],
        out_specs=pl.BlockSpec((None, V7X_SUBLANES, d), lambda i, j: (i, 0, j)),
        out_shape=jax.ShapeDtypeStruct((depth, V7X_SUBLANES, nd), F32),
        name="modulation",
        compiler_params=_params(("parallel", "parallel"), 4 * d * d * 4),
    )(cc, w_mod, b_mod.reshape(depth, 1, nd))


def _ffn_kernel(*refs, n_parts, mix_mrow, mix_grow, mrow, grow):
    part_refs = refs[:n_parts]
    w_out_ref = refs[n_parts] if n_parts else None
    x_ref, m_ref, g_ref, wg_ref, wu_ref, wd_ref, o_ref, acc_ref = refs[n_parts + bool(n_parts):]
    shift, scale, gate = m_ref[mrow:mrow + 1, :], m_ref[mrow + 1:mrow + 2, :], m_ref[mrow + 2:mrow + 3, :]
    hm = x_ref.shape[0] // FFN_SUBBLOCKS
    subs = [slice(s * hm, (s + 1) * hm) for s in range(FFN_SUBBLOCKS)]
    xs, hs = [], []
    for rows in subs:
        x = x_ref[rows, :]
        if n_parts:
            y, r0 = None, 0
            for p_ref in part_refs:
                r1 = r0 + p_ref.shape[1]
                term = _mdot(p_ref[rows, :].astype(BF16), w_out_ref[r0:r1, :])
                y = term if y is None else y + term
                r0 = r1
            x = x + m_ref[mix_mrow:mix_mrow + 1, :] * (_rms(y) * g_ref[mix_grow:mix_grow + 1, :])
        xs.append(x)
        hs.append(((_rms(x) * g_ref[grow:grow + 1, :]) * (1.0 + scale) + shift).astype(BF16))
    for j in range(wg_ref.shape[1] // FFN_CHUNK):
        c0, c1 = j * FFN_CHUNK, (j + 1) * FFN_CHUNK
        for rows, h in zip(subs, hs):
            g = _mdot(h, wg_ref[:, c0:c1])
            u = _mdot(h, wu_ref[:, c0:c1])
            a = (g * jax.nn.sigmoid(g) * u).astype(BF16)
            part = _mdot(a, wd_ref[c0:c1, :])
            if j == 0:
                acc_ref[rows, :] = part
            else:
                acc_ref[rows, :] += part
    for rows, x in zip(subs, xs):
        y = acc_ref[rows, :]
        o_ref[rows, :] = x + FFN_RES * gate * (_rms(y) * g_ref[grow + 1:grow + 2, :])


def _layer_weight(w, index):
    lead = len(index)
    return pl.BlockSpec((None,) * lead + w.shape[lead:], lambda *_: tuple(index) + (0, 0),
                        pipeline_mode=pl.Buffered(1))


def _ffn(x, mod_i, g, wg, wu, wd, layer, *, mrow, grow, batch_of, mixer=None, mix_mrow=None, mix_grow=None):
    m, d = x.shape
    f = wg.shape[-1]
    tm = min(ROW_TILE, m)
    parts, w_out = mixer if mixer else ((), None)
    kern = functools.partial(_ffn_kernel, n_parts=len(parts), mix_mrow=mix_mrow, mix_grow=mix_grow,
                             mrow=mrow, grow=grow)
    part_bytes = sum(p.shape[1] * p.dtype.itemsize for p in parts)
    vmem = (3 * d * f * 2 + 4 * tm * d * 4 + tm * d * 4 + 6 * tm * FFN_CHUNK * 4 + tm * d * 12
            + 2 * tm * part_bytes + (d * d * 2 if parts else 0))
    mix_specs = [pl.BlockSpec((tm, p.shape[1]), lambda i: (i, 0)) for p in parts]
    if parts:
        mix_specs.append(_resident(w_out.shape))
    return pl.pallas_call(
        kern,
        grid=(m // tm,),
        in_specs=mix_specs + [
            pl.BlockSpec((tm, d), lambda i: (i, 0)),
            pl.BlockSpec((None, N_MOD, d), lambda i: (batch_of(i, tm), 0, 0)),
            _resident(g.shape), _layer_weight(wg, layer), _layer_weight(wu, layer), _layer_weight(wd, layer)],
        out_specs=pl.BlockSpec((tm, d), lambda i: (i, 0)),
        out_shape=jax.ShapeDtypeStruct((m, d), F32),
        scratch_shapes=[pltpu.VMEM((tm, d), F32)],
        name="ffn_mix" if parts else "ffn",
        compiler_params=_params(("parallel",), vmem),
    )(*parts, *([w_out] if parts else []), x, mod_i, g, wg, wu, wd)


def _conv3_rows(pe, w0, w1, w2, tm):
    n = pe.shape[0]
    lo = V7X_SUBLANES
    down = pltpu.roll(pe, 1, 0)[lo:lo + tm]
    up = pltpu.roll(pe, n - 1, 0)[lo:lo + tm]
    return down * w0 + pe[lo:lo + tm] * w1 + up * w2


def _normed_with_halo(x_ref, prev_ref, next_ref, m_ref, g_ref, mrow, grow, seq_len):
    tm = x_ref.shape[0]
    lo = V7X_SUBLANES
    shift, scale = m_ref[mrow:mrow + 1, :], m_ref[mrow + 1:mrow + 2, :]
    xe = jnp.concatenate([prev_ref[...], x_ref[...], next_ref[...]], axis=0)
    he = (_rms(xe) * g_ref[grow:grow + 1, :]) * (1.0 + scale) + shift
    start = (pl.program_id(0) * tm) & (seq_len - 1)
    keep_prev = (start != 0).astype(F32)
    keep_next = (start + tm != seq_len).astype(F32)
    row = lax.broadcasted_iota(jnp.int32, (tm + 2 * lo, 1), 0)
    keep = jnp.where(row < lo, keep_prev, jnp.where(row >= lo + tm, keep_next, 1.0))
    return he[lo:lo + tm].astype(BF16), (he * keep).astype(BF16)


def _inproj_ab_kernel(x_ref, prev_ref, next_ref, m_ref, g_ref, w_ref, cw_ref, qkv_ref, *b_refs,
                      mrow, grow, seq_len):
    tm = x_ref.shape[0]
    h, he = _normed_with_halo(x_ref, prev_ref, next_ref, m_ref, g_ref, mrow, grow, seq_len)
    nq = qkv_ref.shape[1]
    for c in range(0, nq, PROJ_CHUNK):
        qkv_ref[:, c:c + PROJ_CHUNK] = _mdot(h, w_ref[:, c:c + PROJ_CHUNK]).astype(qkv_ref.dtype)
    if b_refs:
        (b_ref,) = b_refs
        sw = cw_ref.shape[1]
        gate = _mdot(h, w_ref[:, nq:nq + sw])
        p = _mdot(he, w_ref[:, nq + sw:nq + 2 * sw]) * _mdot(he, w_ref[:, nq + 2 * sw:nq + 3 * sw])
        conv = _conv3_rows(p, cw_ref[0:1, :], cw_ref[1:2, :], cw_ref[2:3, :], tm)
        b_ref[...] = (gate * conv).astype(b_ref.dtype)


def _inproj_hyena_kernel(x_ref, prev_ref, next_ref, m_ref, g_ref, w_ref, cw_ref, x0_ref, u_ref,
                         *, mrow, grow, seq_len):
    tm = x_ref.shape[0]
    _, he = _normed_with_halo(x_ref, prev_ref, next_ref, m_ref, g_ref, mrow, grow, seq_len)
    c = x0_ref.shape[1]

    def conv_cols(c0):
        cols = slice(c0, c0 + PROJ_CHUNK)
        return _conv3_rows(_mdot(he, w_ref[:, cols]), cw_ref[0:1, cols], cw_ref[1:2, cols], cw_ref[2:3, cols], tm)

    for c0 in range(0, c, PROJ_CHUNK):
        x0_ref[:, c0:c0 + PROJ_CHUNK] = conv_cols(c0)
        u_ref[:, c0:c0 + PROJ_CHUNK] = conv_cols(2 * c + c0) * conv_cols(c + c0)


def _halo_specs(tm, m, width):
    blocks = m // V7X_SUBLANES
    per = tm // V7X_SUBLANES
    prev = pl.BlockSpec((V7X_SUBLANES, width), lambda i: (jnp.maximum(i * per - 1, 0), 0))
    nxt = pl.BlockSpec((V7X_SUBLANES, width), lambda i: (jnp.minimum((i + 1) * per, blocks - 1), 0))
    return prev, nxt


def _inproj_call(kern, name, x, mod_i, g, w, conv_w, outs, *, mrow, grow, seq_len, batch_of):
    m, d = x.shape
    tm = min(ROW_TILE, seq_len)
    assert seq_len & (seq_len - 1) == 0 and seq_len % tm == 0 and m % seq_len == 0
    prev, nxt = _halo_specs(tm, m, d)
    out_bytes = sum(cols * jnp.dtype(dt).itemsize for cols, dt in outs)
    vmem = w.size * 2 + 2 * tm * d * 4 + 2 * tm * out_bytes + 6 * tm * d * 4
    return pl.pallas_call(
        functools.partial(kern, mrow=mrow, grow=grow, seq_len=seq_len),
        grid=(m // tm,),
        in_specs=[pl.BlockSpec((tm, d), lambda i: (i, 0)), prev, nxt,
                  pl.BlockSpec((None, N_MOD, d), lambda i: (batch_of(i, tm), 0, 0)),
                  _resident(g.shape), _resident(w.shape), _resident(conv_w.shape)],
        out_specs=[pl.BlockSpec((tm, cols), lambda i: (i, 0)) for cols, _ in outs],
        out_shape=[jax.ShapeDtypeStruct((m, cols), dt) for cols, dt in outs],
        name=name,
        compiler_params=_params(("parallel",), vmem),
    )(x, x, x, mod_i, g, w, conv_w)


def _split_heads_on_rows(q, low_lanes):
    zero = jnp.zeros_like(q)
    return jnp.concatenate([jnp.where(low_lanes, q, zero), jnp.where(low_lanes, zero, q)], axis=0)


def _na_kernel(q_ref, k_ref, v_ref, kc_ref, vc_ref, bt_ref, o_ref, *, rows):
    rb = pl.program_id(2)
    low_lanes = lax.broadcasted_iota(jnp.int32, (GRID_W, V7X_LANES), 1) < NA_HEAD_DIM
    kc = kc_ref[...]
    vc = vc_ref[...]
    nt = (((1,), (1,)), ((), ()))

    for i in range(NA_ROWS_PER_STEP):
        r = rb * NA_ROWS_PER_STEP + i
        rs = jnp.clip(r - NA_KH // 2, 0, rows - NA_KH)
        q = q_ref[i * GRID_W:(i + 1) * GRID_W, :] * (NA_HEAD_DIM ** -0.5)
        q2 = _split_heads_on_rows(q, low_lanes)
        k0 = pl.multiple_of(rs * GRID_W, GRID_W)
        kk = k_ref[pl.ds(k0, NA_KH * GRID_W), :]
        vv = v_ref[pl.ds(k0, NA_KH * GRID_W), :]
        s = lax.dot_general(q2, kk, nt, preferred_element_type=F32) + bt_ref[rs - r + (NA_KH - 1)]
        sc = lax.dot_general(q2, kc, nt, preferred_element_type=F32)
        mx = jnp.maximum(jnp.max(s, axis=-1, keepdims=True), jnp.max(sc, axis=-1, keepdims=True))
        p = jnp.exp(s - mx)
        pc = jnp.exp(sc - mx)
        den = jnp.sum(p, axis=-1, keepdims=True) + jnp.sum(pc, axis=-1, keepdims=True)
        o2 = (_mdot(p.astype(BF16), vv) + _mdot(pc.astype(BF16), vc)) / den
        o = jnp.where(low_lanes, o2[:GRID_W], o2[GRID_W:])
        o_ref[i * GRID_W:(i + 1) * GRID_W, :] = o.astype(o_ref.dtype)


def _na_bias_table(rpb):
    h = rpb.shape[0]
    c = jnp.arange(GRID_W)[:, None]
    kc = jnp.arange(GRID_W)[None, :]
    start = jnp.clip(c - NA_KW // 2, 0, GRID_W - NA_KW)
    valid = (kc >= start) & (kc < start + NA_KW)
    idx = jnp.clip(kc - c + (NA_KW - 1), 0, 2 * NA_KW - 2)
    dense = jnp.where(valid[None, None], rpb[:, :, idx], MASK_VALUE)
    off = jnp.arange(NA_KH)[:, None] + jnp.arange(NA_KH)[None, :]
    tab = dense[:, off]
    tab = tab.transpose(0, 1, 3, 2, 4).reshape(h, NA_KH, GRID_W, NA_KH * GRID_W)
    tab = tab.reshape(h // 2, 2, NA_KH, GRID_W, NA_KH * GRID_W).transpose(0, 2, 1, 3, 4)
    return tab.reshape(h // 2, NA_KH, 2 * GRID_W, NA_KH * GRID_W)


def _neighbourhood_attention(qkv, qkv_c, rpb, batch, seq, ctx_len):
    rows = seq // GRID_W
    assert rows >= NA_KH and rows % NA_ROWS_PER_STEP == 0
    nq = NA_WIDTH // V7X_LANES
    qkv3 = qkv.reshape(batch, seq, 3 * NA_WIDTH)
    qkvc3 = qkv_c.reshape(batch, ctx_len, 3 * NA_WIDTH)
    tq = NA_ROWS_PER_STEP * GRID_W
    table = _na_bias_table(rpb)
    vmem = 4 * seq * V7X_LANES * 2 + 2 * table[0].size * 4 + 16 * 1024 * 1024
    out = pl.pallas_call(
        functools.partial(_na_kernel, rows=rows),
        grid=(batch, nq, rows // NA_ROWS_PER_STEP),
        in_specs=[pl.BlockSpec((None, tq, V7X_LANES), lambda b, h, r: (b, r, h)),
                  pl.BlockSpec((None, seq, V7X_LANES), lambda b, h, r: (b, 0, nq + h)),
                  pl.BlockSpec((None, seq, V7X_LANES), lambda b, h, r: (b, 0, 2 * nq + h)),
                  pl.BlockSpec((None, ctx_len, V7X_LANES), lambda b, h, r: (b, 0, nq + h)),
                  pl.BlockSpec((None, ctx_len, V7X_LANES), lambda b, h, r: (b, 0, 2 * nq + h)),
                  pl.BlockSpec((None,) + table.shape[1:], lambda b, h, r: (h, 0, 0, 0))],
        out_specs=pl.BlockSpec((None, tq, V7X_LANES), lambda b, h, r: (b, r, h)),
        out_shape=jax.ShapeDtypeStruct((batch, seq, NA_WIDTH), BF16),
        name="na_attn",
        compiler_params=_params(("parallel", "parallel", "arbitrary"), vmem),
    )(qkv3, qkv3, qkv3, qkvc3, qkvc3, table)
    return out.reshape(batch * seq, NA_WIDTH)


def _ctx_attn_kernel(q_ref, k_ref, v_ref, o_ref):
    n = q_ref.shape[0]
    low_lanes = lax.broadcasted_iota(jnp.int32, (n, V7X_LANES), 1) < NA_HEAD_DIM
    q2 = _split_heads_on_rows(q_ref[...] * (NA_HEAD_DIM ** -0.5), low_lanes)
    s = lax.dot_general(q2, k_ref[...], (((1,), (1,)), ((), ())), preferred_element_type=F32)
    p = jnp.exp(s - jnp.max(s, axis=-1, keepdims=True))
    o2 = _mdot(p.astype(BF16), v_ref[...]) / jnp.sum(p, axis=-1, keepdims=True)
    o_ref[...] = jnp.where(low_lanes, o2[:n], o2[n:]).astype(o_ref.dtype)


def _context_attention(qkv_c, batch, ctx_len):
    nq = NA_WIDTH // V7X_LANES
    qkvc3 = qkv_c.reshape(batch, ctx_len, 3 * NA_WIDTH)
    out = pl.pallas_call(
        _ctx_attn_kernel,
        grid=(batch, nq),
        in_specs=[pl.BlockSpec((None, ctx_len, V7X_LANES), lambda b, h: (b, 0, h)),
                  pl.BlockSpec((None, ctx_len, V7X_LANES), lambda b, h: (b, 0, nq + h)),
                  pl.BlockSpec((None, ctx_len, V7X_LANES), lambda b, h: (b, 0, 2 * nq + h))],
        out_specs=pl.BlockSpec((None, ctx_len, V7X_LANES), lambda b, h: (b, 0, h)),
        out_shape=jax.ShapeDtypeStruct((batch, ctx_len, NA_WIDTH), BF16),
        name="ctx_attn",
        compiler_params=_params(("parallel", "parallel"), 0),
    )(qkvc3, qkvc3, qkvc3)
    return out.reshape(batch * ctx_len, NA_WIDTH)


def _filter_kernel(z_ref, w1_ref, b1_ref, w2_ref, b2_ref, w3_ref, b3_ref, w4_ref, fr_ref, dec_ref, o_ref,
                   *, length):
    hi = lax.Precision.HIGHEST
    z = z_ref[...]
    fr = fr_ref[...]
    h = jnp.sin(fr * (jnp.dot(z, w1_ref[...], precision=hi, preferred_element_type=F32) + b1_ref[...]))
    h = jnp.sin(fr * (jnp.dot(h, w2_ref[...], precision=hi, preferred_element_type=F32) + b2_ref[...]))
    h = jnp.sin(fr * (jnp.dot(h, w3_ref[...], precision=hi, preferred_element_type=F32) + b3_ref[...]))
    filt = _mdot(h.astype(BF16), w4_ref[...].astype(BF16))
    out = filt * jnp.exp(-z[:, 0:1] * dec_ref[...])
    row = pl.program_id(0) * z.shape[0] + lax.broadcasted_iota(jnp.int32, out.shape, 0)
    o_ref[...] = jnp.where(row == length, 0.0, out)


@functools.lru_cache(maxsize=None)
def _filter_positions(length):
    t = np.linspace(0.0, 1.0, length)[:, None]
    w = 2.0 * math.pi * np.arange(length)[:, None] / length
    bands = np.linspace(1e-4, HY_BANDS - 1, HY_BANDS)[None, :]
    z = np.concatenate([t, np.cos(bands * w), -np.sin(bands * w)], axis=-1)
    z2 = np.concatenate([z, z[0:1], z[:0:-1]], axis=0)
    out = np.zeros((2 * length, V7X_LANES), np.float32)
    out[:, :HY_EMB] = z2
    return out


def _hyena_filter(length, w1, b1, w2, b2, w3, b3, w4, freq):
    c = w4.shape[1] // 2
    ffn = w2.shape[0]
    tr = min(FILTER_ROWS, length)
    per_half = length // tr
    zpos = jnp.asarray(_filter_positions(length))
    w1p = jnp.pad(w1, ((0, V7X_LANES - w1.shape[0]), (0, 0)))
    decay = np.abs(np.linspace(math.log(HY_TARGET) / HY_SLOW_PCT, math.log(HY_TARGET) / HY_FAST_PCT, c))
    decay = jnp.asarray(decay[None, :], F32)
    row = lambda v: v.reshape(1, -1)
    small = lambda shape: pl.BlockSpec(shape, lambda i: (0, 0))
    return pl.pallas_call(
        functools.partial(_filter_kernel, length=length),
        grid=(2 * per_half,),
        in_specs=[pl.BlockSpec((tr, V7X_LANES), lambda i: (i, 0)),
                  small((V7X_LANES, ffn)), small((1, ffn)), small((ffn, ffn)), small((1, ffn)),
                  small((ffn, ffn)), small((1, ffn)),
                  pl.BlockSpec((ffn, c), lambda i: (0, i // per_half)),
                  small((1, ffn)), small((1, c))],
        out_specs=pl.BlockSpec((tr, c), lambda i: (i, 0)),
        out_shape=jax.ShapeDtypeStruct((2 * length, c), F32),
        name="hyena_filter",
        compiler_params=_params(("parallel",), 0),
    )(zpos, w1p, row(b1), w2, row(b2), w3, row(b3), w4, row(freq), decay)


def _complex_as_real(m):
    return np.block([[m.real, -m.imag], [m.imag, m.real]])


@functools.lru_cache(maxsize=None)
def _dft_tables(length):
    n = 2 * length
    n1 = n // DFT_RADIX
    half = n1 // 2
    eye = np.eye(V7X_SUBLANES)
    k1 = np.arange(n1)
    f1 = np.exp(-2j * np.pi * np.outer(k1, k1) / n1)
    first_data = np.kron(_complex_as_real(f1[:, :half]), eye)
    first_filt = np.kron(np.concatenate([f1.real, f1.imag], axis=0), eye)
    inv1 = np.exp(2j * np.pi * np.outer(np.arange(half), k1) / n1) / n
    last = np.kron(_complex_as_real(inv1), eye)
    i2 = np.arange(DFT_RADIX)
    phase = (np.outer(i2, i2)[None] / DFT_RADIX + (i2[None, None, :] * k1[:, None, None]) / n)
    second = np.exp(-2j * np.pi * phase)
    fwd = np.stack([_complex_as_real(second[k]) for k in range(n1)])
    inv = np.stack([_complex_as_real(np.conj(second[k]).T) for k in range(n1)])
    return tuple(np.asarray(a, np.float32) for a in (first_data, first_filt, last, fwd, inv))


def _pack_pair(re, im):
    re_bits = lax.bitcast_convert_type(re.astype(BF16).astype(F32), jnp.uint32)
    im_bits = lax.bitcast_convert_type(im.astype(BF16).astype(F32), jnp.uint32)
    return (re_bits >> 16) | im_bits


def _unpack_pair(packed):
    re = lax.bitcast_convert_type(packed << 16, F32)
    im = lax.bitcast_convert_type(packed & jnp.uint32(0xFFFF0000), F32)
    return jnp.concatenate([re, im], axis=0).astype(BF16)


def _kron_kernel(mat_ref, x_ref, o_ref):
    lanes = x_ref.shape[-1]
    spec = _mdot(mat_ref[...], x_ref[...].reshape(-1, lanes).astype(BF16))
    rows = spec.shape[0] // 2
    o_ref[...] = _pack_pair(spec[:rows], spec[rows:]).reshape(o_ref.shape)


def _kron_out_kernel(mat_ref, b_ref, x0_ref, u_ref, bias_ref, o_ref):
    lanes = b_ref.shape[-1]
    y = _mdot(mat_ref[...], _unpack_pair(b_ref[...].reshape(-1, lanes))).reshape(o_ref.shape)
    o_ref[...] = x0_ref[...] * (y + u_ref[...] * bias_ref[...])


def _second_stage_kernel(a_ref, af_ref, g_ref, gi_ref, o_ref, kf_ref):
    r = DFT_RADIX
    steps = a_ref.shape[0]

    @pl.when(pl.program_id(2) == 0)
    def _():
        for j in range(steps):
            kf_ref[j] = _mdot(g_ref[j], _unpack_pair(af_ref[j]))

    for j in range(steps):
        spec = _mdot(g_ref[j], _unpack_pair(a_ref[j]))
        xr, xi = spec[:r], spec[r:]
        kr, ki = kf_ref[j, :r], kf_ref[j, r:]
        prod = jnp.concatenate([xr * kr - xi * ki, xr * ki + xi * kr], axis=0).astype(BF16)
        back = _mdot(gi_ref[j], prod)
        o_ref[j] = _pack_pair(back[:r], back[r:])


def _long_conv_gate(u, x0, kfilt, bias, batch, length):
    c = u.shape[1]
    pairs = batch // 2
    n1 = 2 * length // DFT_RADIX
    half = n1 // 2
    r, s8, cb, kb = DFT_RADIX, V7X_SUBLANES, min(DFT_LANES, c), DFT_K1_PER_STEP
    assert batch % 2 == 0 and n1 % kb == 0 and c % cb == 0
    first_data, first_filt, last, fwd, inv = (jnp.asarray(t, BF16) for t in _dft_tables(length))
    groups = r // s8
    big = 48 * 1024 * 1024

    u5 = u.reshape(pairs, 2, half, r, c)
    x05 = x0.reshape(pairs, 2, half, r, c)
    a_data = pl.pallas_call(
        _kron_kernel,
        grid=(pairs, groups, c // cb),
        in_specs=[_resident(first_data.shape),
                  pl.BlockSpec((None, 2, half, s8, cb), lambda p, g, l: (p, 0, 0, g, l))],
        out_specs=pl.BlockSpec((None, n1, s8, cb), lambda p, g, l: (p, 0, g, l)),
        out_shape=jax.ShapeDtypeStruct((pairs, n1, r, c), jnp.uint32),
        name="dft_first",
        compiler_params=_params(("parallel", "parallel", "parallel"), big),
    )(first_data, u5)

    a_filt = pl.pallas_call(
        _kron_kernel,
        grid=(groups, c // cb),
        in_specs=[_resident(first_filt.shape),
                  pl.BlockSpec((n1, s8, cb), lambda g, l: (0, g, l))],
        out_specs=pl.BlockSpec((n1, s8, cb), lambda g, l: (0, g, l)),
        out_shape=jax.ShapeDtypeStruct((n1, r, c), jnp.uint32),
        name="dft_first_filter",
        compiler_params=_params(("parallel", "parallel"), big),
    )(first_filt, kfilt.reshape(n1, r, c))

    b_data = pl.pallas_call(
        _second_stage_kernel,
        grid=(n1 // kb, c // cb, pairs),
        in_specs=[pl.BlockSpec((None, kb, r, cb), lambda k, l, p: (p, k, 0, l)),
                  pl.BlockSpec((kb, r, cb), lambda k, l, p: (k, 0, l)),
                  pl.BlockSpec((kb, 2 * r, 2 * r), lambda k, l, p: (k, 0, 0)),
                  pl.BlockSpec((kb, 2 * r, 2 * r), lambda k, l, p: (k, 0, 0))],
        out_specs=pl.BlockSpec((None, kb, r, cb), lambda k, l, p: (p, k, 0, l)),
        out_shape=jax.ShapeDtypeStruct((pairs, n1, r, c), jnp.uint32),
        scratch_shapes=[pltpu.VMEM((kb, 2 * r, cb), F32)],
        name="dft_second",
        compiler_params=_params(("arbitrary", "arbitrary", "arbitrary"), big),
    )(a_data, a_filt, fwd, inv)

    z = pl.pallas_call(
        _kron_out_kernel,
        grid=(pairs, groups, c // cb),
        in_specs=[_resident(last.shape),
                  pl.BlockSpec((None, n1, s8, cb), lambda p, g, l: (p, 0, g, l)),
                  pl.BlockSpec((None, 2, half, s8, cb), lambda p, g, l: (p, 0, 0, g, l)),
                  pl.BlockSpec((None, 2, half, s8, cb), lambda p, g, l: (p, 0, 0, g, l)),
                  pl.BlockSpec((1, cb), lambda p, g, l: (0, l))],
        out_specs=pl.BlockSpec((None, 2, half, s8, cb), lambda p, g, l: (p, 0, 0, g, l)),
        out_shape=jax.ShapeDtypeStruct((pairs, 2, half, r, c), F32),
        name="dft_last",
        compiler_params=_params(("parallel", "parallel", "parallel"), big),
    )(last, b_data, x05, u5, bias.reshape(1, c))
    return z.reshape(batch * length, c)


@functools.lru_cache(maxsize=None)
def _dense_dft_tables(length):
    n = 2 * length
    idx = np.arange(n)
    f = np.exp(-2j * np.pi * np.outer(idx, idx) / n)
    fwd_data = _complex_as_real(f[:, :length])
    fwd_filt = np.concatenate([f.real, f.imag], axis=0)
    inv = _complex_as_real(np.conj(f[:length, :]) / n)
    return tuple(np.asarray(a, np.float32) for a in (fwd_data, fwd_filt, inv))


def _short_conv_gate_kernel(fd_ref, ff_ref, inv_ref, u_ref, x0_ref, k_ref, bias_ref, o_ref):
    n = k_ref.shape[0]
    u = u_ref[...]
    spec = _mdot(fd_ref[...], u.astype(BF16))
    kf = _mdot(ff_ref[...], k_ref[...].astype(BF16))
    xr, xi, kr, ki = spec[:n], spec[n:], kf[:n], kf[n:]
    prod = jnp.concatenate([xr * kr - xi * ki, xr * ki + xi * kr], axis=0).astype(BF16)
    o_ref[...] = x0_ref[...] * (_mdot(inv_ref[...], prod) + u * bias_ref[...])


def _long_conv_gate_short(u, x0, kfilt, bias, batch, length):
    c = u.shape[1]
    pairs = batch // 2
    cb = min(DFT_LANES, c)
    fwd_data, fwd_filt, inv = (jnp.asarray(t, BF16) for t in _dense_dft_tables(length))
    pair_spec = pl.BlockSpec((None, 2 * length, cb), lambda p, l: (p, 0, l))
    z = pl.pallas_call(
        _short_conv_gate_kernel,
        grid=(pairs, c // cb),
        in_specs=[_resident(fwd_data.shape), _resident(fwd_filt.shape), _resident(inv.shape),
                  pair_spec, pair_spec,
                  pl.BlockSpec((2 * length, cb), lambda p, l: (0, l)),
                  pl.BlockSpec((1, cb), lambda p, l: (0, l))],
        out_specs=pair_spec,
        out_shape=jax.ShapeDtypeStruct((pairs, 2 * length, c), F32),
        name="short_conv_gate",
        compiler_params=_params(("parallel", "parallel"), 0),
    )(fwd_data, fwd_filt, inv, u.reshape(pairs, 2 * length, c), x0.reshape(pairs, 2 * length, c),
      kfilt, bias.reshape(1, c))
    return z.reshape(batch * length, c)


def kernel(x, c, ctx, c_ctx, w_mod, b_mod, norm_g, ffn_w_gate, ffn_w_up, ffn_w_down, ab_w_in, na_rpb,
           sc_conv_w, ab_w_out, hy_w_in, hy_short_w, hy_f_w1, hy_f_b1, hy_f_w2, hy_f_b2, hy_f_w3, hy_f_b3,
           hy_f_w4, hy_sin_freq, hy_bias, hy_w_out):
    batch, seq, d = x.shape
    ctx_len = ctx.shape[1]
    depth = w_mod.shape[0]
    assert batch + 1 <= V7X_SUBLANES and seq % ROW_TILE == 0
    last_attn = (depth - 1) - (depth - 1) % 2
    sc_width = sc_conv_w.shape[-1]

    cc = jnp.zeros((V7X_SUBLANES, d), F32).at[:batch].set(c).at[batch].set(c_ctx)
    mod = _modulation(cc, w_mod, b_mod).reshape(depth, V7X_SUBLANES, N_MOD, d)
    lat_batch = lambda i, tm: i // (seq // tm)
    ctx_batch = lambda i, tm: batch

    xs = x.reshape(batch * seq, d)
    cs = ctx.reshape(batch * ctx_len, d)
    ffn_w = (ffn_w_gate.astype(BF16), ffn_w_up.astype(BF16), ffn_w_down.astype(BF16))

    for i in range(depth):
        j = i // 2
        g = norm_g[i]
        m_i = mod[i]
        keep_ctx = i <= last_attn
        upd_ctx = i < last_attn

        xs = _ffn(xs, m_i, g, *ffn_w, (i, 0), mrow=0, grow=0, batch_of=lat_batch)
        if keep_ctx:
            cs = _ffn(cs, m_i, g, *ffn_w, (i, 0), mrow=0, grow=0, batch_of=ctx_batch)

        if i % 2 == 0:
            w_in = ab_w_in[j].astype(BF16)
            w_out = ab_w_out[j].astype(BF16)
            qkv_out, b_out = (3 * NA_WIDTH, BF16), (sc_width, BF16)
            in_ab = functools.partial(_inproj_call, _inproj_ab_kernel, "inproj_ab", mrow=3, grow=2)
            qkv, b_lat = in_ab(xs, m_i, g, w_in, sc_conv_w[j], [qkv_out, b_out], seq_len=seq,
                               batch_of=lat_batch)
            if upd_ctx:
                qkv_c, b_ctx = in_ab(cs, m_i, g, w_in, sc_conv_w[j], [qkv_out, b_out], seq_len=ctx_len,
                                     batch_of=ctx_batch)
            else:
                (qkv_c,) = in_ab(cs, m_i, g, w_in, sc_conv_w[j], [qkv_out], seq_len=ctx_len,
                                 batch_of=ctx_batch)
            a_lat = _neighbourhood_attention(qkv, qkv_c, na_rpb[j], batch, seq, ctx_len)
            mix_lat = ([a_lat, b_lat], w_out)
            if upd_ctx:
                mix_ctx = ([_context_attention(qkv_c, batch, ctx_len), b_ctx], w_out)
        else:
            w_in = hy_w_in[j].astype(BF16)
            w_out = hy_w_out[j].astype(BF16)
            hw = w_out.shape[0]
            filt = lambda n: _hyena_filter(n, hy_f_w1[j], hy_f_b1[j], hy_f_w2[j], hy_f_b2[j], hy_f_w3[j],
                                           hy_f_b3[j], hy_f_w4[j], hy_sin_freq[j])
            in_hy = functools.partial(_inproj_call, _inproj_hyena_kernel, "inproj_hyena", mrow=3, grow=2)
            x0, u = in_hy(xs, m_i, g, w_in, hy_short_w[j], [(hw, F32), (hw, F32)], seq_len=seq,
                          batch_of=lat_batch)
            mix_lat = ([_long_conv_gate(u, x0, filt(seq), hy_bias[j], batch, seq)], w_out)
            if upd_ctx:
                x0_c, u_c = in_hy(cs, m_i, g, w_in, hy_short_w[j], [(hw, F32), (hw, F32)], seq_len=ctx_len,
                                  batch_of=ctx_batch)
                mix_ctx = ([_long_conv_gate_short(u_c, x0_c, filt(ctx_len), hy_bias[j], batch, ctx_len)], w_out)

        xs = _ffn(xs, m_i, g, *ffn_w, (i, 1), mrow=6, grow=4, batch_of=lat_batch, mixer=mix_lat,
                  mix_mrow=5, mix_grow=3)
        if upd_ctx:
            cs = _ffn(cs, m_i, g, *ffn_w, (i, 1), mrow=6, grow=4, batch_of=ctx_batch, mixer=mix_ctx,
                      mix_mrow=5, mix_grow=3)

    return xs.reshape(batch, seq, d)
```

```python
import functools
import math

import numpy as np
import jax
import jax.numpy as jnp
from jax import lax
from jax.experimental import pallas as pl
from jax.experimental.pallas import tpu as pltpu

F32 = jnp.float32
BF16 = jnp.bfloat16

GRID_W = 64
NA_HEADS = 8
NA_HEAD_DIM = 64
NA_WIDTH = NA_HEADS * NA_HEAD_DIM
NA_KH = 8
NA_KW = 16
FFN_RES = 0.5
N_MOD = 9
RMS_EPS = 1e-6
HY_EMB = 33
HY_BANDS = (HY_EMB - 1) // 2
HY_TARGET = 1e-2
HY_FAST_PCT = 0.3
HY_SLOW_PCT = 1.5

V7X_LANES = 128
V7X_SUBLANES = 8
V7X_MXU_DIM = 256
V7X_VMEM_BYTES = 64 * 1024 * 1024

ROW_TILE = 512
FFN_CHUNK = 256
FFN_SUBBLOCKS = 1
PROJ_CHUNK = 512
NA_ROWS_PER_STEP = 8
NA_SLAB_ROWS = NA_KH + 2
DFT_RADIX = 128
DFT_LANES = 256
DFT_K1_PER_STEP = 8
FILTER_ROWS = 512
MASK_VALUE = -1e30


def _vmem_limit(nbytes):
    return int(min(max(nbytes, 32 * 1024 * 1024), V7X_VMEM_BYTES - 8 * 1024 * 1024))


def _params(semantics, vmem_bytes):
    return pltpu.CompilerParams(dimension_semantics=semantics, vmem_limit_bytes=_vmem_limit(vmem_bytes))


def _resident(shape):
    zeros = (0,) * len(shape)
    return pl.BlockSpec(shape, lambda *_: zeros, pipeline_mode=pl.Buffered(1))


def _rms(x):
    return x * lax.rsqrt(jnp.mean(x * x, axis=-1, keepdims=True) + RMS_EPS)


def _mdot(a, b):
    return jnp.dot(a, b, preferred_element_type=F32)


def _mod_kernel(c_ref, w_ref, b_ref, o_ref):
    c = c_ref[...]
    s = (c * jax.nn.sigmoid(c)).astype(BF16)
    o_ref[...] = _mdot(s, w_ref[...].astype(BF16)) + b_ref[...]


def _modulation(cc, w_mod, b_mod):
    depth, d, nd = w_mod.shape
    return pl.pallas_call(
        _mod_kernel,
        grid=(depth, nd // d),
        in_specs=[pl.BlockSpec((V7X_SUBLANES, d), lambda i, j: (0, 0)),
                  pl.BlockSpec((None, d, d), lambda i, j: (i, 0, j)),
                  pl.BlockSpec((None, 1, d), lambda i, j: (i, 0, j))],
        out_specs=pl.BlockSpec((None, V7X_SUBLANES, d), lambda i, j: (i, 0, j)),
        out_shape=jax.ShapeDtypeStruct((depth, V7X_SUBLANES, nd), F32),
        name="modulation",
        compiler_params=_params(("parallel", "parallel"), 4 * d * d * 4),
    )(cc, w_mod, b_mod.reshape(depth, 1, nd))


def _ffn_kernel(*refs, n_parts, mix_mrow, mix_grow, mrow, grow):
    part_refs = refs[:n_parts]
    w_out_ref = refs[n_parts] if n_parts else None
    x_ref, m_ref, g_ref, wg_ref, wu_ref, wd_ref, o_ref, acc_ref = refs[n_parts + bool(n_parts):]
    shift, scale, gate = m_ref[mrow:mrow + 1, :], m_ref[mrow + 1:mrow + 2, :], m_ref[mrow + 2:mrow + 3, :]
    pre_gain = g_ref[grow:grow + 1, :] * (1.0 + scale)
    post_gain = (FFN_RES * gate) * g_ref[grow + 1:grow + 2, :]
    hm = x_ref.shape[0] // FFN_SUBBLOCKS
    subs = [slice(s * hm, (s + 1) * hm) for s in range(FFN_SUBBLOCKS)]
    xs, hs = [], []
    for rows in subs:
        x = x_ref[rows, :]
        if n_parts:
            y, r0 = None, 0
            for p_ref in part_refs:
                r1 = r0 + p_ref.shape[1]
                term = _mdot(p_ref[rows, :].astype(BF16), w_out_ref[r0:r1, :])
                y = term if y is None else y + term
                r0 = r1
            x = x + m_ref[mix_mrow:mix_mrow + 1, :] * (_rms(y) * g_ref[mix_grow:mix_grow + 1, :])
        xs.append(x)
        hs.append((_rms(x) * pre_gain + shift).astype(BF16))
    for j in range(wg_ref.shape[1] // FFN_CHUNK):
        c0, c1 = j * FFN_CHUNK, (j + 1) * FFN_CHUNK
        for rows, h in zip(subs, hs):
            g = _mdot(h, wg_ref[:, c0:c1])
            u = _mdot(h, wu_ref[:, c0:c1])
            a = (g * jax.nn.sigmoid(g) * u).astype(BF16)
            part = _mdot(a, wd_ref[c0:c1, :])
            if j == 0:
                acc_ref[rows, :] = part
            else:
                acc_ref[rows, :] += part
    for rows, x in zip(subs, xs):
        o_ref[rows, :] = x + _rms(acc_ref[rows, :]) * post_gain


def _layer_weight(w, index):
    lead = len(index)
    return pl.BlockSpec((None,) * lead + w.shape[lead:], lambda *_: tuple(index) + (0, 0),
                        pipeline_mode=pl.Buffered(1))


def _ffn(x, mod_i, g, wg, wu, wd, layer, *, mrow, grow, batch_of, mixer=None, mix_mrow=None, mix_grow=None):
    m, d = x.shape
    f = wg.shape[-1]
    tm = min(ROW_TILE, m)
    parts, w_out = mixer if mixer else ((), None)
    kern = functools.partial(_ffn_kernel, n_parts=len(parts), mix_mrow=mix_mrow, mix_grow=mix_grow,
                             mrow=mrow, grow=grow)
    part_bytes = sum(p.shape[1] * p.dtype.itemsize for p in parts)
    vmem = (3 * d * f * 2 + 4 * tm * d * 4 + tm * d * 4 + 6 * tm * FFN_CHUNK * 4 + tm * d * 12
            + 2 * tm * part_bytes + (d * d * 2 if parts else 0))
    mix_specs = [pl.BlockSpec((tm, p.shape[1]), lambda i: (i, 0)) for p in parts]
    if parts:
        mix_specs.append(_resident(w_out.shape))
    return pl.pallas_call(
        kern,
        grid=(m // tm,),
        in_specs=mix_specs + [
            pl.BlockSpec((tm, d), lambda i: (i, 0)),
            pl.BlockSpec((None, N_MOD, d), lambda i: (batch_of(i, tm), 0, 0)),
            _resident(g.shape), _layer_weight(wg, layer), _layer_weight(wu, layer), _layer_weight(wd, layer)],
        out_specs=pl.BlockSpec((tm, d), lambda i: (i, 0)),
        out_shape=jax.ShapeDtypeStruct((m, d), F32),
        scratch_shapes=[pltpu.VMEM((tm, d), F32)],
        name="ffn_mix" if parts else "ffn",
        compiler_params=_params(("parallel",), vmem),
    )(*parts, *([w_out] if parts else []), x, mod_i, g, wg, wu, wd)


def _conv3_rows(pe, w0, w1, w2, tm):
    n = pe.shape[0]
    lo = V7X_SUBLANES
    down = pltpu.roll(pe, 1, 0)[lo:lo + tm]
    up = pltpu.roll(pe, n - 1, 0)[lo:lo + tm]
    return down * w0 + pe[lo:lo + tm] * w1 + up * w2


def _normed_with_halo(x_ref, prev_ref, next_ref, m_ref, g_ref, mrow, grow, seq_len):
    tm = x_ref.shape[0]
    lo = V7X_SUBLANES
    shift, scale = m_ref[mrow:mrow + 1, :], m_ref[mrow + 1:mrow + 2, :]
    xe = jnp.concatenate([prev_ref[...], x_ref[...], next_ref[...]], axis=0)
    he = (_rms(xe) * g_ref[grow:grow + 1, :]) * (1.0 + scale) + shift
    start = (pl.program_id(0) * tm) & (seq_len - 1)
    keep_prev = (start != 0).astype(F32)
    keep_next = (start + tm != seq_len).astype(F32)
    row = lax.broadcasted_iota(jnp.int32, (tm + 2 * lo, 1), 0)
    keep = jnp.where(row < lo, keep_prev, jnp.where(row >= lo + tm, keep_next, 1.0))
    return he[lo:lo + tm].astype(BF16), (he * keep).astype(BF16)


def _inproj_ab_kernel(x_ref, prev_ref, next_ref, m_ref, g_ref, w_ref, cw_ref, wvt_ref, qk_ref, vt_ref, *b_refs,
                      mrow, grow, seq_len):
    tm = x_ref.shape[0]
    h, he = _normed_with_halo(x_ref, prev_ref, next_ref, m_ref, g_ref, mrow, grow, seq_len)
    nqk = qk_ref.shape[1]
    nq = nqk + vt_ref.shape[0]
    for c in range(0, nqk, PROJ_CHUNK):
        qk_ref[:, c:c + PROJ_CHUNK] = _mdot(h, w_ref[:, c:c + PROJ_CHUNK]).astype(qk_ref.dtype)
    vt_ref[...] = lax.dot_general(wvt_ref[...], h, (((1,), (1,)), ((), ())),
                                  preferred_element_type=F32).astype(vt_ref.dtype)
    if b_refs:
        (b_ref,) = b_refs
        sw = cw_ref.shape[1]
        gate = _mdot(h, w_ref[:, nq:nq + sw])
        p = _mdot(he, w_ref[:, nq + sw:nq + 2 * sw]) * _mdot(he, w_ref[:, nq + 2 * sw:nq + 3 * sw])
        conv = _conv3_rows(p, cw_ref[0:1, :], cw_ref[1:2, :], cw_ref[2:3, :], tm)
        b_ref[...] = (gate * conv).astype(b_ref.dtype)


def _inproj_hyena_kernel(x_ref, prev_ref, next_ref, m_ref, g_ref, w_ref, cw_ref, x0_ref, u_ref,
                         *, mrow, grow, seq_len):
    tm = x_ref.shape[0]
    _, he = _normed_with_halo(x_ref, prev_ref, next_ref, m_ref, g_ref, mrow, grow, seq_len)
    c = x0_ref.shape[1]

    def conv_cols(c0):
        cols = slice(c0, c0 + PROJ_CHUNK)
        return _conv3_rows(_mdot(he, w_ref[:, cols]), cw_ref[0:1, cols], cw_ref[1:2, cols], cw_ref[2:3, cols], tm)

    for c0 in range(0, c, PROJ_CHUNK):
        x0_ref[:, c0:c0 + PROJ_CHUNK] = conv_cols(c0)
        u_ref[:, c0:c0 + PROJ_CHUNK] = conv_cols(2 * c + c0) * conv_cols(c + c0)


def _halo_specs(tm, m, width):
    blocks = m // V7X_SUBLANES
    per = tm // V7X_SUBLANES
    prev = pl.BlockSpec((V7X_SUBLANES, width), lambda i: (jnp.maximum(i * per - 1, 0), 0))
    nxt = pl.BlockSpec((V7X_SUBLANES, width), lambda i: (jnp.minimum((i + 1) * per, blocks - 1), 0))
    return prev, nxt


def _inproj_call(kern, name, x, mod_i, g, w, conv_w, outs, *, mrow, grow, seq_len, batch_of, extra=()):
    m, d = x.shape
    tm = min(ROW_TILE, seq_len)
    assert seq_len & (seq_len - 1) == 0 and seq_len % tm == 0 and m % seq_len == 0
    prev, nxt = _halo_specs(tm, m, d)
    out_bytes = sum(cols * jnp.dtype(dt).itemsize for cols, dt, _ in outs)
    vmem = w.size * 2 + 2 * tm * d * 4 + 2 * tm * out_bytes + 6 * tm * d * 4
    out_specs = [pl.BlockSpec((cols, tm), lambda i: (0, i)) if tr else pl.BlockSpec((tm, cols), lambda i: (i, 0))
                 for cols, _, tr in outs]
    out_shape = [jax.ShapeDtypeStruct((cols, m) if tr else (m, cols), dt) for cols, dt, tr in outs]
    return pl.pallas_call(
        functools.partial(kern, mrow=mrow, grow=grow, seq_len=seq_len),
        grid=(m // tm,),
        in_specs=[pl.BlockSpec((tm, d), lambda i: (i, 0)), prev, nxt,
                  pl.BlockSpec((None, N_MOD, d), lambda i: (batch_of(i, tm), 0, 0)),
                  _resident(g.shape), _resident(w.shape), _resident(conv_w.shape)]
        + [_resident(e.shape) for e in extra],
        out_specs=out_specs,
        out_shape=out_shape,
        name=name,
        compiler_params=_params(("parallel",), vmem),
    )(x, x, x, mod_i, g, w, conv_w, *extra)


def _split_heads_on_rows(q, low_lanes):
    zero = jnp.zeros_like(q)
    return jnp.concatenate([jnp.where(low_lanes, q, zero), jnp.where(low_lanes, zero, q)], axis=0)


def _na_kernel(pat_ref, q_ref, k_ref, vt_ref, kc_ref, vct_ref, bt_ref, o_ref, *, rows):
    rb = pl.program_id(2)
    low_lanes = lax.broadcasted_iota(jnp.int32, (GRID_W, V7X_LANES), 1) < NA_HEAD_DIM
    kc = kc_ref[...]
    vct = vct_ref[...]
    nt = (((1,), (1,)), ((), ()))

    for t in range(NA_ROWS_PER_STEP // 2):
        r = rb * NA_ROWS_PER_STEP + 2 * t
        base = jnp.clip(r - NA_KH // 2, 0, rows - NA_SLAB_ROWS)
        k0 = pl.multiple_of(base * GRID_W, 2 * GRID_W)
        qa = q_ref[2 * t * GRID_W:(2 * t + 1) * GRID_W, :] * (NA_HEAD_DIM ** -0.5)
        qb = q_ref[(2 * t + 1) * GRID_W:(2 * t + 2) * GRID_W, :] * (NA_HEAD_DIM ** -0.5)
        q4 = jnp.concatenate([_split_heads_on_rows(qa, low_lanes), _split_heads_on_rows(qb, low_lanes)], axis=0)
        kk = k_ref[pl.ds(k0, NA_SLAB_ROWS * GRID_W), :]
        s = lax.dot_general(kk, q4, nt, preferred_element_type=F32) + bt_ref[pat_ref[r // 2]]
        sc = lax.dot_general(kc, q4, nt, preferred_element_type=F32)
        mx = jnp.maximum(jnp.max(s, axis=0, keepdims=True), jnp.max(sc, axis=0, keepdims=True))
        p = jnp.exp(s - mx)
        pc = jnp.exp(sc - mx)
        den = jnp.sum(p, axis=0, keepdims=True) + jnp.sum(pc, axis=0, keepdims=True)
        vv = vt_ref[:, pl.ds(k0, NA_SLAB_ROWS * GRID_W)]
        ot = (_mdot(vv, p.astype(BF16)) + _mdot(vct, pc.astype(BF16))) / den
        o4 = ot.T
        oa = jnp.where(low_lanes, o4[0:GRID_W], o4[GRID_W:2 * GRID_W])
        ob = jnp.where(low_lanes, o4[2 * GRID_W:3 * GRID_W], o4[3 * GRID_W:4 * GRID_W])
        o_ref[2 * t * GRID_W:(2 * t + 2) * GRID_W, :] = jnp.concatenate([oa, ob], axis=0).astype(o_ref.dtype)


def _na_window_row(r, rows):
    return min(max(r - NA_KH // 2, 0), rows - NA_KH)


def _na_bias_table(rpb, rows):
    h = rpb.shape[0]
    c = np.arange(GRID_W)[:, None]
    kc = np.arange(GRID_W)[None, :]
    start = np.clip(c - NA_KW // 2, 0, GRID_W - NA_KW)
    valid = (kc >= start) & (kc < start + NA_KW)
    onehot = ((kc - c + (NA_KW - 1))[:, :, None] == np.arange(2 * NA_KW - 1)) & valid[:, :, None]
    dense = jnp.einsum("hdx,ckx->hdkc", rpb, jnp.asarray(onehot, F32), precision=lax.Precision.HIGHEST)
    dense = dense + jnp.asarray(np.where(valid.T, 0.0, MASK_VALUE), F32)
    masked = jnp.full((h, GRID_W, GRID_W), MASK_VALUE, F32)
    patterns, ids = [], []
    for r in range(0, rows, 2):
        base = min(max(r - NA_KH // 2, 0), rows - NA_SLAB_ROWS)
        key = tuple((_na_window_row(r + s, rows) - base, _na_window_row(r + s, rows) - (r + s) + NA_KH - 1)
                    for s in (0, 1))
        if key not in patterns:
            patterns.append(key)
        ids.append(patterns.index(key))
    tabs = []
    for key in patterns:
        per_row = []
        for off, dy0 in key:
            blocks = [dense[:, dy0 + y - off] if off <= y < off + NA_KH else masked for y in range(NA_SLAB_ROWS)]
            per_row.append(jnp.concatenate(blocks, axis=1))
        t = jnp.stack(per_row, axis=1).reshape(h // 2, 2, 2, NA_SLAB_ROWS * GRID_W, GRID_W)
        tabs.append(t.transpose(0, 3, 2, 1, 4).reshape(h // 2, NA_SLAB_ROWS * GRID_W, 4 * GRID_W))
    return jnp.stack(tabs, axis=1), np.asarray(ids, np.int32)


def _neighbourhood_attention(qk, vt, qk_c, vt_c, rpb, batch, seq, ctx_len):
    rows = seq // GRID_W
    assert rows >= NA_SLAB_ROWS and rows % NA_ROWS_PER_STEP == 0 and NA_ROWS_PER_STEP % 2 == 0
    nq = NA_WIDTH // V7X_LANES
    qk3 = qk.reshape(batch, seq, 2 * NA_WIDTH)
    qkc3 = qk_c.reshape(batch, ctx_len, 2 * NA_WIDTH)
    tq = NA_ROWS_PER_STEP * GRID_W
    table, pattern_ids = _na_bias_table(rpb, rows)
    vmem = 4 * seq * V7X_LANES * 2 + 2 * table[0].size * 4 + 16 * 1024 * 1024
    out = pl.pallas_call(
        functools.partial(_na_kernel, rows=rows),
        grid=(batch, nq, rows // NA_ROWS_PER_STEP),
        in_specs=[pl.BlockSpec(memory_space=pltpu.SMEM),
                  pl.BlockSpec((None, tq, V7X_LANES), lambda b, h, r: (b, r, h)),
                  pl.BlockSpec((None, seq, V7X_LANES), lambda b, h, r: (b, 0, nq + h)),
                  pl.BlockSpec((V7X_LANES, seq), lambda b, h, r: (h, b)),
                  pl.BlockSpec((None, ctx_len, V7X_LANES), lambda b, h, r: (b, 0, nq + h)),
                  pl.BlockSpec((V7X_LANES, ctx_len), lambda b, h, r: (h, b)),
                  pl.BlockSpec((None,) + table.shape[1:], lambda b, h, r: (h, 0, 0, 0))],
        out_specs=pl.BlockSpec((None, tq, V7X_LANES), lambda b, h, r: (b, r, h)),
        out_shape=jax.ShapeDtypeStruct((batch, seq, NA_WIDTH), BF16),
        name="na_attn",
        compiler_params=_params(("parallel", "parallel", "arbitrary"), vmem),
    )(jnp.asarray(pattern_ids), qk3, qk3, vt, qkc3, vt_c, table)
    return out.reshape(batch * seq, NA_WIDTH)


def _ctx_attn_kernel(q_ref, k_ref, vt_ref, o_ref):
    n = q_ref.shape[0]
    nt = (((1,), (1,)), ((), ()))
    low_lanes = lax.broadcasted_iota(jnp.int32, (n, V7X_LANES), 1) < NA_HEAD_DIM
    q2 = _split_heads_on_rows(q_ref[...] * (NA_HEAD_DIM ** -0.5), low_lanes)
    s = lax.dot_general(q2, k_ref[...], nt, preferred_element_type=F32)
    p = jnp.exp(s - jnp.max(s, axis=-1, keepdims=True))
    o2 = lax.dot_general(p.astype(BF16), vt_ref[...], nt, preferred_element_type=F32)
    o2 = o2 / jnp.sum(p, axis=-1, keepdims=True)
    o_ref[...] = jnp.where(low_lanes, o2[:n], o2[n:]).astype(o_ref.dtype)


def _context_attention(qk_c, vt_c, batch, ctx_len):
    nq = NA_WIDTH // V7X_LANES
    qkc3 = qk_c.reshape(batch, ctx_len, 2 * NA_WIDTH)
    out = pl.pallas_call(
        _ctx_attn_kernel,
        grid=(batch, nq),
        in_specs=[pl.BlockSpec((None, ctx_len, V7X_LANES), lambda b, h: (b, 0, h)),
                  pl.BlockSpec((None, ctx_len, V7X_LANES), lambda b, h: (b, 0, nq + h)),
                  pl.BlockSpec((V7X_LANES, ctx_len), lambda b, h: (h, b))],
        out_specs=pl.BlockSpec((None, ctx_len, V7X_LANES), lambda b, h: (b, 0, h)),
        out_shape=jax.ShapeDtypeStruct((batch, ctx_len, NA_WIDTH), BF16),
        name="ctx_attn",
        compiler_params=_params(("parallel", "parallel"), 0),
    )(qkc3, qkc3, vt_c)
    return out.reshape(batch * ctx_len, NA_WIDTH)


def _filter_kernel(z_ref, w1_ref, b1_ref, w2_ref, b2_ref, w3_ref, b3_ref, w4_ref, fr_ref, dec_ref, o_ref,
                   *, length):
    hi = lax.Precision.HIGHEST
    z = z_ref[...]
    fr = fr_ref[...]
    h = jnp.sin(fr * (jnp.dot(z, w1_ref[...], precision=hi, preferred_element_type=F32) + b1_ref[...]))
    h = jnp.sin(fr * (jnp.dot(h, w2_ref[...], precision=hi, preferred_element_type=F32) + b2_ref[...]))
    h = jnp.sin(fr * (jnp.dot(h, w3_ref[...], precision=hi, preferred_element_type=F32) + b3_ref[...]))
    filt = _mdot(h.astype(BF16), w4_ref[...].astype(BF16))
    out = filt * jnp.exp(-z[:, 0:1] * dec_ref[...])
    row = pl.program_id(0) * z.shape[0] + lax.broadcasted_iota(jnp.int32, out.shape, 0)
    o_ref[...] = jnp.where(row == length, 0.0, out)


@functools.lru_cache(maxsize=None)
def _filter_positions(length):
    t = np.linspace(0.0, 1.0, length)[:, None]
    w = 2.0 * math.pi * np.arange(length)[:, None] / length
    bands = np.linspace(1e-4, HY_BANDS - 1, HY_BANDS)[None, :]
    z = np.concatenate([t, np.cos(bands * w), -np.sin(bands * w)], axis=-1)
    z2 = np.concatenate([z, z[0:1], z[:0:-1]], axis=0)
    out = np.zeros((2 * length, V7X_LANES), np.float32)
    out[:, :HY_EMB] = z2
    return out


def _hyena_filter(length, w1, b1, w2, b2, w3, b3, w4, freq):
    c = w4.shape[1] // 2
    ffn = w2.shape[0]
    tr = min(FILTER_ROWS, length)
    per_half = length // tr
    zpos = jnp.asarray(_filter_positions(length))
    w1p = jnp.pad(w1, ((0, V7X_LANES - w1.shape[0]), (0, 0)))
    decay = np.abs(np.linspace(math.log(HY_TARGET) / HY_SLOW_PCT, math.log(HY_TARGET) / HY_FAST_PCT, c))
    decay = jnp.asarray(decay[None, :], F32)
    row = lambda v: v.reshape(1, -1)
    small = lambda shape: pl.BlockSpec(shape, lambda i: (0, 0))
    return pl.pallas_call(
        functools.partial(_filter_kernel, length=length),
        grid=(2 * per_half,),
        in_specs=[pl.BlockSpec((tr, V7X_LANES), lambda i: (i, 0)),
                  small((V7X_LANES, ffn)), small((1, ffn)), small((ffn, ffn)), small((1, ffn)),
                  small((ffn, ffn)), small((1, ffn)),
                  pl.BlockSpec((ffn, c), lambda i: (0, i // per_half)),
                  small((1, ffn)), small((1, c))],
        out_specs=pl.BlockSpec((tr, c), lambda i: (i, 0)),
        out_shape=jax.ShapeDtypeStruct((2 * length, c), F32),
        name="hyena_filter",
        compiler_params=_params(("parallel",), 0),
    )(zpos, w1p, row(b1), w2, row(b2), w3, row(b3), w4, row(freq), decay)


def _complex_as_real(m):
    return np.block([[m.real, -m.imag], [m.imag, m.real]])


@functools.lru_cache(maxsize=None)
def _dft_tables(length):
    n = 2 * length
    n1 = n // DFT_RADIX
    half = n1 // 2
    eye = np.eye(V7X_SUBLANES)
    k1 = np.arange(n1)
    f1 = np.exp(-2j * np.pi * np.outer(k1, k1) / n1)
    first_data = np.kron(_complex_as_real(f1[:, :half]), eye)
    first_filt = np.kron(np.concatenate([f1.real, f1.imag], axis=0), eye)
    inv1 = np.exp(2j * np.pi * np.outer(np.arange(half), k1) / n1) / n
    last = np.kron(_complex_as_real(inv1), eye)
    i2 = np.arange(DFT_RADIX)
    phase = (np.outer(i2, i2)[None] / DFT_RADIX + (i2[None, None, :] * k1[:, None, None]) / n)
    second = np.exp(-2j * np.pi * phase)
    fwd = np.stack([_complex_as_real(second[k]) for k in range(n1)])
    inv = np.stack([_complex_as_real(np.conj(second[k]).T) for k in range(n1)])
    return tuple(np.asarray(a, np.float32) for a in (first_data, first_filt, last, fwd, inv))


def _pack_pair(re, im):
    re_bits = lax.bitcast_convert_type(re.astype(BF16).astype(F32), jnp.uint32)
    im_bits = lax.bitcast_convert_type(im.astype(BF16).astype(F32), jnp.uint32)
    return (re_bits >> 16) | im_bits


def _unpack_pair(packed):
    re = lax.bitcast_convert_type(packed << 16, F32)
    im = lax.bitcast_convert_type(packed & jnp.uint32(0xFFFF0000), F32)
    return jnp.concatenate([re, im], axis=0).astype(BF16)


def _kron_kernel(mat_ref, x_ref, o_ref):
    lanes = x_ref.shape[-1]
    spec = _mdot(mat_ref[...], x_ref[...].reshape(-1, lanes).astype(BF16))
    rows = spec.shape[0] // 2
    o_ref[...] = _pack_pair(spec[:rows], spec[rows:]).reshape(o_ref.shape)


def _kron_out_kernel(mat_ref, b_ref, x0_ref, u_ref, bias_ref, o_ref):
    lanes = b_ref.shape[-1]
    y = _mdot(mat_ref[...], _unpack_pair(b_ref[...].reshape(-1, lanes))).reshape(o_ref.shape)
    o_ref[...] = x0_ref[...] * (y + u_ref[...] * bias_ref[...])


def _second_stage_kernel(a_ref, af_ref, g_ref, gi_ref, o_ref, kf_ref):
    r = DFT_RADIX
    steps = a_ref.shape[0]

    @pl.when(pl.program_id(2) == 0)
    def _():
        for j in range(steps):
            kf_ref[j] = _mdot(g_ref[j], _unpack_pair(af_ref[j]))

    for j in range(steps):
        spec = _mdot(g_ref[j], _unpack_pair(a_ref[j]))
        xr, xi = spec[:r], spec[r:]
        kr, ki = kf_ref[j, :r], kf_ref[j, r:]
        prod = jnp.concatenate([xr * kr - xi * ki, xr * ki + xi * kr], axis=0).astype(BF16)
        back = _mdot(gi_ref[j], prod)
        o_ref[j] = _pack_pair(back[:r], back[r:])


def _long_conv_gate(u, x0, kfilt, bias, batch, length):
    c = u.shape[1]
    pairs = batch // 2
    n1 = 2 * length // DFT_RADIX
    half = n1 // 2
    r, s8, cb, kb = DFT_RADIX, V7X_SUBLANES, min(DFT_LANES, c), DFT_K1_PER_STEP
    assert batch % 2 == 0 and n1 % kb == 0 and c % cb == 0
    first_data, first_filt, last, fwd, inv = (jnp.asarray(t, BF16) for t in _dft_tables(length))
    groups = r // s8
    big = 48 * 1024 * 1024

    u5 = u.reshape(pairs, 2, half, r, c)
    x05 = x0.reshape(pairs, 2, half, r, c)
    a_data = pl.pallas_call(
        _kron_kernel,
        grid=(pairs, groups, c // cb),
        in_specs=[_resident(first_data.shape),
                  pl.BlockSpec((None, 2, half, s8, cb), lambda p, g, l: (p, 0, 0, g, l))],
        out_specs=pl.BlockSpec((None, n1, s8, cb), lambda p, g, l: (p, 0, g, l)),
        out_shape=jax.ShapeDtypeStruct((pairs, n1, r, c), jnp.uint32),
        name="dft_first",
        compiler_params=_params(("parallel", "parallel", "parallel"), big),
    )(first_data, u5)

    a_filt = pl.pallas_call(
        _kron_kernel,
        grid=(groups, c // cb),
        in_specs=[_resident(first_filt.shape),
                  pl.BlockSpec((n1, s8, cb), lambda g, l: (0, g, l))],
        out_specs=pl.BlockSpec((n1, s8, cb), lambda g, l: (0, g, l)),
        out_shape=jax.ShapeDtypeStruct((n1, r, c), jnp.uint32),
        name="dft_first_filter",
        compiler_params=_params(("parallel", "parallel"), big),
    )(first_filt, kfilt.reshape(n1, r, c))

    b_data = pl.pallas_call(
        _second_stage_kernel,
        grid=(n1 // kb, c // cb, pairs),
        in_specs=[pl.BlockSpec((None, kb, r, cb), lambda k, l, p: (p, k, 0, l)),
                  pl.BlockSpec((kb, r, cb), lambda k, l, p: (k, 0, l)),
                  pl.BlockSpec((kb, 2 * r, 2 * r), lambda k, l, p: (k, 0, 0)),
                  pl.BlockSpec((kb, 2 * r, 2 * r), lambda k, l, p: (k, 0, 0))],
        out_specs=pl.BlockSpec((None, kb, r, cb), lambda k, l, p: (p, k, 0, l)),
        out_shape=jax.ShapeDtypeStruct((pairs, n1, r, c), jnp.uint32),
        scratch_shapes=[pltpu.VMEM((kb, 2 * r, cb), F32)],
        name="dft_second",
        compiler_params=_params(("arbitrary", "arbitrary", "arbitrary"), big),
    )(a_data, a_filt, fwd, inv)

    z = pl.pallas_call(
        _kron_out_kernel,
        grid=(pairs, groups, c // cb),
        in_specs=[_resident(last.shape),
                  pl.BlockSpec((None, n1, s8, cb), lambda p, g, l: (p, 0, g, l)),
                  pl.BlockSpec((None, 2, half, s8, cb), lambda p, g, l: (p, 0, 0, g, l)),
                  pl.BlockSpec((None, 2, half, s8, cb), lambda p, g, l: (p, 0, 0, g, l)),
                  pl.BlockSpec((1, cb), lambda p, g, l: (0, l))],
        out_specs=pl.BlockSpec((None, 2, half, s8, cb), lambda p, g, l: (p, 0, 0, g, l)),
        out_shape=jax.ShapeDtypeStruct((pairs, 2, half, r, c), F32),
        name="dft_last",
        compiler_params=_params(("parallel", "parallel", "parallel"), big),
    )(last, b_data, x05, u5, bias.reshape(1, c))
    return z.reshape(batch * length, c)


@functools.lru_cache(maxsize=None)
def _dense_dft_tables(length):
    n = 2 * length
    idx = np.arange(n)
    f = np.exp(-2j * np.pi * np.outer(idx, idx) / n)
    fwd_data = _complex_as_real(f[:, :length])
    fwd_filt = np.concatenate([f.real, f.imag], axis=0)
    inv = _complex_as_real(np.conj(f[:length, :]) / n)
    return tuple(np.asarray(a, np.float32) for a in (fwd_data, fwd_filt, inv))


def _short_conv_gate_kernel(fd_ref, ff_ref, inv_ref, u_ref, x0_ref, k_ref, bias_ref, o_ref):
    n = k_ref.shape[0]
    u = u_ref[...]
    spec = _mdot(fd_ref[...], u.astype(BF16))
    kf = _mdot(ff_ref[...], k_ref[...].astype(BF16))
    xr, xi, kr, ki = spec[:n], spec[n:], kf[:n], kf[n:]
    prod = jnp.concatenate([xr * kr - xi * ki, xr * ki + xi * kr], axis=0).astype(BF16)
    o_ref[...] = x0_ref[...] * (_mdot(inv_ref[...], prod) + u * bias_ref[...])


def _long_conv_gate_short(u, x0, kfilt, bias, batch, length):
    c = u.shape[1]
    pairs = batch // 2
    cb = min(DFT_LANES, c)
    fwd_data, fwd_filt, inv = (jnp.asarray(t, BF16) for t in _dense_dft_tables(length))
    pair_spec = pl.BlockSpec((None, 2 * length, cb), lambda p, l: (p, 0, l))
    z = pl.pallas_call(
        _short_conv_gate_kernel,
        grid=(pairs, c // cb),
        in_specs=[_resident(fwd_data.shape), _resident(fwd_filt.shape), _resident(inv.shape),
                  pair_spec, pair_spec,
                  pl.BlockSpec((2 * length, cb), lambda p, l: (0, l)),
                  pl.BlockSpec((1, cb), lambda p, l: (0, l))],
        out_specs=pair_spec,
        out_shape=jax.ShapeDtypeStruct((pairs, 2 * length, c), F32),
        name="short_conv_gate",
        compiler_params=_params(("parallel", "parallel"), 0),
    )(fwd_data, fwd_filt, inv, u.reshape(pairs, 2 * length, c), x0.reshape(pairs, 2 * length, c),
      kfilt, bias.reshape(1, c))
    return z.reshape(batch * length, c)


def kernel(x, c, ctx, c_ctx, w_mod, b_mod, norm_g, ffn_w_gate, ffn_w_up, ffn_w_down, ab_w_in, na_rpb,
           sc_conv_w, ab_w_out, hy_w_in, hy_short_w, hy_f_w1, hy_f_b1, hy_f_w2, hy_f_b2, hy_f_w3, hy_f_b3,
           hy_f_w4, hy_sin_freq, hy_bias, hy_w_out):
    batch, seq, d = x.shape
    ctx_len = ctx.shape[1]
    depth = w_mod.shape[0]
    assert batch + 1 <= V7X_SUBLANES and seq % ROW_TILE == 0
    last_attn = (depth - 1) - (depth - 1) % 2
    sc_width = sc_conv_w.shape[-1]

    cc = jnp.zeros((V7X_SUBLANES, d), F32).at[:batch].set(c).at[batch].set(c_ctx)
    mod = _modulation(cc, w_mod, b_mod).reshape(depth, V7X_SUBLANES, N_MOD, d)
    lat_batch = lambda i, tm: i // (seq // tm)
    ctx_batch = lambda i, tm: batch

    xs = x.reshape(batch * seq, d)
    cs = ctx.reshape(batch * ctx_len, d)
    ffn_w = (ffn_w_gate.astype(BF16), ffn_w_up.astype(BF16), ffn_w_down.astype(BF16))

    for i in range(depth):
        j = i // 2
        g = norm_g[i]
        m_i = mod[i]
        keep_ctx = i <= last_attn
        upd_ctx = i < last_attn

        xs = _ffn(xs, m_i, g, *ffn_w, (i, 0), mrow=0, grow=0, batch_of=lat_batch)
        if keep_ctx:
            cs = _ffn(cs, m_i, g, *ffn_w, (i, 0), mrow=0, grow=0, batch_of=ctx_batch)

        if i % 2 == 0:
            w_in = ab_w_in[j].astype(BF16)
            w_out = ab_w_out[j].astype(BF16)
            ab_outs = [(2 * NA_WIDTH, BF16, False), (NA_WIDTH, BF16, True), (sc_width, BF16, False)]
            w_vt = w_in[:, 2 * NA_WIDTH:3 * NA_WIDTH].T
            in_ab = functools.partial(_inproj_call, _inproj_ab_kernel, "inproj_ab", mrow=3, grow=2, extra=(w_vt,))
            qk, vt, b_lat = in_ab(xs, m_i, g, w_in, sc_conv_w[j], ab_outs, seq_len=seq, batch_of=lat_batch)
            if upd_ctx:
                qk_c, vt_c, b_ctx = in_ab(cs, m_i, g, w_in, sc_conv_w[j], ab_outs, seq_len=ctx_len,
                                          batch_of=ctx_batch)
            else:
                qk_c, vt_c = in_ab(cs, m_i, g, w_in, sc_conv_w[j], ab_outs[:2], seq_len=ctx_len,
                                   batch_of=ctx_batch)
            a_lat = _neighbourhood_attention(qk, vt, qk_c, vt_c, na_rpb[j], batch, seq, ctx_len)
            mix_lat = ([a_lat, b_lat], w_out)
            if upd_ctx:
                mix_ctx = ([_context_attention(qk_c, vt_c, batch, ctx_len), b_ctx], w_out)
        else:
            w_in = hy_w_in[j].astype(BF16)
            w_out = hy_w_out[j].astype(BF16)
            hw = w_out.shape[0]
            filt = lambda n: _hyena_filter(n, hy_f_w1[j], hy_f_b1[j], hy_f_w2[j], hy_f_b2[j], hy_f_w3[j],
                                           hy_f_b3[j], hy_f_w4[j], hy_sin_freq[j])
            in_hy = functools.partial(_inproj_call, _inproj_hyena_kernel, "inproj_hyena", mrow=3, grow=2)
            x0, u = in_hy(xs, m_i, g, w_in, hy_short_w[j], [(hw, F32, False)] * 2, seq_len=seq,
                          batch_of=lat_batch)
            mix_lat = ([_long_conv_gate(u, x0, filt(seq), hy_bias[j], batch, seq)], w_out)
            if upd_ctx:
                x0_c, u_c = in_hy(cs, m_i, g, w_in, hy_short_w[j], [(hw, F32, False)] * 2, seq_len=ctx_len,
                                  batch_of=ctx_batch)
                mix_ctx = ([_long_conv_gate_short(u_c, x0_c, filt(ctx_len), hy_bias[j], batch, ctx_len)], w_out)

        xs = _ffn(xs, m_i, g, *ffn_w, (i, 1), mrow=6, grow=4, batch_of=lat_batch, mixer=mix_lat,
                  mix_mrow=5, mix_grow=3)
        if upd_ctx:
            cs = _ffn(cs, m_i, g, *ffn_w, (i, 1), mrow=6, grow=4, batch_of=ctx_batch, mixer=mix_ctx,
                      mix_mrow=5, mix_grow=3)

    return xs.reshape(batch, seq, d)
```

```python
import functools
import math

import numpy as np
import jax
import jax.numpy as jnp
from jax import lax
from jax.experimental import pallas as pl
from jax.experimental.pallas import tpu as pltpu

F32 = jnp.float32
BF16 = jnp.bfloat16

GRID_W = 64
NA_HEADS = 8
NA_HEAD_DIM = 64
NA_WIDTH = NA_HEADS * NA_HEAD_DIM
NA_KH = 8
NA_KW = 16
FFN_RES = 0.5
N_MOD = 9
RMS_EPS = 1e-6
HY_EMB = 33
HY_BANDS = (HY_EMB - 1) // 2
HY_TARGET = 1e-2
HY_FAST_PCT = 0.3
HY_SLOW_PCT = 1.5

V7X_LANES = 128
V7X_SUBLANES = 8
V7X_MXU_DIM = 256
V7X_VMEM_BYTES = 64 * 1024 * 1024

ROW_TILE = 512
FFN_CHUNK = 256
FFN_ROW_TILE = 1024
FFN_SUBBLOCKS = 2
PROJ_CHUNK = 512
NA_ROWS_PER_STEP = 8
NA_SLAB_ROWS = NA_KH + 2
DFT_RADIX = 128
DFT_LANES = 256
DFT_K1_PER_STEP = 8
FILTER_ROWS = 512
MASK_VALUE = -1e30


def _vmem_limit(nbytes):
    return int(min(max(nbytes, 32 * 1024 * 1024), V7X_VMEM_BYTES - 8 * 1024 * 1024))


def _params(semantics, vmem_bytes):
    return pltpu.CompilerParams(dimension_semantics=semantics, vmem_limit_bytes=_vmem_limit(vmem_bytes))


def _resident(shape):
    zeros = (0,) * len(shape)
    return pl.BlockSpec(shape, lambda *_: zeros, pipeline_mode=pl.Buffered(1))


def _rms(x):
    return x * lax.rsqrt(jnp.mean(x * x, axis=-1, keepdims=True) + RMS_EPS)


def _mdot(a, b):
    return jnp.dot(a, b, preferred_element_type=F32)


def _mod_kernel(c_ref, w_ref, b_ref, o_ref):
    c = c_ref[...]
    s = (c * jax.nn.sigmoid(c)).astype(BF16)
    o_ref[...] = _mdot(s, w_ref[...].astype(BF16)) + b_ref[...]


def _modulation(cc, w_mod, b_mod):
    depth, d, nd = w_mod.shape
    return pl.pallas_call(
        _mod_kernel,
        grid=(depth, nd // d),
        in_specs=[pl.BlockSpec((V7X_SUBLANES, d), lambda i, j: (0, 0)),
                  pl.BlockSpec((None, d, d), lambda i, j: (i, 0, j)),
                  pl.BlockSpec((None, 1, d), lambda i, j: (i, 0, j))],
        out_specs=pl.BlockSpec((None, V7X_SUBLANES, d), lambda i, j: (i, 0, j)),
        out_shape=jax.ShapeDtypeStruct((depth, V7X_SUBLANES, nd), F32),
        name="modulation",
        compiler_params=_params(("parallel", "parallel"), 4 * d * d * 4),
    )(cc, w_mod, b_mod.reshape(depth, 1, nd))


def _ffn_kernel(*refs, n_parts, mix_mrow, mix_grow, mrow, grow):
    part_refs = refs[:n_parts]
    w_out_ref = refs[n_parts] if n_parts else None
    x_ref, m_ref, g_ref, wg_ref, wu_ref, wd_ref, o_ref, acc_ref = refs[n_parts + bool(n_parts):]
    shift, scale, gate = m_ref[mrow:mrow + 1, :], m_ref[mrow + 1:mrow + 2, :], m_ref[mrow + 2:mrow + 3, :]
    pre_gain = g_ref[grow:grow + 1, :] * (1.0 + scale)
    post_gain = (FFN_RES * gate) * g_ref[grow + 1:grow + 2, :]
    hm = x_ref.shape[0] // FFN_SUBBLOCKS
    for s in range(FFN_SUBBLOCKS):
        rows = slice(s * hm, (s + 1) * hm)
        x = x_ref[rows, :]
        if n_parts:
            y, r0 = None, 0
            for p_ref in part_refs:
                r1 = r0 + p_ref.shape[1]
                term = _mdot(p_ref[rows, :].astype(BF16), w_out_ref[r0:r1, :])
                y = term if y is None else y + term
                r0 = r1
            x = x + m_ref[mix_mrow:mix_mrow + 1, :] * (_rms(y) * g_ref[mix_grow:mix_grow + 1, :])
        h = (_rms(x) * pre_gain + shift).astype(BF16)
        for j in range(wg_ref.shape[1] // FFN_CHUNK):
            c0, c1 = j * FFN_CHUNK, (j + 1) * FFN_CHUNK
            g = _mdot(h, wg_ref[:, c0:c1])
            u = _mdot(h, wu_ref[:, c0:c1])
            a = (g * jax.nn.sigmoid(g) * u).astype(BF16)
            part = _mdot(a, wd_ref[c0:c1, :])
            if j == 0:
                acc_ref[rows, :] = part
            else:
                acc_ref[rows, :] += part
        o_ref[rows, :] = x + _rms(acc_ref[rows, :]) * post_gain


def _layer_weight(w, index):
    lead = len(index)
    return pl.BlockSpec((None,) * lead + w.shape[lead:], lambda *_: tuple(index) + (0, 0),
                        pipeline_mode=pl.Buffered(1))


def _ffn(x, mod_i, g, wg, wu, wd, layer, *, mrow, grow, batch_of, mixer=None, mix_mrow=None, mix_grow=None):
    m, d = x.shape
    f = wg.shape[-1]
    tm = min(FFN_ROW_TILE, m)
    hm = tm // FFN_SUBBLOCKS
    parts, w_out = mixer if mixer else ((), None)
    kern = functools.partial(_ffn_kernel, n_parts=len(parts), mix_mrow=mix_mrow, mix_grow=mix_grow,
                             mrow=mrow, grow=grow)
    part_bytes = sum(p.shape[1] * p.dtype.itemsize for p in parts)
    vmem = (3 * d * f * 2 + 4 * tm * d * 4 + tm * d * 4 + 6 * hm * FFN_CHUNK * 4 + hm * d * 12
            + 2 * tm * part_bytes + (d * d * 2 if parts else 0))
    mix_specs = [pl.BlockSpec((tm, p.shape[1]), lambda i: (i, 0)) for p in parts]
    if parts:
        mix_specs.append(_resident(w_out.shape))
    return pl.pallas_call(
        kern,
        grid=(m // tm,),
        in_specs=mix_specs + [
            pl.BlockSpec((tm, d), lambda i: (i, 0)),
            pl.BlockSpec((None, N_MOD, d), lambda i: (batch_of(i, tm), 0, 0)),
            _resident(g.shape), _layer_weight(wg, layer), _layer_weight(wu, layer), _layer_weight(wd, layer)],
        out_specs=pl.BlockSpec((tm, d), lambda i: (i, 0)),
        out_shape=jax.ShapeDtypeStruct((m, d), F32),
        scratch_shapes=[pltpu.VMEM((tm, d), F32)],
        name="ffn_mix" if parts else "ffn",
        compiler_params=_params(("parallel",), vmem),
    )(*parts, *([w_out] if parts else []), x, mod_i, g, wg, wu, wd)


def _conv3_rows(pe, w0, w1, w2, tm):
    n = pe.shape[0]
    lo = V7X_SUBLANES
    down = pltpu.roll(pe, 1, 0)[lo:lo + tm]
    up = pltpu.roll(pe, n - 1, 0)[lo:lo + tm]
    return down * w0 + pe[lo:lo + tm] * w1 + up * w2


def _normed_with_halo(x_ref, prev_ref, next_ref, m_ref, g_ref, mrow, grow, seq_len):
    tm = x_ref.shape[0]
    lo = V7X_SUBLANES
    shift, scale = m_ref[mrow:mrow + 1, :], m_ref[mrow + 1:mrow + 2, :]
    xe = jnp.concatenate([prev_ref[...], x_ref[...], next_ref[...]], axis=0)
    he = (_rms(xe) * g_ref[grow:grow + 1, :]) * (1.0 + scale) + shift
    start = (pl.program_id(0) * tm) & (seq_len - 1)
    keep_prev = (start != 0).astype(F32)
    keep_next = (start + tm != seq_len).astype(F32)
    row = lax.broadcasted_iota(jnp.int32, (tm + 2 * lo, 1), 0)
    keep = jnp.where(row < lo, keep_prev, jnp.where(row >= lo + tm, keep_next, 1.0))
    return he[lo:lo + tm].astype(BF16), (he * keep).astype(BF16)


def _inproj_ab_kernel(x_ref, prev_ref, next_ref, m_ref, g_ref, w_ref, cw_ref, wvt_ref, qk_ref, vt_ref, *b_refs,
                      mrow, grow, seq_len):
    tm = x_ref.shape[0]
    h, he = _normed_with_halo(x_ref, prev_ref, next_ref, m_ref, g_ref, mrow, grow, seq_len)
    nqk = qk_ref.shape[1]
    nq = nqk + vt_ref.shape[0]
    for c in range(0, nqk, PROJ_CHUNK):
        qk_ref[:, c:c + PROJ_CHUNK] = _mdot(h, w_ref[:, c:c + PROJ_CHUNK]).astype(qk_ref.dtype)
    vt_ref[...] = lax.dot_general(wvt_ref[...], h, (((1,), (1,)), ((), ())),
                                  preferred_element_type=F32).astype(vt_ref.dtype)
    if b_refs:
        (b_ref,) = b_refs
        sw = cw_ref.shape[1]
        gate = _mdot(h, w_ref[:, nq:nq + sw])
        p = _mdot(he, w_ref[:, nq + sw:nq + 2 * sw]) * _mdot(he, w_ref[:, nq + 2 * sw:nq + 3 * sw])
        conv = _conv3_rows(p, cw_ref[0:1, :], cw_ref[1:2, :], cw_ref[2:3, :], tm)
        b_ref[...] = (gate * conv).astype(b_ref.dtype)


def _inproj_hyena_kernel(x_ref, prev_ref, next_ref, m_ref, g_ref, w_ref, cw_ref, x0_ref, u_ref,
                         *, mrow, grow, seq_len):
    tm = x_ref.shape[0]
    _, he = _normed_with_halo(x_ref, prev_ref, next_ref, m_ref, g_ref, mrow, grow, seq_len)
    c = x0_ref.shape[1]

    def conv_cols(c0):
        cols = slice(c0, c0 + PROJ_CHUNK)
        return _conv3_rows(_mdot(he, w_ref[:, cols]), cw_ref[0:1, cols], cw_ref[1:2, cols], cw_ref[2:3, cols], tm)

    for c0 in range(0, c, PROJ_CHUNK):
        x0_ref[:, c0:c0 + PROJ_CHUNK] = conv_cols(c0)
        u_ref[:, c0:c0 + PROJ_CHUNK] = conv_cols(2 * c + c0) * conv_cols(c + c0)


def _halo_specs(tm, m, width):
    blocks = m // V7X_SUBLANES
    per = tm // V7X_SUBLANES
    prev = pl.BlockSpec((V7X_SUBLANES, width), lambda i: (jnp.maximum(i * per - 1, 0), 0))
    nxt = pl.BlockSpec((V7X_SUBLANES, width), lambda i: (jnp.minimum((i + 1) * per, blocks - 1), 0))
    return prev, nxt


def _inproj_call(kern, name, x, mod_i, g, w, conv_w, outs, *, mrow, grow, seq_len, batch_of, extra=()):
    m, d = x.shape
    tm = min(ROW_TILE, seq_len)
    assert seq_len & (seq_len - 1) == 0 and seq_len % tm == 0 and m % seq_len == 0
    prev, nxt = _halo_specs(tm, m, d)
    out_bytes = sum(cols * jnp.dtype(dt).itemsize for cols, dt, _ in outs)
    vmem = w.size * 2 + 2 * tm * d * 4 + 2 * tm * out_bytes + 6 * tm * d * 4
    out_specs = [pl.BlockSpec((cols, tm), lambda i: (0, i)) if tr else pl.BlockSpec((tm, cols), lambda i: (i, 0))
                 for cols, _, tr in outs]
    out_shape = [jax.ShapeDtypeStruct((cols, m) if tr else (m, cols), dt) for cols, dt, tr in outs]
    return pl.pallas_call(
        functools.partial(kern, mrow=mrow, grow=grow, seq_len=seq_len),
        grid=(m // tm,),
        in_specs=[pl.BlockSpec((tm, d), lambda i: (i, 0)), prev, nxt,
                  pl.BlockSpec((None, N_MOD, d), lambda i: (batch_of(i, tm), 0, 0)),
                  _resident(g.shape), _resident(w.shape), _resident(conv_w.shape)]
        + [_resident(e.shape) for e in extra],
        out_specs=out_specs,
        out_shape=out_shape,
        name=name,
        compiler_params=_params(("parallel",), vmem),
    )(x, x, x, mod_i, g, w, conv_w, *extra)


def _split_heads_on_rows(q, low_lanes):
    zero = jnp.zeros_like(q)
    return jnp.concatenate([jnp.where(low_lanes, q, zero), jnp.where(low_lanes, zero, q)], axis=0)


def _na_kernel(pat_ref, q_ref, k_ref, vt_ref, kc_ref, vct_ref, bt_ref, o_ref, *, rows):
    rb = pl.program_id(2)
    low_lanes = lax.broadcasted_iota(jnp.int32, (GRID_W, V7X_LANES), 1) < NA_HEAD_DIM
    kc = kc_ref[...]
    vct = vct_ref[...]
    nt = (((1,), (1,)), ((), ()))

    for t in range(NA_ROWS_PER_STEP // 2):
        r = rb * NA_ROWS_PER_STEP + 2 * t
        base = jnp.clip(r - NA_KH // 2, 0, rows - NA_SLAB_ROWS)
        k0 = pl.multiple_of(base * GRID_W, 2 * GRID_W)
        qa = q_ref[2 * t * GRID_W:(2 * t + 1) * GRID_W, :] * (NA_HEAD_DIM ** -0.5)
        qb = q_ref[(2 * t + 1) * GRID_W:(2 * t + 2) * GRID_W, :] * (NA_HEAD_DIM ** -0.5)
        q4 = jnp.concatenate([_split_heads_on_rows(qa, low_lanes), _split_heads_on_rows(qb, low_lanes)], axis=0)
        kk = k_ref[pl.ds(k0, NA_SLAB_ROWS * GRID_W), :]
        s = lax.dot_general(kk, q4, nt, preferred_element_type=F32) + bt_ref[pat_ref[r // 2]]
        sc = lax.dot_general(kc, q4, nt, preferred_element_type=F32)
        mx = jnp.maximum(jnp.max(s, axis=0, keepdims=True), jnp.max(sc, axis=0, keepdims=True))
        p = jnp.exp(s - mx)
        pc = jnp.exp(sc - mx)
        den = jnp.sum(p, axis=0, keepdims=True) + jnp.sum(pc, axis=0, keepdims=True)
        vv = vt_ref[:, pl.ds(k0, NA_SLAB_ROWS * GRID_W)]
        ot = (_mdot(vv, p.astype(BF16)) + _mdot(vct, pc.astype(BF16))) / den
        o4 = ot.T
        oa = jnp.where(low_lanes, o4[0:GRID_W], o4[GRID_W:2 * GRID_W])
        ob = jnp.where(low_lanes, o4[2 * GRID_W:3 * GRID_W], o4[3 * GRID_W:4 * GRID_W])
        o_ref[2 * t * GRID_W:(2 * t + 2) * GRID_W, :] = jnp.concatenate([oa, ob], axis=0).astype(o_ref.dtype)


def _na_window_row(r, rows):
    return min(max(r - NA_KH // 2, 0), rows - NA_KH)


def _na_bias_table(rpb, rows):
    h = rpb.shape[0]
    c = np.arange(GRID_W)[:, None]
    kc = np.arange(GRID_W)[None, :]
    start = np.clip(c - NA_KW // 2, 0, GRID_W - NA_KW)
    valid = (kc >= start) & (kc < start + NA_KW)
    onehot = ((kc - c + (NA_KW - 1))[:, :, None] == np.arange(2 * NA_KW - 1)) & valid[:, :, None]
    dense = jnp.einsum("hdx,ckx->hdkc", rpb, jnp.asarray(onehot, F32), precision=lax.Precision.HIGHEST)
    dense = dense + jnp.asarray(np.where(valid.T, 0.0, MASK_VALUE), F32)
    masked = jnp.full((h, GRID_W, GRID_W), MASK_VALUE, F32)
    patterns, ids = [], []
    for r in range(0, rows, 2):
        base = min(max(r - NA_KH // 2, 0), rows - NA_SLAB_ROWS)
        key = tuple((_na_window_row(r + s, rows) - base, _na_window_row(r + s, rows) - (r + s) + NA_KH - 1)
                    for s in (0, 1))
        if key not in patterns:
            patterns.append(key)
        ids.append(patterns.index(key))
    tabs = []
    for key in patterns:
        per_row = []
        for off, dy0 in key:
            blocks = [dense[:, dy0 + y - off] if off <= y < off + NA_KH else masked for y in range(NA_SLAB_ROWS)]
            per_row.append(jnp.concatenate(blocks, axis=1))
        t = jnp.stack(per_row, axis=1).reshape(h // 2, 2, 2, NA_SLAB_ROWS * GRID_W, GRID_W)
        tabs.append(t.transpose(0, 3, 2, 1, 4).reshape(h // 2, NA_SLAB_ROWS * GRID_W, 4 * GRID_W))
    return jnp.stack(tabs, axis=1), np.asarray(ids, np.int32)


def _neighbourhood_attention(qk, vt, qk_c, vt_c, rpb, batch, seq, ctx_len):
    rows = seq // GRID_W
    assert rows >= NA_SLAB_ROWS and rows % NA_ROWS_PER_STEP == 0 and NA_ROWS_PER_STEP % 2 == 0
    nq = NA_WIDTH // V7X_LANES
    qk3 = qk.reshape(batch, seq, 2 * NA_WIDTH)
    qkc3 = qk_c.reshape(batch, ctx_len, 2 * NA_WIDTH)
    tq = NA_ROWS_PER_STEP * GRID_W
    table, pattern_ids = _na_bias_table(rpb, rows)
    vmem = 4 * seq * V7X_LANES * 2 + 2 * table[0].size * 4 + 16 * 1024 * 1024
    out = pl.pallas_call(
        functools.partial(_na_kernel, rows=rows),
        grid=(batch, nq, rows // NA_ROWS_PER_STEP),
        in_specs=[pl.BlockSpec(memory_space=pltpu.SMEM),
                  pl.BlockSpec((None, tq, V7X_LANES), lambda b, h, r: (b, r, h)),
                  pl.BlockSpec((None, seq, V7X_LANES), lambda b, h, r: (b, 0, nq + h)),
                  pl.BlockSpec((V7X_LANES, seq), lambda b, h, r: (h, b)),
                  pl.BlockSpec((None, ctx_len, V7X_LANES), lambda b, h, r: (b, 0, nq + h)),
                  pl.BlockSpec((V7X_LANES, ctx_len), lambda b, h, r: (h, b)),
                  pl.BlockSpec((None,) + table.shape[1:], lambda b, h, r: (h, 0, 0, 0))],
        out_specs=pl.BlockSpec((None, tq, V7X_LANES), lambda b, h, r: (b, r, h)),
        out_shape=jax.ShapeDtypeStruct((batch, seq, NA_WIDTH), BF16),
        name="na_attn",
        compiler_params=_params(("parallel", "parallel", "arbitrary"), vmem),
    )(jnp.asarray(pattern_ids), qk3, qk3, vt, qkc3, vt_c, table)
    return out.reshape(batch * seq, NA_WIDTH)


def _ctx_attn_kernel(q_ref, k_ref, vt_ref, o_ref):
    n = q_ref.shape[0]
    nt = (((1,), (1,)), ((), ()))
    low_lanes = lax.broadcasted_iota(jnp.int32, (n, V7X_LANES), 1) < NA_HEAD_DIM
    q2 = _split_heads_on_rows(q_ref[...] * (NA_HEAD_DIM ** -0.5), low_lanes)
    s = lax.dot_general(q2, k_ref[...], nt, preferred_element_type=F32)
    p = jnp.exp(s - jnp.max(s, axis=-1, keepdims=True))
    o2 = lax.dot_general(p.astype(BF16), vt_ref[...], nt, preferred_element_type=F32)
    o2 = o2 / jnp.sum(p, axis=-1, keepdims=True)
    o_ref[...] = jnp.where(low_lanes, o2[:n], o2[n:]).astype(o_ref.dtype)


def _context_attention(qk_c, vt_c, batch, ctx_len):
    nq = NA_WIDTH // V7X_LANES
    qkc3 = qk_c.reshape(batch, ctx_len, 2 * NA_WIDTH)
    out = pl.pallas_call(
        _ctx_attn_kernel,
        grid=(batch, nq),
        in_specs=[pl.BlockSpec((None, ctx_len, V7X_LANES), lambda b, h: (b, 0, h)),
                  pl.BlockSpec((None, ctx_len, V7X_LANES), lambda b, h: (b, 0, nq + h)),
                  pl.BlockSpec((V7X_LANES, ctx_len), lambda b, h: (h, b))],
        out_specs=pl.BlockSpec((None, ctx_len, V7X_LANES), lambda b, h: (b, 0, h)),
        out_shape=jax.ShapeDtypeStruct((batch, ctx_len, NA_WIDTH), BF16),
        name="ctx_attn",
        compiler_params=_params(("parallel", "parallel"), 0),
    )(qkc3, qkc3, vt_c)
    return out.reshape(batch * ctx_len, NA_WIDTH)


def _filter_kernel(z_ref, w1_ref, b1_ref, w2_ref, b2_ref, w3_ref, b3_ref, w4lo_ref, w4hi_ref, fr_ref, dec_ref,
                   o_ref, *, length):
    hi = lax.Precision.HIGHEST
    half = V7X_LANES // 2
    z = z_ref[...]
    fr = fr_ref[...]
    h = jnp.sin(fr * (jnp.dot(z, w1_ref[...], precision=hi, preferred_element_type=F32) + b1_ref[...]))
    h = jnp.sin(fr * (jnp.dot(h, w2_ref[...], precision=hi, preferred_element_type=F32) + b2_ref[...]))
    h = jnp.sin(fr * (jnp.dot(h, w3_ref[...], precision=hi, preferred_element_type=F32) + b3_ref[...]))
    hb = h.astype(BF16)
    n = z.shape[0]
    first = pl.program_id(0) * 2 * n
    for part, (w4_ref, t_lane) in enumerate(((w4lo_ref, 0), (w4hi_ref, half))):
        out = _mdot(hb, w4_ref[...].astype(BF16)) * jnp.exp(-z[:, t_lane:t_lane + 1] * dec_ref[...])
        row = first + part * n + lax.broadcasted_iota(jnp.int32, out.shape, 0)
        o_ref[part * n:(part + 1) * n, :] = jnp.where(row == length, 0.0, out)


@functools.lru_cache(maxsize=None)
def _filter_positions(length, block):
    half = V7X_LANES // 2
    t = np.linspace(0.0, 1.0, length)[:, None]
    w = 2.0 * math.pi * np.arange(length)[:, None] / length
    bands = np.linspace(1e-4, HY_BANDS - 1, HY_BANDS)[None, :]
    z = np.concatenate([t, np.cos(bands * w), -np.sin(bands * w)], axis=-1)
    z2 = np.concatenate([z, z[0:1], z[:0:-1]], axis=0)
    feat = np.zeros((2 * length, half), np.float32)
    feat[:, :HY_EMB] = z2
    feat = feat.reshape(2 * length // block, 2, block // 2, half)
    return np.ascontiguousarray(feat.transpose(0, 2, 1, 3)).reshape(length, V7X_LANES)


def _hyena_filter(length, w1, b1, w2, b2, w3, b3, w4, freq):
    c = w4.shape[1] // 2
    ffn = w2.shape[0]
    half = V7X_LANES // 2
    assert ffn == half and w1.shape[0] <= half
    tr = min(FILTER_ROWS, length)
    per_half = length // tr
    zpos = jnp.asarray(_filter_positions(length, tr))
    twice = lambda m: jnp.zeros((2 * m.shape[0], 2 * m.shape[1]), F32).at[:m.shape[0], :m.shape[1]].set(m).at[
        m.shape[0]:, m.shape[1]:].set(m)
    w1d = twice(jnp.pad(w1, ((0, half - w1.shape[0]), (0, 0))))
    w4lo = jnp.pad(w4, ((0, ffn), (0, 0)))
    w4hi = jnp.pad(w4, ((ffn, 0), (0, 0)))
    decay = np.abs(np.linspace(math.log(HY_TARGET) / HY_SLOW_PCT, math.log(HY_TARGET) / HY_FAST_PCT, c))
    decay = jnp.asarray(decay[None, :], F32)
    row2 = lambda v: jnp.tile(v.reshape(1, -1), (1, 2))
    small = lambda shape: pl.BlockSpec(shape, lambda i: (0, 0))
    mat, vec = small((V7X_LANES, V7X_LANES)), small((1, V7X_LANES))
    w4_spec = pl.BlockSpec((V7X_LANES, c), lambda i: (0, i // per_half))
    return pl.pallas_call(
        functools.partial(_filter_kernel, length=length),
        grid=(2 * per_half,),
        in_specs=[pl.BlockSpec((tr // 2, V7X_LANES), lambda i: (i, 0)),
                  mat, vec, mat, vec, mat, vec, w4_spec, w4_spec, vec, small((1, c))],
        out_specs=pl.BlockSpec((tr, c), lambda i: (i, 0)),
        out_shape=jax.ShapeDtypeStruct((2 * length, c), F32),
        name="hyena_filter",
        compiler_params=_params(("parallel",), 0),
    )(zpos, w1d, row2(b1), twice(w2), row2(b2), twice(w3), row2(b3), w4lo, w4hi, row2(freq), decay)


def _complex_as_real(m):
    return np.block([[m.real, -m.imag], [m.imag, m.real]])


@functools.lru_cache(maxsize=None)
def _dft_tables(length):
    n = 2 * length
    n1 = n // DFT_RADIX
    half = n1 // 2
    eye = np.eye(V7X_SUBLANES)
    k1 = np.arange(n1)
    f1 = np.exp(-2j * np.pi * np.outer(k1, k1) / n1)
    first_data = np.kron(_complex_as_real(f1[:, :half]), eye)
    first_filt = np.kron(np.concatenate([f1.real, f1.imag], axis=0), eye)
    inv1 = np.exp(2j * np.pi * np.outer(np.arange(half), k1) / n1) / n
    last = np.kron(_complex_as_real(inv1), eye)
    i2 = np.arange(DFT_RADIX)
    phase = (np.outer(i2, i2)[None] / DFT_RADIX + (i2[None, None, :] * k1[:, None, None]) / n)
    second = np.exp(-2j * np.pi * phase)
    fwd = np.stack([_complex_as_real(second[k]) for k in range(n1)])
    inv = np.stack([_complex_as_real(np.conj(second[k]).T) for k in range(n1)])
    return tuple(np.asarray(a, np.float32) for a in (first_data, first_filt, last, fwd, inv))


def _pack_pair(re, im):
    re_bits = lax.bitcast_convert_type(re.astype(BF16).astype(F32), jnp.uint32)
    im_bits = lax.bitcast_convert_type(im.astype(BF16).astype(F32), jnp.uint32)
    return (re_bits >> 16) | im_bits


def _unpack_pair(packed):
    re = lax.bitcast_convert_type(packed << 16, F32)
    im = lax.bitcast_convert_type(packed & jnp.uint32(0xFFFF0000), F32)
    return jnp.concatenate([re, im], axis=0).astype(BF16)


def _kron_kernel(mat_ref, x_ref, o_ref):
    lanes = x_ref.shape[-1]
    spec = _mdot(mat_ref[...], x_ref[...].reshape(-1, lanes).astype(BF16))
    rows = spec.shape[0] // 2
    o_ref[...] = _pack_pair(spec[:rows], spec[rows:]).reshape(o_ref.shape)


def _kron_out_kernel(mat_ref, b_ref, x0_ref, u_ref, bias_ref, o_ref):
    lanes = b_ref.shape[-1]
    y = _mdot(mat_ref[...], _unpack_pair(b_ref[...].reshape(-1, lanes))).reshape(o_ref.shape)
    o_ref[...] = x0_ref[...] * (y + u_ref[...] * bias_ref[...])


def _second_stage_kernel(a_ref, af_ref, g_ref, gi_ref, o_ref, kf_ref):
    r = DFT_RADIX
    steps = a_ref.shape[0]

    @pl.when(pl.program_id(2) == 0)
    def _():
        for j in range(steps):
            kf_ref[j] = _mdot(g_ref[j], _unpack_pair(af_ref[j]))

    for j in range(steps):
        spec = _mdot(g_ref[j], _unpack_pair(a_ref[j]))
        xr, xi = spec[:r], spec[r:]
        kr, ki = kf_ref[j, :r], kf_ref[j, r:]
        prod = jnp.concatenate([xr * kr - xi * ki, xr * ki + xi * kr], axis=0).astype(BF16)
        back = _mdot(gi_ref[j], prod)
        o_ref[j] = _pack_pair(back[:r], back[r:])


def _long_conv_gate(u, x0, kfilt, bias, batch, length):
    c = u.shape[1]
    pairs = batch // 2
    n1 = 2 * length // DFT_RADIX
    half = n1 // 2
    r, s8, cb, kb = DFT_RADIX, V7X_SUBLANES, min(DFT_LANES, c), DFT_K1_PER_STEP
    assert batch % 2 == 0 and n1 % kb == 0 and c % cb == 0
    first_data, first_filt, last, fwd, inv = (jnp.asarray(t, BF16) for t in _dft_tables(length))
    groups = r // s8
    big = 48 * 1024 * 1024

    u5 = u.reshape(pairs, 2, half, r, c)
    x05 = x0.reshape(pairs, 2, half, r, c)
    a_data = pl.pallas_call(
        _kron_kernel,
        grid=(pairs, groups, c // cb),
        in_specs=[_resident(first_data.shape),
                  pl.BlockSpec((None, 2, half, s8, cb), lambda p, g, l: (p, 0, 0, g, l))],
        out_specs=pl.BlockSpec((None, n1, s8, cb), lambda p, g, l: (p, 0, g, l)),
        out_shape=jax.ShapeDtypeStruct((pairs, n1, r, c), jnp.uint32),
        name="dft_first",
        compiler_params=_params(("parallel", "parallel", "parallel"), big),
    )(first_data, u5)

    a_filt = pl.pallas_call(
        _kron_kernel,
        grid=(groups, c // cb),
        in_specs=[_resident(first_filt.shape),
                  pl.BlockSpec((n1, s8, cb), lambda g, l: (0, g, l))],
        out_specs=pl.BlockSpec((n1, s8, cb), lambda g, l: (0, g, l)),
        out_shape=jax.ShapeDtypeStruct((n1, r, c), jnp.uint32),
        name="dft_first_filter",
        compiler_params=_params(("parallel", "parallel"), big),
    )(first_filt, kfilt.reshape(n1, r, c))

    b_data = pl.pallas_call(
        _second_stage_kernel,
        grid=(n1 // kb, c // cb, pairs),
        in_specs=[pl.BlockSpec((None, kb, r, cb), lambda k, l, p: (p, k, 0, l)),
                  pl.BlockSpec((kb, r, cb), lambda k, l, p: (k, 0, l)),
                  pl.BlockSpec((kb, 2 * r, 2 * r), lambda k, l, p: (k, 0, 0)),
                  pl.BlockSpec((kb, 2 * r, 2 * r), lambda k, l, p: (k, 0, 0))],
        out_specs=pl.BlockSpec((None, kb, r, cb), lambda k, l, p: (p, k, 0, l)),
        out_shape=jax.ShapeDtypeStruct((pairs, n1, r, c), jnp.uint32),
        scratch_shapes=[pltpu.VMEM((kb, 2 * r, cb), F32)],
        name="dft_second",
        compiler_params=_params(("arbitrary", "arbitrary", "arbitrary"), big),
    )(a_data, a_filt, fwd, inv)

    z = pl.pallas_call(
        _kron_out_kernel,
        grid=(pairs, groups, c // cb),
        in_specs=[_resident(last.shape),
                  pl.BlockSpec((None, n1, s8, cb), lambda p, g, l: (p, 0, g, l)),
                  pl.BlockSpec((None, 2, half, s8, cb), lambda p, g, l: (p, 0, 0, g, l)),
                  pl.BlockSpec((None, 2, half, s8, cb), lambda p, g, l: (p, 0, 0, g, l)),
                  pl.BlockSpec((1, cb), lambda p, g, l: (0, l))],
        out_specs=pl.BlockSpec((None, 2, half, s8, cb), lambda p, g, l: (p, 0, 0, g, l)),
        out_shape=jax.ShapeDtypeStruct((pairs, 2, half, r, c), F32),
        name="dft_last",
        compiler_params=_params(("parallel", "parallel", "parallel"), big),
    )(last, b_data, x05, u5, bias.reshape(1, c))
    return z.reshape(batch * length, c)


@functools.lru_cache(maxsize=None)
def _dense_dft_tables(length):
    n = 2 * length
    idx = np.arange(n)
    f = np.exp(-2j * np.pi * np.outer(idx, idx) / n)
    fwd_data = _complex_as_real(f[:, :length])
    fwd_filt = np.concatenate([f.real, f.imag], axis=0)
    inv = _complex_as_real(np.conj(f[:length, :]) / n)
    return tuple(np.asarray(a, np.float32) for a in (fwd_data, fwd_filt, inv))


def _short_conv_gate_kernel(fd_ref, ff_ref, inv_ref, u_ref, x0_ref, k_ref, bias_ref, o_ref):
    n = k_ref.shape[0]
    u = u_ref[...]
    spec = _mdot(fd_ref[...], u.astype(BF16))
    kf = _mdot(ff_ref[...], k_ref[...].astype(BF16))
    xr, xi, kr, ki = spec[:n], spec[n:], kf[:n], kf[n:]
    prod = jnp.concatenate([xr * kr - xi * ki, xr * ki + xi * kr], axis=0).astype(BF16)
    o_ref[...] = x0_ref[...] * (_mdot(inv_ref[...], prod) + u * bias_ref[...])


def _long_conv_gate_short(u, x0, kfilt, bias, batch, length):
    c = u.shape[1]
    pairs = batch // 2
    cb = min(DFT_LANES, c)
    fwd_data, fwd_filt, inv = (jnp.asarray(t, BF16) for t in _dense_dft_tables(length))
    pair_spec = pl.BlockSpec((None, 2 * length, cb), lambda p, l: (p, 0, l))
    z = pl.pallas_call(
        _short_conv_gate_kernel,
        grid=(pairs, c // cb),
        in_specs=[_resident(fwd_data.shape), _resident(fwd_filt.shape), _resident(inv.shape),
                  pair_spec, pair_spec,
                  pl.BlockSpec((2 * length, cb), lambda p, l: (0, l)),
                  pl.BlockSpec((1, cb), lambda p, l: (0, l))],
        out_specs=pair_spec,
        out_shape=jax.ShapeDtypeStruct((pairs, 2 * length, c), F32),
        name="short_conv_gate",
        compiler_params=_params(("parallel", "parallel"), 0),
    )(fwd_data, fwd_filt, inv, u.reshape(pairs, 2 * length, c), x0.reshape(pairs, 2 * length, c),
      kfilt, bias.reshape(1, c))
    return z.reshape(batch * length, c)


def kernel(x, c, ctx, c_ctx, w_mod, b_mod, norm_g, ffn_w_gate, ffn_w_up, ffn_w_down, ab_w_in, na_rpb,
           sc_conv_w, ab_w_out, hy_w_in, hy_short_w, hy_f_w1, hy_f_b1, hy_f_w2, hy_f_b2, hy_f_w3, hy_f_b3,
           hy_f_w4, hy_sin_freq, hy_bias, hy_w_out):
    batch, seq, d = x.shape
    ctx_len = ctx.shape[1]
    depth = w_mod.shape[0]
    assert batch + 1 <= V7X_SUBLANES and seq % ROW_TILE == 0
    last_attn = (depth - 1) - (depth - 1) % 2
    sc_width = sc_conv_w.shape[-1]

    cc = jnp.zeros((V7X_SUBLANES, d), F32).at[:batch].set(c).at[batch].set(c_ctx)
    mod = _modulation(cc, w_mod, b_mod).reshape(depth, V7X_SUBLANES, N_MOD, d)
    lat_batch = lambda i, tm: i // (seq // tm)
    ctx_batch = lambda i, tm: batch

    xs = x.reshape(batch * seq, d)
    cs = ctx.reshape(batch * ctx_len, d)
    ffn_w = (ffn_w_gate.astype(BF16), ffn_w_up.astype(BF16), ffn_w_down.astype(BF16))

    for i in range(depth):
        j = i // 2
        g = norm_g[i]
        m_i = mod[i]
        keep_ctx = i <= last_attn
        upd_ctx = i < last_attn

        xs = _ffn(xs, m_i, g, *ffn_w, (i, 0), mrow=0, grow=0, batch_of=lat_batch)
        if keep_ctx:
            cs = _ffn(cs, m_i, g, *ffn_w, (i, 0), mrow=0, grow=0, batch_of=ctx_batch)

        if i % 2 == 0:
            w_in = ab_w_in[j].astype(BF16)
            w_out = ab_w_out[j].astype(BF16)
            ab_outs = [(2 * NA_WIDTH, BF16, False), (NA_WIDTH, BF16, True), (sc_width, BF16, False)]
            w_vt = w_in[:, 2 * NA_WIDTH:3 * NA_WIDTH].T
            in_ab = functools.partial(_inproj_call, _inproj_ab_kernel, "inproj_ab", mrow=3, grow=2, extra=(w_vt,))
            qk, vt, b_lat = in_ab(xs, m_i, g, w_in, sc_conv_w[j], ab_outs, seq_len=seq, batch_of=lat_batch)
            if upd_ctx:
                qk_c, vt_c, b_ctx = in_ab(cs, m_i, g, w_in, sc_conv_w[j], ab_outs, seq_len=ctx_len,
                                          batch_of=ctx_batch)
            else:
                qk_c, vt_c = in_ab(cs, m_i, g, w_in, sc_conv_w[j], ab_outs[:2], seq_len=ctx_len,
                                   batch_of=ctx_batch)
            a_lat = _neighbourhood_attention(qk, vt, qk_c, vt_c, na_rpb[j], batch, seq, ctx_len)
            mix_lat = ([a_lat, b_lat], w_out)
            if upd_ctx:
                mix_ctx = ([_context_attention(qk_c, vt_c, batch, ctx_len), b_ctx], w_out)
        else:
            w_in = hy_w_in[j].astype(BF16)
            w_out = hy_w_out[j].astype(BF16)
            hw = w_out.shape[0]
            filt = lambda n: _hyena_filter(n, hy_f_w1[j], hy_f_b1[j], hy_f_w2[j], hy_f_b2[j], hy_f_w3[j],
                                           hy_f_b3[j], hy_f_w4[j], hy_sin_freq[j])
            in_hy = functools.partial(_inproj_call, _inproj_hyena_kernel, "inproj_hyena", mrow=3, grow=2)
            x0, u = in_hy(xs, m_i, g, w_in, hy_short_w[j], [(hw, F32, False)] * 2, seq_len=seq,
                          batch_of=lat_batch)
            mix_lat = ([_long_conv_gate(u, x0, filt(seq), hy_bias[j], batch, seq)], w_out)
            if upd_ctx:
                x0_c, u_c = in_hy(cs, m_i, g, w_in, hy_short_w[j], [(hw, F32, False)] * 2, seq_len=ctx_len,
                                  batch_of=ctx_batch)
                mix_ctx = ([_long_conv_gate_short(u_c, x0_c, filt(ctx_len), hy_bias[j], batch, ctx_len)], w_out)

        xs = _ffn(xs, m_i, g, *ffn_w, (i, 1), mrow=6, grow=4, batch_of=lat_batch, mixer=mix_lat,
                  mix_mrow=5, mix_grow=3)
        if upd_ctx:
            cs = _ffn(cs, m_i, g, *ffn_w, (i, 1), mrow=6, grow=4, batch_of=ctx_batch, mixer=mix_ctx,
                      mix_mrow=5, mix_grow=3)

    return xs.reshape(batch, seq, d)
```

```python
import functools
import math

import numpy as np
import jax
import jax.numpy as jnp
from jax import lax
from jax.experimental import pallas as pl
from jax.experimental.pallas import tpu as pltpu

F32 = jnp.float32
BF16 = jnp.bfloat16

GRID_W = 64
NA_HEADS = 8
NA_HEAD_DIM = 64
NA_WIDTH = NA_HEADS * NA_HEAD_DIM
NA_KH = 8
NA_KW = 16
FFN_RES = 0.5
N_MOD = 9
RMS_EPS = 1e-6
HY_EMB = 33
HY_BANDS = (HY_EMB - 1) // 2
HY_TARGET = 1e-2
HY_FAST_PCT = 0.3
HY_SLOW_PCT = 1.5

V7X_LANES = 128
V7X_SUBLANES = 8
V7X_MXU_DIM = 256
V7X_VMEM_BYTES = 64 * 1024 * 1024

ROW_TILE = 512
FFN_CHUNK = 256
FFN_ROW_TILE = 1024
FFN_SUBBLOCKS = 2
PROJ_CHUNK = 512
NA_ROWS_PER_STEP = 8
NA_SLAB_ROWS = NA_KH + 2
DFT_RADIX = 128
DFT_LANES = 512
DFT_K1_PER_STEP = 16
FILTER_ROWS = 512
MASK_VALUE = -1e30


def _vmem_limit(nbytes):
    return int(min(max(nbytes, 32 * 1024 * 1024), V7X_VMEM_BYTES - 8 * 1024 * 1024))


def _params(semantics, vmem_bytes):
    return pltpu.CompilerParams(dimension_semantics=semantics, vmem_limit_bytes=_vmem_limit(vmem_bytes))


def _resident(shape):
    zeros = (0,) * len(shape)
    return pl.BlockSpec(shape, lambda *_: zeros, pipeline_mode=pl.Buffered(1))


def _rms(x):
    return x * lax.rsqrt(jnp.mean(x * x, axis=-1, keepdims=True) + RMS_EPS)


def _mdot(a, b):
    return jnp.dot(a, b, preferred_element_type=F32)


def _mod_kernel(c_ref, w_ref, b_ref, o_ref):
    c = c_ref[...]
    s = (c * jax.nn.sigmoid(c)).astype(BF16)
    o_ref[...] = _mdot(s, w_ref[...].astype(BF16)) + b_ref[...]


def _modulation(cc, w_mod, b_mod):
    depth, d, nd = w_mod.shape
    return pl.pallas_call(
        _mod_kernel,
        grid=(depth, nd // d),
        in_specs=[pl.BlockSpec((V7X_SUBLANES, d), lambda i, j: (0, 0)),
                  pl.BlockSpec((None, d, d), lambda i, j: (i, 0, j)),
                  pl.BlockSpec((None, 1, d), lambda i, j: (i, 0, j))],
        out_specs=pl.BlockSpec((None, V7X_SUBLANES, d), lambda i, j: (i, 0, j)),
        out_shape=jax.ShapeDtypeStruct((depth, V7X_SUBLANES, nd), F32),
        name="modulation",
        compiler_params=_params(("parallel", "parallel"), 4 * d * d * 4),
    )(cc, w_mod, b_mod.reshape(depth, 1, nd))


def _ffn_kernel(*refs, n_parts, mix_mrow, mix_grow, mrow, grow):
    part_refs = refs[:n_parts]
    w_out_ref = refs[n_parts] if n_parts else None
    x_ref, m_ref, g_ref, wg_ref, wu_ref, wd_ref, o_ref, acc_ref = refs[n_parts + bool(n_parts):]
    shift, scale, gate = m_ref[mrow:mrow + 1, :], m_ref[mrow + 1:mrow + 2, :], m_ref[mrow + 2:mrow + 3, :]
    pre_gain = g_ref[grow:grow + 1, :] * (1.0 + scale)
    post_gain = (FFN_RES * gate) * g_ref[grow + 1:grow + 2, :]
    hm = x_ref.shape[0] // FFN_SUBBLOCKS
    for s in range(FFN_SUBBLOCKS):
        rows = slice(s * hm, (s + 1) * hm)
        x = x_ref[rows, :]
        if n_parts:
            y, r0 = None, 0
            for p_ref in part_refs:
                r1 = r0 + p_ref.shape[1]
                term = _mdot(p_ref[rows, :].astype(BF16), w_out_ref[r0:r1, :])
                y = term if y is None else y + term
                r0 = r1
            x = x + m_ref[mix_mrow:mix_mrow + 1, :] * (_rms(y) * g_ref[mix_grow:mix_grow + 1, :])
        h = (_rms(x) * pre_gain + shift).astype(BF16)
        for j in range(wg_ref.shape[1] // FFN_CHUNK):
            c0, c1 = j * FFN_CHUNK, (j + 1) * FFN_CHUNK
            g = _mdot(h, wg_ref[:, c0:c1])
            u = _mdot(h, wu_ref[:, c0:c1])
            a = (g * jax.nn.sigmoid(g) * u).astype(BF16)
            part = _mdot(a, wd_ref[c0:c1, :])
            if j == 0:
                acc_ref[rows, :] = part
            else:
                acc_ref[rows, :] += part
        o_ref[rows, :] = x + _rms(acc_ref[rows, :]) * post_gain


def _layer_weight(w, index):
    lead = len(index)
    return pl.BlockSpec((None,) * lead + w.shape[lead:], lambda *_: tuple(index) + (0, 0),
                        pipeline_mode=pl.Buffered(1))


def _ffn(x, mod_i, g, wg, wu, wd, layer, *, mrow, grow, batch_of, mixer=None, mix_mrow=None, mix_grow=None):
    m, d = x.shape
    f = wg.shape[-1]
    tm = min(FFN_ROW_TILE, m)
    hm = tm // FFN_SUBBLOCKS
    parts, w_out = mixer if mixer else ((), None)
    kern = functools.partial(_ffn_kernel, n_parts=len(parts), mix_mrow=mix_mrow, mix_grow=mix_grow,
                             mrow=mrow, grow=grow)
    part_bytes = sum(p.shape[1] * p.dtype.itemsize for p in parts)
    vmem = (3 * d * f * 2 + 4 * tm * d * 4 + tm * d * 4 + 6 * hm * FFN_CHUNK * 4 + hm * d * 12
            + 2 * tm * part_bytes + (d * d * 2 if parts else 0))
    mix_specs = [pl.BlockSpec((tm, p.shape[1]), lambda i: (i, 0)) for p in parts]
    if parts:
        mix_specs.append(_resident(w_out.shape))
    return pl.pallas_call(
        kern,
        grid=(m // tm,),
        in_specs=mix_specs + [
            pl.BlockSpec((tm, d), lambda i: (i, 0)),
            pl.BlockSpec((None, N_MOD, d), lambda i: (batch_of(i, tm), 0, 0)),
            _resident(g.shape), _layer_weight(wg, layer), _layer_weight(wu, layer), _layer_weight(wd, layer)],
        out_specs=pl.BlockSpec((tm, d), lambda i: (i, 0)),
        out_shape=jax.ShapeDtypeStruct((m, d), F32),
        scratch_shapes=[pltpu.VMEM((tm, d), F32)],
        name="ffn_mix" if parts else "ffn",
        compiler_params=_params(("parallel",), vmem),
    )(*parts, *([w_out] if parts else []), x, mod_i, g, wg, wu, wd)


def _conv3_rows(pe, w0, w1, w2, tm):
    n = pe.shape[0]
    lo = V7X_SUBLANES
    down = pltpu.roll(pe, 1, 0)[lo:lo + tm]
    up = pltpu.roll(pe, n - 1, 0)[lo:lo + tm]
    return down * w0 + pe[lo:lo + tm] * w1 + up * w2


def _normed_with_halo(x_ref, prev_ref, next_ref, m_ref, g_ref, mrow, grow, seq_len):
    tm = x_ref.shape[0]
    lo = V7X_SUBLANES
    shift, scale = m_ref[mrow:mrow + 1, :], m_ref[mrow + 1:mrow + 2, :]
    xe = jnp.concatenate([prev_ref[...], x_ref[...], next_ref[...]], axis=0)
    he = (_rms(xe) * g_ref[grow:grow + 1, :]) * (1.0 + scale) + shift
    start = (pl.program_id(0) * tm) & (seq_len - 1)
    keep_prev = (start != 0).astype(F32)
    keep_next = (start + tm != seq_len).astype(F32)
    row = lax.broadcasted_iota(jnp.int32, (tm + 2 * lo, 1), 0)
    keep = jnp.where(row < lo, keep_prev, jnp.where(row >= lo + tm, keep_next, 1.0))
    return he[lo:lo + tm].astype(BF16), (he * keep).astype(BF16)


def _inproj_ab_kernel(x_ref, prev_ref, next_ref, m_ref, g_ref, w_ref, cw_ref, wvt_ref, qk_ref, vt_ref, *b_refs,
                      mrow, grow, seq_len):
    tm = x_ref.shape[0]
    h, he = _normed_with_halo(x_ref, prev_ref, next_ref, m_ref, g_ref, mrow, grow, seq_len)
    nqk = qk_ref.shape[1]
    nq = nqk + vt_ref.shape[0]
    for c in range(0, nqk, PROJ_CHUNK):
        qk_ref[:, c:c + PROJ_CHUNK] = _mdot(h, w_ref[:, c:c + PROJ_CHUNK]).astype(qk_ref.dtype)
    vt_ref[...] = lax.dot_general(wvt_ref[...], h, (((1,), (1,)), ((), ())),
                                  preferred_element_type=F32).astype(vt_ref.dtype)
    if b_refs:
        (b_ref,) = b_refs
        sw = cw_ref.shape[1]
        gate = _mdot(h, w_ref[:, nq:nq + sw])
        p = _mdot(he, w_ref[:, nq + sw:nq + 2 * sw]) * _mdot(he, w_ref[:, nq + 2 * sw:nq + 3 * sw])
        conv = _conv3_rows(p, cw_ref[0:1, :], cw_ref[1:2, :], cw_ref[2:3, :], tm)
        b_ref[...] = (gate * conv).astype(b_ref.dtype)


def _inproj_hyena_kernel(x_ref, prev_ref, next_ref, m_ref, g_ref, w_ref, cw_ref, x0_ref, u_ref,
                         *, mrow, grow, seq_len):
    tm = x_ref.shape[0]
    _, he = _normed_with_halo(x_ref, prev_ref, next_ref, m_ref, g_ref, mrow, grow, seq_len)
    c = x0_ref.shape[1]

    def conv_cols(c0):
        cols = slice(c0, c0 + PROJ_CHUNK)
        return _conv3_rows(_mdot(he, w_ref[:, cols]), cw_ref[0:1, cols], cw_ref[1:2, cols], cw_ref[2:3, cols], tm)

    for c0 in range(0, c, PROJ_CHUNK):
        x0_ref[:, c0:c0 + PROJ_CHUNK] = conv_cols(c0)
        u_ref[:, c0:c0 + PROJ_CHUNK] = conv_cols(2 * c + c0) * conv_cols(c + c0)


def _halo_specs(tm, m, width):
    blocks = m // V7X_SUBLANES
    per = tm // V7X_SUBLANES
    prev = pl.BlockSpec((V7X_SUBLANES, width), lambda i: (jnp.maximum(i * per - 1, 0), 0))
    nxt = pl.BlockSpec((V7X_SUBLANES, width), lambda i: (jnp.minimum((i + 1) * per, blocks - 1), 0))
    return prev, nxt


def _inproj_call(kern, name, x, mod_i, g, w, conv_w, outs, *, mrow, grow, seq_len, batch_of, extra=()):
    m, d = x.shape
    tm = min(ROW_TILE, seq_len)
    assert seq_len & (seq_len - 1) == 0 and seq_len % tm == 0 and m % seq_len == 0
    prev, nxt = _halo_specs(tm, m, d)
    out_bytes = sum(cols * jnp.dtype(dt).itemsize for cols, dt, _ in outs)
    vmem = w.size * 2 + 2 * tm * d * 4 + 2 * tm * out_bytes + 6 * tm * d * 4
    out_specs = [pl.BlockSpec((cols, tm), lambda i: (0, i)) if tr else pl.BlockSpec((tm, cols), lambda i: (i, 0))
                 for cols, _, tr in outs]
    out_shape = [jax.ShapeDtypeStruct((cols, m) if tr else (m, cols), dt) for cols, dt, tr in outs]
    return pl.pallas_call(
        functools.partial(kern, mrow=mrow, grow=grow, seq_len=seq_len),
        grid=(m // tm,),
        in_specs=[pl.BlockSpec((tm, d), lambda i: (i, 0)), prev, nxt,
                  pl.BlockSpec((None, N_MOD, d), lambda i: (batch_of(i, tm), 0, 0)),
                  _resident(g.shape), _resident(w.shape), _resident(conv_w.shape)]
        + [_resident(e.shape) for e in extra],
        out_specs=out_specs,
        out_shape=out_shape,
        name=name,
        compiler_params=_params(("parallel",), vmem),
    )(x, x, x, mod_i, g, w, conv_w, *extra)


def _split_heads_on_rows(q, low_lanes):
    zero = jnp.zeros_like(q)
    return jnp.concatenate([jnp.where(low_lanes, q, zero), jnp.where(low_lanes, zero, q)], axis=0)


def _na_kernel(pat_ref, q_ref, k_ref, vt_ref, kc_ref, vct_ref, bt_ref, o_ref, *, rows):
    rb = pl.program_id(2)
    low_lanes = lax.broadcasted_iota(jnp.int32, (GRID_W, V7X_LANES), 1) < NA_HEAD_DIM
    kc = kc_ref[...]
    vct = vct_ref[...]
    nt = (((1,), (1,)), ((), ()))

    for t in range(NA_ROWS_PER_STEP // 2):
        r = rb * NA_ROWS_PER_STEP + 2 * t
        base = jnp.clip(r - NA_KH // 2, 0, rows - NA_SLAB_ROWS)
        k0 = pl.multiple_of(base * GRID_W, 2 * GRID_W)
        qa = q_ref[2 * t * GRID_W:(2 * t + 1) * GRID_W, :] * (NA_HEAD_DIM ** -0.5)
        qb = q_ref[(2 * t + 1) * GRID_W:(2 * t + 2) * GRID_W, :] * (NA_HEAD_DIM ** -0.5)
        q4 = jnp.concatenate([_split_heads_on_rows(qa, low_lanes), _split_heads_on_rows(qb, low_lanes)], axis=0)
        kk = k_ref[pl.ds(k0, NA_SLAB_ROWS * GRID_W), :]
        s = lax.dot_general(kk, q4, nt, preferred_element_type=F32) + bt_ref[pat_ref[r // 2]]
        sc = lax.dot_general(kc, q4, nt, preferred_element_type=F32)
        mx = jnp.maximum(jnp.max(s, axis=0, keepdims=True), jnp.max(sc, axis=0, keepdims=True))
        p = jnp.exp(s - mx)
        pc = jnp.exp(sc - mx)
        den = jnp.sum(p, axis=0, keepdims=True) + jnp.sum(pc, axis=0, keepdims=True)
        vv = vt_ref[:, pl.ds(k0, NA_SLAB_ROWS * GRID_W)]
        ot = (_mdot(vv, p.astype(BF16)) + _mdot(vct, pc.astype(BF16))) / den
        o4 = ot.T
        oa = jnp.where(low_lanes, o4[0:GRID_W], o4[GRID_W:2 * GRID_W])
        ob = jnp.where(low_lanes, o4[2 * GRID_W:3 * GRID_W], o4[3 * GRID_W:4 * GRID_W])
        o_ref[2 * t * GRID_W:(2 * t + 2) * GRID_W, :] = jnp.concatenate([oa, ob], axis=0).astype(o_ref.dtype)


def _na_window_row(r, rows):
    return min(max(r - NA_KH // 2, 0), rows - NA_KH)


def _na_bias_table(rpb, rows):
    h = rpb.shape[0]
    c = np.arange(GRID_W)[:, None]
    kc = np.arange(GRID_W)[None, :]
    start = np.clip(c - NA_KW // 2, 0, GRID_W - NA_KW)
    valid = (kc >= start) & (kc < start + NA_KW)
    onehot = ((kc - c + (NA_KW - 1))[:, :, None] == np.arange(2 * NA_KW - 1)) & valid[:, :, None]
    dense = jnp.einsum("hdx,ckx->hdkc", rpb, jnp.asarray(onehot, F32), precision=lax.Precision.HIGHEST)
    dense = dense + jnp.asarray(np.where(valid.T, 0.0, MASK_VALUE), F32)
    masked = jnp.full((h, GRID_W, GRID_W), MASK_VALUE, F32)
    patterns, ids = [], []
    for r in range(0, rows, 2):
        base = min(max(r - NA_KH // 2, 0), rows - NA_SLAB_ROWS)
        key = tuple((_na_window_row(r + s, rows) - base, _na_window_row(r + s, rows) - (r + s) + NA_KH - 1)
                    for s in (0, 1))
        if key not in patterns:
            patterns.append(key)
        ids.append(patterns.index(key))
    tabs = []
    for key in patterns:
        per_row = []
        for off, dy0 in key:
            blocks = [dense[:, dy0 + y - off] if off <= y < off + NA_KH else masked for y in range(NA_SLAB_ROWS)]
            per_row.append(jnp.concatenate(blocks, axis=1))
        t = jnp.stack(per_row, axis=1).reshape(h // 2, 2, 2, NA_SLAB_ROWS * GRID_W, GRID_W)
        tabs.append(t.transpose(0, 3, 2, 1, 4).reshape(h // 2, NA_SLAB_ROWS * GRID_W, 4 * GRID_W))
    return jnp.stack(tabs, axis=1), np.asarray(ids, np.int32)


def _neighbourhood_attention(qk, vt, qk_c, vt_c, rpb, batch, seq, ctx_len):
    rows = seq // GRID_W
    assert rows >= NA_SLAB_ROWS and rows % NA_ROWS_PER_STEP == 0 and NA_ROWS_PER_STEP % 2 == 0
    nq = NA_WIDTH // V7X_LANES
    qk3 = qk.reshape(batch, seq, 2 * NA_WIDTH)
    qkc3 = qk_c.reshape(batch, ctx_len, 2 * NA_WIDTH)
    tq = NA_ROWS_PER_STEP * GRID_W
    table, pattern_ids = _na_bias_table(rpb, rows)
    vmem = 4 * seq * V7X_LANES * 2 + 2 * table[0].size * 4 + 16 * 1024 * 1024
    out = pl.pallas_call(
        functools.partial(_na_kernel, rows=rows),
        grid=(batch, nq, rows // NA_ROWS_PER_STEP),
        in_specs=[pl.BlockSpec(memory_space=pltpu.SMEM),
                  pl.BlockSpec((None, tq, V7X_LANES), lambda b, h, r: (b, r, h)),
                  pl.BlockSpec((None, seq, V7X_LANES), lambda b, h, r: (b, 0, nq + h)),
                  pl.BlockSpec((V7X_LANES, seq), lambda b, h, r: (h, b)),
                  pl.BlockSpec((None, ctx_len, V7X_LANES), lambda b, h, r: (b, 0, nq + h)),
                  pl.BlockSpec((V7X_LANES, ctx_len), lambda b, h, r: (h, b)),
                  pl.BlockSpec((None,) + table.shape[1:], lambda b, h, r: (h, 0, 0, 0))],
        out_specs=pl.BlockSpec((None, tq, V7X_LANES), lambda b, h, r: (b, r, h)),
        out_shape=jax.ShapeDtypeStruct((batch, seq, NA_WIDTH), BF16),
        name="na_attn",
        compiler_params=_params(("parallel", "parallel", "arbitrary"), vmem),
    )(jnp.asarray(pattern_ids), qk3, qk3, vt, qkc3, vt_c, table)
    return out.reshape(batch * seq, NA_WIDTH)


def _ctx_attn_kernel(q_ref, k_ref, vt_ref, o_ref):
    n = q_ref.shape[0]
    nt = (((1,), (1,)), ((), ()))
    low_lanes = lax.broadcasted_iota(jnp.int32, (n, V7X_LANES), 1) < NA_HEAD_DIM
    q2 = _split_heads_on_rows(q_ref[...] * (NA_HEAD_DIM ** -0.5), low_lanes)
    s = lax.dot_general(q2, k_ref[...], nt, preferred_element_type=F32)
    p = jnp.exp(s - jnp.max(s, axis=-1, keepdims=True))
    o2 = lax.dot_general(p.astype(BF16), vt_ref[...], nt, preferred_element_type=F32)
    o2 = o2 / jnp.sum(p, axis=-1, keepdims=True)
    o_ref[...] = jnp.where(low_lanes, o2[:n], o2[n:]).astype(o_ref.dtype)


def _context_attention(qk_c, vt_c, batch, ctx_len):
    nq = NA_WIDTH // V7X_LANES
    qkc3 = qk_c.reshape(batch, ctx_len, 2 * NA_WIDTH)
    out = pl.pallas_call(
        _ctx_attn_kernel,
        grid=(batch, nq),
        in_specs=[pl.BlockSpec((None, ctx_len, V7X_LANES), lambda b, h: (b, 0, h)),
                  pl.BlockSpec((None, ctx_len, V7X_LANES), lambda b, h: (b, 0, nq + h)),
                  pl.BlockSpec((V7X_LANES, ctx_len), lambda b, h: (h, b))],
        out_specs=pl.BlockSpec((None, ctx_len, V7X_LANES), lambda b, h: (b, 0, h)),
        out_shape=jax.ShapeDtypeStruct((batch, ctx_len, NA_WIDTH), BF16),
        name="ctx_attn",
        compiler_params=_params(("parallel", "parallel"), 0),
    )(qkc3, qkc3, vt_c)
    return out.reshape(batch * ctx_len, NA_WIDTH)


def _filter_kernel(z_ref, w1_ref, b1_ref, w2_ref, b2_ref, w3_ref, b3_ref, w4lo_ref, w4hi_ref, fr_ref, dec_ref,
                   o_ref, *, length):
    hi = lax.Precision.HIGHEST
    half = V7X_LANES // 2
    z = z_ref[...]
    fr = fr_ref[...]
    h = jnp.sin(fr * (jnp.dot(z, w1_ref[...], precision=hi, preferred_element_type=F32) + b1_ref[...]))
    h = jnp.sin(fr * (jnp.dot(h, w2_ref[...], precision=hi, preferred_element_type=F32) + b2_ref[...]))
    h = jnp.sin(fr * (jnp.dot(h, w3_ref[...], precision=hi, preferred_element_type=F32) + b3_ref[...]))
    hb = h.astype(BF16)
    n = z.shape[0]
    first = pl.program_id(0) * 2 * n
    for part, (w4_ref, t_lane) in enumerate(((w4lo_ref, 0), (w4hi_ref, half))):
        out = _mdot(hb, w4_ref[...].astype(BF16)) * jnp.exp(-z[:, t_lane:t_lane + 1] * dec_ref[...])
        row = first + part * n + lax.broadcasted_iota(jnp.int32, out.shape, 0)
        o_ref[part * n:(part + 1) * n, :] = jnp.where(row == length, 0.0, out)


@functools.lru_cache(maxsize=None)
def _filter_positions(length, block):
    half = V7X_LANES // 2
    t = np.linspace(0.0, 1.0, length)[:, None]
    w = 2.0 * math.pi * np.arange(length)[:, None] / length
    bands = np.linspace(1e-4, HY_BANDS - 1, HY_BANDS)[None, :]
    z = np.concatenate([t, np.cos(bands * w), -np.sin(bands * w)], axis=-1)
    z2 = np.concatenate([z, z[0:1], z[:0:-1]], axis=0)
    feat = np.zeros((2 * length, half), np.float32)
    feat[:, :HY_EMB] = z2
    feat = feat.reshape(2 * length // block, 2, block // 2, half)
    return np.ascontiguousarray(feat.transpose(0, 2, 1, 3)).reshape(length, V7X_LANES)


def _hyena_filter(length, w1, b1, w2, b2, w3, b3, w4, freq):
    c = w4.shape[1] // 2
    ffn = w2.shape[0]
    half = V7X_LANES // 2
    assert ffn == half and w1.shape[0] <= half
    tr = min(FILTER_ROWS, length)
    per_half = length // tr
    zpos = jnp.asarray(_filter_positions(length, tr))
    twice = lambda m: jnp.zeros((2 * m.shape[0], 2 * m.shape[1]), F32).at[:m.shape[0], :m.shape[1]].set(m).at[
        m.shape[0]:, m.shape[1]:].set(m)
    w1d = twice(jnp.pad(w1, ((0, half - w1.shape[0]), (0, 0))))
    w4lo = jnp.pad(w4, ((0, ffn), (0, 0)))
    w4hi = jnp.pad(w4, ((ffn, 0), (0, 0)))
    decay = np.abs(np.linspace(math.log(HY_TARGET) / HY_SLOW_PCT, math.log(HY_TARGET) / HY_FAST_PCT, c))
    decay = jnp.asarray(decay[None, :], F32)
    row2 = lambda v: jnp.tile(v.reshape(1, -1), (1, 2))
    small = lambda shape: pl.BlockSpec(shape, lambda i: (0, 0))
    mat, vec = small((V7X_LANES, V7X_LANES)), small((1, V7X_LANES))
    w4_spec = pl.BlockSpec((V7X_LANES, c), lambda i: (0, i // per_half))
    return pl.pallas_call(
        functools.partial(_filter_kernel, length=length),
        grid=(2 * per_half,),
        in_specs=[pl.BlockSpec((tr // 2, V7X_LANES), lambda i: (i, 0)),
                  mat, vec, mat, vec, mat, vec, w4_spec, w4_spec, vec, small((1, c))],
        out_specs=pl.BlockSpec((tr, c), lambda i: (i, 0)),
        out_shape=jax.ShapeDtypeStruct((2 * length, c), F32),
        name="hyena_filter",
        compiler_params=_params(("parallel",), 0),
    )(zpos, w1d, row2(b1), twice(w2), row2(b2), twice(w3), row2(b3), w4lo, w4hi, row2(freq), decay)


def _complex_as_real(m):
    return np.block([[m.real, -m.imag], [m.imag, m.real]])


@functools.lru_cache(maxsize=None)
def _dft_tables(length):
    n = 2 * length
    n1 = n // DFT_RADIX
    half = n1 // 2
    eye = np.eye(V7X_SUBLANES)
    k1 = np.arange(n1)
    f1 = np.exp(-2j * np.pi * np.outer(k1, k1) / n1)
    first_data = np.kron(_complex_as_real(f1[:, :half]), eye)
    first_filt = np.kron(np.concatenate([f1.real, f1.imag], axis=0), eye)
    inv1 = np.exp(2j * np.pi * np.outer(np.arange(half), k1) / n1) / n
    last = np.kron(_complex_as_real(inv1), eye)
    i2 = np.arange(DFT_RADIX)
    phase = (np.outer(i2, i2)[None] / DFT_RADIX + (i2[None, None, :] * k1[:, None, None]) / n)
    second = np.exp(-2j * np.pi * phase)
    fwd = np.stack([_complex_as_real(second[k]) for k in range(n1)])
    inv = np.stack([_complex_as_real(np.conj(second[k]).T) for k in range(n1)])
    return tuple(np.asarray(a, np.float32) for a in (first_data, first_filt, last, fwd, inv))


def _pack_pair(re, im):
    re_bits = lax.bitcast_convert_type(re.astype(BF16).astype(F32), jnp.uint32)
    im_bits = lax.bitcast_convert_type(im.astype(BF16).astype(F32), jnp.uint32)
    return (re_bits >> 16) | im_bits


def _unpack_pair(packed):
    re = lax.bitcast_convert_type(packed << 16, F32)
    im = lax.bitcast_convert_type(packed & jnp.uint32(0xFFFF0000), F32)
    return jnp.concatenate([re, im], axis=0).astype(BF16)


def _kron_kernel(mat_ref, x_ref, o_ref):
    lanes = x_ref.shape[-1]
    spec = _mdot(mat_ref[...], x_ref[...].reshape(-1, lanes).astype(BF16))
    rows = spec.shape[0] // 2
    o_ref[...] = _pack_pair(spec[:rows], spec[rows:]).reshape(o_ref.shape)


def _kron_out_kernel(mat_ref, b_ref, x0_ref, u_ref, bias_ref, o_ref):
    lanes = b_ref.shape[-1]
    y = _mdot(mat_ref[...], _unpack_pair(b_ref[...].reshape(-1, lanes))).reshape(o_ref.shape)
    o_ref[...] = x0_ref[...] * (y + u_ref[...] * bias_ref[...])


def _second_stage_kernel(a_ref, af_ref, g_ref, gi_ref, o_ref, kf_ref):
    r = DFT_RADIX
    steps = a_ref.shape[0]

    @pl.when(pl.program_id(2) == 0)
    def _():
        for j in range(steps):
            kf_ref[j] = _mdot(g_ref[j], _unpack_pair(af_ref[j]))

    for j in range(steps):
        spec = _mdot(g_ref[j], _unpack_pair(a_ref[j]))
        xr, xi = spec[:r], spec[r:]
        kr, ki = kf_ref[j, :r], kf_ref[j, r:]
        prod = jnp.concatenate([xr * kr - xi * ki, xr * ki + xi * kr], axis=0).astype(BF16)
        back = _mdot(gi_ref[j], prod)
        o_ref[j] = _pack_pair(back[:r], back[r:])


def _long_conv_gate(u, x0, kfilt, bias, batch, length):
    c = u.shape[1]
    pairs = batch // 2
    n1 = 2 * length // DFT_RADIX
    half = n1 // 2
    r, s8, cb, kb = DFT_RADIX, V7X_SUBLANES, min(DFT_LANES, c), min(DFT_K1_PER_STEP, n1)
    cb2 = min(V7X_MXU_DIM, c)
    assert batch % 2 == 0 and n1 % kb == 0 and c % cb == 0 and c % cb2 == 0
    first_data, first_filt, last, fwd, inv = (jnp.asarray(t, BF16) for t in _dft_tables(length))
    groups = r // s8
    big = 48 * 1024 * 1024

    u5 = u.reshape(pairs, 2, half, r, c)
    x05 = x0.reshape(pairs, 2, half, r, c)
    a_data = pl.pallas_call(
        _kron_kernel,
        grid=(pairs, groups, c // cb),
        in_specs=[_resident(first_data.shape),
                  pl.BlockSpec((None, 2, half, s8, cb), lambda p, g, l: (p, 0, 0, g, l))],
        out_specs=pl.BlockSpec((None, n1, s8, cb), lambda p, g, l: (p, 0, g, l)),
        out_shape=jax.ShapeDtypeStruct((pairs, n1, r, c), jnp.uint32),
        name="dft_first",
        compiler_params=_params(("parallel", "parallel", "parallel"), big),
    )(first_data, u5)

    a_filt = pl.pallas_call(
        _kron_kernel,
        grid=(groups, c // cb),
        in_specs=[_resident(first_filt.shape),
                  pl.BlockSpec((n1, s8, cb), lambda g, l: (0, g, l))],
        out_specs=pl.BlockSpec((n1, s8, cb), lambda g, l: (0, g, l)),
        out_shape=jax.ShapeDtypeStruct((n1, r, c), jnp.uint32),
        name="dft_first_filter",
        compiler_params=_params(("parallel", "parallel"), big),
    )(first_filt, kfilt.reshape(n1, r, c))

    b_data = pl.pallas_call(
        _second_stage_kernel,
        grid=(n1 // kb, c // cb2, pairs),
        in_specs=[pl.BlockSpec((None, kb, r, cb2), lambda k, l, p: (p, k, 0, l)),
                  pl.BlockSpec((kb, r, cb2), lambda k, l, p: (k, 0, l)),
                  pl.BlockSpec((kb, 2 * r, 2 * r), lambda k, l, p: (k, 0, 0)),
                  pl.BlockSpec((kb, 2 * r, 2 * r), lambda k, l, p: (k, 0, 0))],
        out_specs=pl.BlockSpec((None, kb, r, cb2), lambda k, l, p: (p, k, 0, l)),
        out_shape=jax.ShapeDtypeStruct((pairs, n1, r, c), jnp.uint32),
        scratch_shapes=[pltpu.VMEM((kb, 2 * r, cb2), F32)],
        name="dft_second",
        compiler_params=_params(("arbitrary", "arbitrary", "arbitrary"), big),
    )(a_data, a_filt, fwd, inv)

    z = pl.pallas_call(
        _kron_out_kernel,
        grid=(pairs, groups, c // cb),
        in_specs=[_resident(last.shape),
                  pl.BlockSpec((None, n1, s8, cb), lambda p, g, l: (p, 0, g, l)),
                  pl.BlockSpec((None, 2, half, s8, cb), lambda p, g, l: (p, 0, 0, g, l)),
                  pl.BlockSpec((None, 2, half, s8, cb), lambda p, g, l: (p, 0, 0, g, l)),
                  pl.BlockSpec((1, cb), lambda p, g, l: (0, l))],
        out_specs=pl.BlockSpec((None, 2, half, s8, cb), lambda p, g, l: (p, 0, 0, g, l)),
        out_shape=jax.ShapeDtypeStruct((pairs, 2, half, r, c), F32),
        name="dft_last",
        compiler_params=_params(("parallel", "parallel", "parallel"), big),
    )(last, b_data, x05, u5, bias.reshape(1, c))
    return z.reshape(batch * length, c)


@functools.lru_cache(maxsize=None)
def _dense_dft_tables(length):
    n = 2 * length
    idx = np.arange(n)
    f = np.exp(-2j * np.pi * np.outer(idx, idx) / n)
    fwd_data = _complex_as_real(f[:, :length])
    fwd_filt = np.concatenate([f.real, f.imag], axis=0)
    inv = _complex_as_real(np.conj(f[:length, :]) / n)
    return tuple(np.asarray(a, np.float32) for a in (fwd_data, fwd_filt, inv))


def _short_conv_gate_kernel(fd_ref, ff_ref, inv_ref, u_ref, x0_ref, k_ref, bias_ref, o_ref):
    n = k_ref.shape[0]
    u = u_ref[...]
    spec = _mdot(fd_ref[...], u.astype(BF16))
    kf = _mdot(ff_ref[...], k_ref[...].astype(BF16))
    xr, xi, kr, ki = spec[:n], spec[n:], kf[:n], kf[n:]
    prod = jnp.concatenate([xr * kr - xi * ki, xr * ki + xi * kr], axis=0).astype(BF16)
    o_ref[...] = x0_ref[...] * (_mdot(inv_ref[...], prod) + u * bias_ref[...])


def _long_conv_gate_short(u, x0, kfilt, bias, batch, length):
    c = u.shape[1]
    pairs = batch // 2
    cb = min(DFT_LANES, c)
    fwd_data, fwd_filt, inv = (jnp.asarray(t, BF16) for t in _dense_dft_tables(length))
    pair_spec = pl.BlockSpec((None, 2 * length, cb), lambda p, l: (p, 0, l))
    z = pl.pallas_call(
        _short_conv_gate_kernel,
        grid=(pairs, c // cb),
        in_specs=[_resident(fwd_data.shape), _resident(fwd_filt.shape), _resident(inv.shape),
                  pair_spec, pair_spec,
                  pl.BlockSpec((2 * length, cb), lambda p, l: (0, l)),
                  pl.BlockSpec((1, cb), lambda p, l: (0, l))],
        out_specs=pair_spec,
        out_shape=jax.ShapeDtypeStruct((pairs, 2 * length, c), F32),
        name="short_conv_gate",
        compiler_params=_params(("parallel", "parallel"), 0),
    )(fwd_data, fwd_filt, inv, u.reshape(pairs, 2 * length, c), x0.reshape(pairs, 2 * length, c),
      kfilt, bias.reshape(1, c))
    return z.reshape(batch * length, c)


def kernel(x, c, ctx, c_ctx, w_mod, b_mod, norm_g, ffn_w_gate, ffn_w_up, ffn_w_down, ab_w_in, na_rpb,
           sc_conv_w, ab_w_out, hy_w_in, hy_short_w, hy_f_w1, hy_f_b1, hy_f_w2, hy_f_b2, hy_f_w3, hy_f_b3,
           hy_f_w4, hy_sin_freq, hy_bias, hy_w_out):
    batch, seq, d = x.shape
    ctx_len = ctx.shape[1]
    depth = w_mod.shape[0]
    assert batch + 1 <= V7X_SUBLANES and seq % ROW_TILE == 0
    last_attn = (depth - 1) - (depth - 1) % 2
    sc_width = sc_conv_w.shape[-1]

    cc = jnp.zeros((V7X_SUBLANES, d), F32).at[:batch].set(c).at[batch].set(c_ctx)
    mod = _modulation(cc, w_mod, b_mod).reshape(depth, V7X_SUBLANES, N_MOD, d)
    lat_batch = lambda i, tm: i // (seq // tm)
    ctx_batch = lambda i, tm: batch

    xs = x.reshape(batch * seq, d)
    cs = ctx.reshape(batch * ctx_len, d)
    ffn_w = (ffn_w_gate.astype(BF16), ffn_w_up.astype(BF16), ffn_w_down.astype(BF16))

    for i in range(depth):
        j = i // 2
        g = norm_g[i]
        m_i = mod[i]
        keep_ctx = i <= last_attn
        upd_ctx = i < last_attn

        xs = _ffn(xs, m_i, g, *ffn_w, (i, 0), mrow=0, grow=0, batch_of=lat_batch)
        if keep_ctx:
            cs = _ffn(cs, m_i, g, *ffn_w, (i, 0), mrow=0, grow=0, batch_of=ctx_batch)

        if i % 2 == 0:
            w_in = ab_w_in[j].astype(BF16)
            w_out = ab_w_out[j].astype(BF16)
            ab_outs = [(2 * NA_WIDTH, BF16, False), (NA_WIDTH, BF16, True), (sc_width, BF16, False)]
            w_vt = w_in[:, 2 * NA_WIDTH:3 * NA_WIDTH].T
            in_ab = functools.partial(_inproj_call, _inproj_ab_kernel, "inproj_ab", mrow=3, grow=2, extra=(w_vt,))
            qk, vt, b_lat = in_ab(xs, m_i, g, w_in, sc_conv_w[j], ab_outs, seq_len=seq, batch_of=lat_batch)
            if upd_ctx:
                qk_c, vt_c, b_ctx = in_ab(cs, m_i, g, w_in, sc_conv_w[j], ab_outs, seq_len=ctx_len,
                                          batch_of=ctx_batch)
            else:
                qk_c, vt_c = in_ab(cs, m_i, g, w_in, sc_conv_w[j], ab_outs[:2], seq_len=ctx_len,
                                   batch_of=ctx_batch)
            a_lat = _neighbourhood_attention(qk, vt, qk_c, vt_c, na_rpb[j], batch, seq, ctx_len)
            mix_lat = ([a_lat, b_lat], w_out)
            if upd_ctx:
                mix_ctx = ([_context_attention(qk_c, vt_c, batch, ctx_len), b_ctx], w_out)
        else:
            w_in = hy_w_in[j].astype(BF16)
            w_out = hy_w_out[j].astype(BF16)
            hw = w_out.shape[0]
            filt = lambda n: _hyena_filter(n, hy_f_w1[j], hy_f_b1[j], hy_f_w2[j], hy_f_b2[j], hy_f_w3[j],
                                           hy_f_b3[j], hy_f_w4[j], hy_sin_freq[j])
            in_hy = functools.partial(_inproj_call, _inproj_hyena_kernel, "inproj_hyena", mrow=3, grow=2)
            x0, u = in_hy(xs, m_i, g, w_in, hy_short_w[j], [(hw, F32, False)] * 2, seq_len=seq,
                          batch_of=lat_batch)
            mix_lat = ([_long_conv_gate(u, x0, filt(seq), hy_bias[j], batch, seq)], w_out)
            if upd_ctx:
                x0_c, u_c = in_hy(cs, m_i, g, w_in, hy_short_w[j], [(hw, F32, False)] * 2, seq_len=ctx_len,
                                  batch_of=ctx_batch)
                mix_ctx = ([_long_conv_gate_short(u_c, x0_c, filt(ctx_len), hy_bias[j], batch, ctx_len)], w_out)

        xs = _ffn(xs, m_i, g, *ffn_w, (i, 1), mrow=6, grow=4, batch_of=lat_batch, mixer=mix_lat,
                  mix_mrow=5, mix_grow=3)
        if upd_ctx:
            cs = _ffn(cs, m_i, g, *ffn_w, (i, 1), mrow=6, grow=4, batch_of=ctx_batch, mixer=mix_ctx,
                      mix_mrow=5, mix_grow=3)

    return xs.reshape(batch, seq, d)
```

```python
import functools
import math

import numpy as np
import jax
import jax.numpy as jnp
from jax import lax
from jax.experimental import pallas as pl
from jax.experimental.pallas import tpu as pltpu

F32 = jnp.float32
BF16 = jnp.bfloat16

GRID_W = 64
NA_HEADS = 8
NA_HEAD_DIM = 64
NA_WIDTH = NA_HEADS * NA_HEAD_DIM
NA_KH = 8
NA_KW = 16
FFN_RES = 0.5
N_MOD = 9
RMS_EPS = 1e-6
HY_EMB = 33
HY_BANDS = (HY_EMB - 1) // 2
HY_TARGET = 1e-2
HY_FAST_PCT = 0.3
HY_SLOW_PCT = 1.5

V7X_LANES = 128
V7X_SUBLANES = 8
V7X_MXU_DIM = 256
V7X_VMEM_BYTES = 64 * 1024 * 1024

ROW_TILE = 512
FFN_CHUNK = 256
FFN_ROW_TILE = 1024
FFN_SUBBLOCKS = 2
PROJ_CHUNK = 512
NA_ROWS_PER_STEP = 32
NA_SLAB_ROWS = NA_KH + 2
DFT_RADIX = 128
DFT_LANES = 1024
DFT_K1_PER_STEP = 16
FILTER_ROWS = 512
MASK_VALUE = -1e30


def _vmem_limit(nbytes):
    return int(min(max(nbytes, 32 * 1024 * 1024), V7X_VMEM_BYTES - 8 * 1024 * 1024))


def _params(semantics, vmem_bytes):
    return pltpu.CompilerParams(dimension_semantics=semantics, vmem_limit_bytes=_vmem_limit(vmem_bytes))


def _resident(shape):
    zeros = (0,) * len(shape)
    return pl.BlockSpec(shape, lambda *_: zeros, pipeline_mode=pl.Buffered(1))


def _rms(x):
    return x * lax.rsqrt(jnp.mean(x * x, axis=-1, keepdims=True) + RMS_EPS)


def _mdot(a, b):
    return jnp.dot(a, b, preferred_element_type=F32)


def _mod_kernel(c_ref, w_ref, b_ref, o_ref):
    c = c_ref[...]
    s = (c * jax.nn.sigmoid(c)).astype(BF16)
    o_ref[...] = _mdot(s, w_ref[...].astype(BF16)) + b_ref[...]


def _modulation(cc, w_mod, b_mod):
    depth, d, nd = w_mod.shape
    return pl.pallas_call(
        _mod_kernel,
        grid=(depth, nd // d),
        in_specs=[pl.BlockSpec((V7X_SUBLANES, d), lambda i, j: (0, 0)),
                  pl.BlockSpec((None, d, d), lambda i, j: (i, 0, j)),
                  pl.BlockSpec((None, 1, d), lambda i, j: (i, 0, j))],
        out_specs=pl.BlockSpec((None, V7X_SUBLANES, d), lambda i, j: (i, 0, j)),
        out_shape=jax.ShapeDtypeStruct((depth, V7X_SUBLANES, nd), F32),
        name="modulation",
        compiler_params=_params(("parallel", "parallel"), 4 * d * d * 4),
    )(cc, w_mod, b_mod.reshape(depth, 1, nd))


def _ffn_kernel(*refs, n_parts, mix_mrow, mix_grow, mrow, grow):
    part_refs = refs[:n_parts]
    w_out_ref = refs[n_parts] if n_parts else None
    x_ref, m_ref, g_ref, wg_ref, wu_ref, wd_ref, o_ref, acc_ref = refs[n_parts + bool(n_parts):]
    shift, scale, gate = m_ref[mrow:mrow + 1, :], m_ref[mrow + 1:mrow + 2, :], m_ref[mrow + 2:mrow + 3, :]
    pre_gain = g_ref[grow:grow + 1, :] * (1.0 + scale)
    post_gain = (FFN_RES * gate) * g_ref[grow + 1:grow + 2, :]
    hm = x_ref.shape[0] // FFN_SUBBLOCKS
    for s in range(FFN_SUBBLOCKS):
        rows = slice(s * hm, (s + 1) * hm)
        x = x_ref[rows, :]
        if n_parts:
            y, r0 = None, 0
            for p_ref in part_refs:
                r1 = r0 + p_ref.shape[1]
                term = _mdot(p_ref[rows, :].astype(BF16), w_out_ref[r0:r1, :])
                y = term if y is None else y + term
                r0 = r1
            x = x + m_ref[mix_mrow:mix_mrow + 1, :] * (_rms(y) * g_ref[mix_grow:mix_grow + 1, :])
        h = (_rms(x) * pre_gain + shift).astype(BF16)
        for j in range(wg_ref.shape[1] // FFN_CHUNK):
            c0, c1 = j * FFN_CHUNK, (j + 1) * FFN_CHUNK
            g = _mdot(h, wg_ref[:, c0:c1])
            u = _mdot(h, wu_ref[:, c0:c1])
            a = (g * jax.nn.sigmoid(g) * u).astype(BF16)
            part = _mdot(a, wd_ref[c0:c1, :])
            if j == 0:
                acc_ref[rows, :] = part
            else:
                acc_ref[rows, :] += part
        o_ref[rows, :] = x + _rms(acc_ref[rows, :]) * post_gain


def _layer_weight(w, index):
    lead = len(index)
    return pl.BlockSpec((None,) * lead + w.shape[lead:], lambda *_: tuple(index) + (0, 0),
                        pipeline_mode=pl.Buffered(1))


def _ffn(x, mod_i, g, wg, wu, wd, layer, *, mrow, grow, batch_of, mixer=None, mix_mrow=None, mix_grow=None):
    m, d = x.shape
    f = wg.shape[-1]
    tm = min(FFN_ROW_TILE, m)
    hm = tm // FFN_SUBBLOCKS
    parts, w_out = mixer if mixer else ((), None)
    kern = functools.partial(_ffn_kernel, n_parts=len(parts), mix_mrow=mix_mrow, mix_grow=mix_grow,
                             mrow=mrow, grow=grow)
    part_bytes = sum(p.shape[1] * p.dtype.itemsize for p in parts)
    vmem = (3 * d * f * 2 + 4 * tm * d * 4 + tm * d * 4 + 6 * hm * FFN_CHUNK * 4 + hm * d * 12
            + 2 * tm * part_bytes + (d * d * 2 if parts else 0))
    mix_specs = [pl.BlockSpec((tm, p.shape[1]), lambda i: (i, 0)) for p in parts]
    if parts:
        mix_specs.append(_resident(w_out.shape))
    return pl.pallas_call(
        kern,
        grid=(m // tm,),
        in_specs=mix_specs + [
            pl.BlockSpec((tm, d), lambda i: (i, 0)),
            pl.BlockSpec((None, N_MOD, d), lambda i: (batch_of(i, tm), 0, 0)),
            _resident(g.shape), _layer_weight(wg, layer), _layer_weight(wu, layer), _layer_weight(wd, layer)],
        out_specs=pl.BlockSpec((tm, d), lambda i: (i, 0)),
        out_shape=jax.ShapeDtypeStruct((m, d), F32),
        scratch_shapes=[pltpu.VMEM((tm, d), F32)],
        name="ffn_mix" if parts else "ffn",
        compiler_params=_params(("parallel",), vmem),
    )(*parts, *([w_out] if parts else []), x, mod_i, g, wg, wu, wd)


def _conv3_rows(pe, w0, w1, w2, tm):
    n = pe.shape[0]
    lo = V7X_SUBLANES
    down = pltpu.roll(pe, 1, 0)[lo:lo + tm]
    up = pltpu.roll(pe, n - 1, 0)[lo:lo + tm]
    return down * w0 + pe[lo:lo + tm] * w1 + up * w2


def _normed_with_halo(x_ref, prev_ref, next_ref, m_ref, g_ref, mrow, grow, seq_len):
    tm = x_ref.shape[0]
    lo = V7X_SUBLANES
    shift, scale = m_ref[mrow:mrow + 1, :], m_ref[mrow + 1:mrow + 2, :]
    xe = jnp.concatenate([prev_ref[...], x_ref[...], next_ref[...]], axis=0)
    he = (_rms(xe) * g_ref[grow:grow + 1, :]) * (1.0 + scale) + shift
    start = (pl.program_id(0) * tm) & (seq_len - 1)
    keep_prev = (start != 0).astype(F32)
    keep_next = (start + tm != seq_len).astype(F32)
    row = lax.broadcasted_iota(jnp.int32, (tm + 2 * lo, 1), 0)
    keep = jnp.where(row < lo, keep_prev, jnp.where(row >= lo + tm, keep_next, 1.0))
    return he[lo:lo + tm].astype(BF16), (he * keep).astype(BF16)


def _inproj_ab_kernel(x_ref, prev_ref, next_ref, m_ref, g_ref, w_ref, cw_ref, wvt_ref, qk_ref, vt_ref, *b_refs,
                      mrow, grow, seq_len):
    tm = x_ref.shape[0]
    h, he = _normed_with_halo(x_ref, prev_ref, next_ref, m_ref, g_ref, mrow, grow, seq_len)
    nqk = qk_ref.shape[1]
    nq = nqk + vt_ref.shape[0]
    for c in range(0, nqk, PROJ_CHUNK):
        qk_ref[:, c:c + PROJ_CHUNK] = _mdot(h, w_ref[:, c:c + PROJ_CHUNK]).astype(qk_ref.dtype)
    vt_ref[...] = lax.dot_general(wvt_ref[...], h, (((1,), (1,)), ((), ())),
                                  preferred_element_type=F32).astype(vt_ref.dtype)
    if b_refs:
        (b_ref,) = b_refs
        sw = cw_ref.shape[1]
        gate = _mdot(h, w_ref[:, nq:nq + sw])
        p = _mdot(he, w_ref[:, nq + sw:nq + 2 * sw]) * _mdot(he, w_ref[:, nq + 2 * sw:nq + 3 * sw])
        conv = _conv3_rows(p, cw_ref[0:1, :], cw_ref[1:2, :], cw_ref[2:3, :], tm)
        b_ref[...] = (gate * conv).astype(b_ref.dtype)


def _inproj_hyena_kernel(x_ref, prev_ref, next_ref, m_ref, g_ref, w_ref, cw_ref, x0_ref, u_ref,
                         *, mrow, grow, seq_len):
    tm = x_ref.shape[0]
    _, he = _normed_with_halo(x_ref, prev_ref, next_ref, m_ref, g_ref, mrow, grow, seq_len)
    c = x0_ref.shape[1]

    def conv_cols(c0):
        cols = slice(c0, c0 + PROJ_CHUNK)
        return _conv3_rows(_mdot(he, w_ref[:, cols]), cw_ref[0:1, cols], cw_ref[1:2, cols], cw_ref[2:3, cols], tm)

    for c0 in range(0, c, PROJ_CHUNK):
        x0_ref[:, c0:c0 + PROJ_CHUNK] = conv_cols(c0)
        u_ref[:, c0:c0 + PROJ_CHUNK] = conv_cols(2 * c + c0) * conv_cols(c + c0)


def _halo_specs(tm, m, width):
    blocks = m // V7X_SUBLANES
    per = tm // V7X_SUBLANES
    prev = pl.BlockSpec((V7X_SUBLANES, width), lambda i: (jnp.maximum(i * per - 1, 0), 0))
    nxt = pl.BlockSpec((V7X_SUBLANES, width), lambda i: (jnp.minimum((i + 1) * per, blocks - 1), 0))
    return prev, nxt


def _inproj_call(kern, name, x, mod_i, g, w, conv_w, outs, *, mrow, grow, seq_len, batch_of, extra=()):
    m, d = x.shape
    tm = min(ROW_TILE, seq_len)
    assert seq_len & (seq_len - 1) == 0 and seq_len % tm == 0 and m % seq_len == 0
    prev, nxt = _halo_specs(tm, m, d)
    out_bytes = sum(cols * jnp.dtype(dt).itemsize for cols, dt, _ in outs)
    vmem = w.size * 2 + 2 * tm * d * 4 + 2 * tm * out_bytes + 6 * tm * d * 4
    out_specs = [pl.BlockSpec((cols, tm), lambda i: (0, i)) if tr else pl.BlockSpec((tm, cols), lambda i: (i, 0))
                 for cols, _, tr in outs]
    out_shape = [jax.ShapeDtypeStruct((cols, m) if tr else (m, cols), dt) for cols, dt, tr in outs]
    return pl.pallas_call(
        functools.partial(kern, mrow=mrow, grow=grow, seq_len=seq_len),
        grid=(m // tm,),
        in_specs=[pl.BlockSpec((tm, d), lambda i: (i, 0)), prev, nxt,
                  pl.BlockSpec((None, N_MOD, d), lambda i: (batch_of(i, tm), 0, 0)),
                  _resident(g.shape), _resident(w.shape), _resident(conv_w.shape)]
        + [_resident(e.shape) for e in extra],
        out_specs=out_specs,
        out_shape=out_shape,
        name=name,
        compiler_params=_params(("parallel",), vmem),
    )(x, x, x, mod_i, g, w, conv_w, *extra)


def _split_heads_on_rows(q, low_lanes):
    zero = jnp.zeros_like(q)
    return jnp.concatenate([jnp.where(low_lanes, q, zero), jnp.where(low_lanes, zero, q)], axis=0)


def _na_kernel(pat_ref, q_ref, k_ref, vt_ref, kc_ref, vct_ref, bt_ref, o_ref, *, rows):
    rb = pl.program_id(2)
    step_rows = q_ref.shape[0] // GRID_W
    low_lanes = lax.broadcasted_iota(jnp.int32, (GRID_W, V7X_LANES), 1) < NA_HEAD_DIM
    kc = kc_ref[...]
    vct = vct_ref[...]
    nt = (((1,), (1,)), ((), ()))

    for t in range(step_rows // 2):
        r = rb * step_rows + 2 * t
        base = jnp.clip(r - NA_KH // 2, 0, rows - NA_SLAB_ROWS)
        k0 = pl.multiple_of(base * GRID_W, 2 * GRID_W)
        qa = q_ref[2 * t * GRID_W:(2 * t + 1) * GRID_W, :] * (NA_HEAD_DIM ** -0.5)
        qb = q_ref[(2 * t + 1) * GRID_W:(2 * t + 2) * GRID_W, :] * (NA_HEAD_DIM ** -0.5)
        q4 = jnp.concatenate([_split_heads_on_rows(qa, low_lanes), _split_heads_on_rows(qb, low_lanes)], axis=0)
        kk = k_ref[pl.ds(k0, NA_SLAB_ROWS * GRID_W), :]
        s = lax.dot_general(kk, q4, nt, preferred_element_type=F32) + bt_ref[pat_ref[r // 2]]
        sc = lax.dot_general(kc, q4, nt, preferred_element_type=F32)
        mx = jnp.maximum(jnp.max(s, axis=0, keepdims=True), jnp.max(sc, axis=0, keepdims=True))
        p = jnp.exp(s - mx)
        pc = jnp.exp(sc - mx)
        den = jnp.sum(p, axis=0, keepdims=True) + jnp.sum(pc, axis=0, keepdims=True)
        vv = vt_ref[:, pl.ds(k0, NA_SLAB_ROWS * GRID_W)]
        ot = (_mdot(vv, p.astype(BF16)) + _mdot(vct, pc.astype(BF16))) / den
        o4 = ot.T
        oa = jnp.where(low_lanes, o4[0:GRID_W], o4[GRID_W:2 * GRID_W])
        ob = jnp.where(low_lanes, o4[2 * GRID_W:3 * GRID_W], o4[3 * GRID_W:4 * GRID_W])
        o_ref[2 * t * GRID_W:(2 * t + 2) * GRID_W, :] = jnp.concatenate([oa, ob], axis=0).astype(o_ref.dtype)


def _na_window_row(r, rows):
    return min(max(r - NA_KH // 2, 0), rows - NA_KH)


def _na_bias_table(rpb, rows):
    h = rpb.shape[0]
    c = np.arange(GRID_W)[:, None]
    kc = np.arange(GRID_W)[None, :]
    start = np.clip(c - NA_KW // 2, 0, GRID_W - NA_KW)
    valid = (kc >= start) & (kc < start + NA_KW)
    onehot = ((kc - c + (NA_KW - 1))[:, :, None] == np.arange(2 * NA_KW - 1)) & valid[:, :, None]
    dense = jnp.einsum("hdx,ckx->hdkc", rpb, jnp.asarray(onehot, F32), precision=lax.Precision.HIGHEST)
    dense = dense + jnp.asarray(np.where(valid.T, 0.0, MASK_VALUE), F32)
    masked = jnp.full((h, GRID_W, GRID_W), MASK_VALUE, F32)
    patterns, ids = [], []
    for r in range(0, rows, 2):
        base = min(max(r - NA_KH // 2, 0), rows - NA_SLAB_ROWS)
        key = tuple((_na_window_row(r + s, rows) - base, _na_window_row(r + s, rows) - (r + s) + NA_KH - 1)
                    for s in (0, 1))
        if key not in patterns:
            patterns.append(key)
        ids.append(patterns.index(key))
    tabs = []
    for key in patterns:
        per_row = []
        for off, dy0 in key:
            blocks = [dense[:, dy0 + y - off] if off <= y < off + NA_KH else masked for y in range(NA_SLAB_ROWS)]
            per_row.append(jnp.concatenate(blocks, axis=1))
        t = jnp.stack(per_row, axis=1).reshape(h // 2, 2, 2, NA_SLAB_ROWS * GRID_W, GRID_W)
        tabs.append(t.transpose(0, 3, 2, 1, 4).reshape(h // 2, NA_SLAB_ROWS * GRID_W, 4 * GRID_W))
    return jnp.stack(tabs, axis=1), np.asarray(ids, np.int32)


def _neighbourhood_attention(qk, vt, qk_c, vt_c, rpb, batch, seq, ctx_len):
    rows = seq // GRID_W
    step_rows = min(NA_ROWS_PER_STEP, rows)
    assert rows >= NA_SLAB_ROWS and rows % step_rows == 0 and step_rows % 2 == 0
    nq = NA_WIDTH // V7X_LANES
    qk3 = qk.reshape(batch, seq, 2 * NA_WIDTH)
    qkc3 = qk_c.reshape(batch, ctx_len, 2 * NA_WIDTH)
    tq = step_rows * GRID_W
    table, pattern_ids = _na_bias_table(rpb, rows)
    vmem = 4 * seq * V7X_LANES * 2 + 2 * table[0].size * 4 + 16 * 1024 * 1024
    out = pl.pallas_call(
        functools.partial(_na_kernel, rows=rows),
        grid=(batch, nq, rows // step_rows),
        in_specs=[pl.BlockSpec(memory_space=pltpu.SMEM),
                  pl.BlockSpec((None, tq, V7X_LANES), lambda b, h, r: (b, r, h)),
                  pl.BlockSpec((None, seq, V7X_LANES), lambda b, h, r: (b, 0, nq + h)),
                  pl.BlockSpec((V7X_LANES, seq), lambda b, h, r: (h, b)),
                  pl.BlockSpec((None, ctx_len, V7X_LANES), lambda b, h, r: (b, 0, nq + h)),
                  pl.BlockSpec((V7X_LANES, ctx_len), lambda b, h, r: (h, b)),
                  pl.BlockSpec((None,) + table.shape[1:], lambda b, h, r: (h, 0, 0, 0))],
        out_specs=pl.BlockSpec((None, tq, V7X_LANES), lambda b, h, r: (b, r, h)),
        out_shape=jax.ShapeDtypeStruct((batch, seq, NA_WIDTH), BF16),
        name="na_attn",
        compiler_params=_params(("parallel", "parallel", "arbitrary"), vmem),
    )(jnp.asarray(pattern_ids), qk3, qk3, vt, qkc3, vt_c, table)
    return out.reshape(batch * seq, NA_WIDTH)


def _ctx_attn_kernel(q_ref, k_ref, vt_ref, o_ref):
    n = q_ref.shape[0]
    nt = (((1,), (1,)), ((), ()))
    low_lanes = lax.broadcasted_iota(jnp.int32, (n, V7X_LANES), 1) < NA_HEAD_DIM
    q2 = _split_heads_on_rows(q_ref[...] * (NA_HEAD_DIM ** -0.5), low_lanes)
    s = lax.dot_general(q2, k_ref[...], nt, preferred_element_type=F32)
    p = jnp.exp(s - jnp.max(s, axis=-1, keepdims=True))
    o2 = lax.dot_general(p.astype(BF16), vt_ref[...], nt, preferred_element_type=F32)
    o2 = o2 / jnp.sum(p, axis=-1, keepdims=True)
    o_ref[...] = jnp.where(low_lanes, o2[:n], o2[n:]).astype(o_ref.dtype)


def _context_attention(qk_c, vt_c, batch, ctx_len):
    nq = NA_WIDTH // V7X_LANES
    qkc3 = qk_c.reshape(batch, ctx_len, 2 * NA_WIDTH)
    out = pl.pallas_call(
        _ctx_attn_kernel,
        grid=(batch, nq),
        in_specs=[pl.BlockSpec((None, ctx_len, V7X_LANES), lambda b, h: (b, 0, h)),
                  pl.BlockSpec((None, ctx_len, V7X_LANES), lambda b, h: (b, 0, nq + h)),
                  pl.BlockSpec((V7X_LANES, ctx_len), lambda b, h: (h, b))],
        out_specs=pl.BlockSpec((None, ctx_len, V7X_LANES), lambda b, h: (b, 0, h)),
        out_shape=jax.ShapeDtypeStruct((batch, ctx_len, NA_WIDTH), BF16),
        name="ctx_attn",
        compiler_params=_params(("parallel", "parallel"), 0),
    )(qkc3, qkc3, vt_c)
    return out.reshape(batch * ctx_len, NA_WIDTH)


def _filter_kernel(z_ref, w1_ref, b1_ref, w2_ref, b2_ref, w3_ref, b3_ref, w4lo_ref, w4hi_ref, fr_ref, dec_ref,
                   o_ref, *, length):
    hi = lax.Precision.HIGHEST
    half = V7X_LANES // 2
    z = z_ref[...]
    fr = fr_ref[...]
    h = jnp.sin(fr * (jnp.dot(z, w1_ref[...], precision=hi, preferred_element_type=F32) + b1_ref[...]))
    h = jnp.sin(fr * (jnp.dot(h, w2_ref[...], precision=hi, preferred_element_type=F32) + b2_ref[...]))
    h = jnp.sin(fr * (jnp.dot(h, w3_ref[...], precision=hi, preferred_element_type=F32) + b3_ref[...]))
    hb = h.astype(BF16)
    n = z.shape[0]
    first = pl.program_id(0) * 2 * n
    for part, (w4_ref, t_lane) in enumerate(((w4lo_ref, 0), (w4hi_ref, half))):
        out = _mdot(hb, w4_ref[...].astype(BF16)) * jnp.exp(-z[:, t_lane:t_lane + 1] * dec_ref[...])
        row = first + part * n + lax.broadcasted_iota(jnp.int32, out.shape, 0)
        o_ref[part * n:(part + 1) * n, :] = jnp.where(row == length, 0.0, out)


@functools.lru_cache(maxsize=None)
def _filter_positions(length, block):
    half = V7X_LANES // 2
    t = np.linspace(0.0, 1.0, length)[:, None]
    w = 2.0 * math.pi * np.arange(length)[:, None] / length
    bands = np.linspace(1e-4, HY_BANDS - 1, HY_BANDS)[None, :]
    z = np.concatenate([t, np.cos(bands * w), -np.sin(bands * w)], axis=-1)
    z2 = np.concatenate([z, z[0:1], z[:0:-1]], axis=0)
    feat = np.zeros((2 * length, half), np.float32)
    feat[:, :HY_EMB] = z2
    feat = feat.reshape(2 * length // block, 2, block // 2, half)
    return np.ascontiguousarray(feat.transpose(0, 2, 1, 3)).reshape(length, V7X_LANES)


def _hyena_filter(length, w1, b1, w2, b2, w3, b3, w4, freq):
    c = w4.shape[1] // 2
    ffn = w2.shape[0]
    half = V7X_LANES // 2
    assert ffn == half and w1.shape[0] <= half
    tr = min(FILTER_ROWS, length)
    per_half = length // tr
    zpos = jnp.asarray(_filter_positions(length, tr))
    twice = lambda m: jnp.zeros((2 * m.shape[0], 2 * m.shape[1]), F32).at[:m.shape[0], :m.shape[1]].set(m).at[
        m.shape[0]:, m.shape[1]:].set(m)
    w1d = twice(jnp.pad(w1, ((0, half - w1.shape[0]), (0, 0))))
    w4lo = jnp.pad(w4, ((0, ffn), (0, 0)))
    w4hi = jnp.pad(w4, ((ffn, 0), (0, 0)))
    decay = np.abs(np.linspace(math.log(HY_TARGET) / HY_SLOW_PCT, math.log(HY_TARGET) / HY_FAST_PCT, c))
    decay = jnp.asarray(decay[None, :], F32)
    row2 = lambda v: jnp.tile(v.reshape(1, -1), (1, 2))
    small = lambda shape: pl.BlockSpec(shape, lambda i: (0, 0))
    mat, vec = small((V7X_LANES, V7X_LANES)), small((1, V7X_LANES))
    w4_spec = pl.BlockSpec((V7X_LANES, c), lambda i: (0, i // per_half))
    return pl.pallas_call(
        functools.partial(_filter_kernel, length=length),
        grid=(2 * per_half,),
        in_specs=[pl.BlockSpec((tr // 2, V7X_LANES), lambda i: (i, 0)),
                  mat, vec, mat, vec, mat, vec, w4_spec, w4_spec, vec, small((1, c))],
        out_specs=pl.BlockSpec((tr, c), lambda i: (i, 0)),
        out_shape=jax.ShapeDtypeStruct((2 * length, c), F32),
        name="hyena_filter",
        compiler_params=_params(("parallel",), 0),
    )(zpos, w1d, row2(b1), twice(w2), row2(b2), twice(w3), row2(b3), w4lo, w4hi, row2(freq), decay)


def _complex_as_real(m):
    return np.block([[m.real, -m.imag], [m.imag, m.real]])


@functools.lru_cache(maxsize=None)
def _dft_tables(length):
    n = 2 * length
    n1 = n // DFT_RADIX
    half = n1 // 2
    eye = np.eye(V7X_SUBLANES)
    k1 = np.arange(n1)
    f1 = np.exp(-2j * np.pi * np.outer(k1, k1) / n1)
    first_data = np.kron(_complex_as_real(f1[:, :half]), eye)
    first_filt = np.kron(np.concatenate([f1.real, f1.imag], axis=0), eye)
    inv1 = np.exp(2j * np.pi * np.outer(np.arange(half), k1) / n1) / n
    last = np.kron(_complex_as_real(inv1), eye)
    i2 = np.arange(DFT_RADIX)
    phase = (np.outer(i2, i2)[None] / DFT_RADIX + (i2[None, None, :] * k1[:, None, None]) / n)
    second = np.exp(-2j * np.pi * phase)
    fwd = np.stack([_complex_as_real(second[k]) for k in range(n1)])
    inv = np.stack([_complex_as_real(np.conj(second[k]).T) for k in range(n1)])
    return tuple(np.asarray(a, np.float32) for a in (first_data, first_filt, last, fwd, inv))


def _pack_pair(re, im):
    re_bits = lax.bitcast_convert_type(re.astype(BF16).astype(F32), jnp.uint32)
    im_bits = lax.bitcast_convert_type(im.astype(BF16).astype(F32), jnp.uint32)
    return (re_bits >> 16) | im_bits


def _unpack_pair(packed):
    re = lax.bitcast_convert_type(packed << 16, F32)
    im = lax.bitcast_convert_type(packed & jnp.uint32(0xFFFF0000), F32)
    return jnp.concatenate([re, im], axis=0).astype(BF16)


def _kron_kernel(mat_ref, x_ref, o_ref):
    lanes = x_ref.shape[-1]
    spec = _mdot(mat_ref[...], x_ref[...].reshape(-1, lanes).astype(BF16))
    rows = spec.shape[0] // 2
    o_ref[...] = _pack_pair(spec[:rows], spec[rows:]).reshape(o_ref.shape)


def _kron_out_kernel(mat_ref, b_ref, x0_ref, u_ref, bias_ref, o_ref):
    lanes = b_ref.shape[-1]
    y = _mdot(mat_ref[...], _unpack_pair(b_ref[...].reshape(-1, lanes))).reshape(o_ref.shape)
    o_ref[...] = x0_ref[...] * (y + u_ref[...] * bias_ref[...])


def _second_stage_kernel(a_ref, af_ref, g_ref, gi_ref, o_ref, kf_ref):
    r = DFT_RADIX
    steps = a_ref.shape[0]

    @pl.when(pl.program_id(2) == 0)
    def _():
        for j in range(steps):
            kf_ref[j] = _mdot(g_ref[j], _unpack_pair(af_ref[j]))

    for j in range(steps):
        spec = _mdot(g_ref[j], _unpack_pair(a_ref[j]))
        xr, xi = spec[:r], spec[r:]
        kr, ki = kf_ref[j, :r], kf_ref[j, r:]
        prod = jnp.concatenate([xr * kr - xi * ki, xr * ki + xi * kr], axis=0).astype(BF16)
        back = _mdot(gi_ref[j], prod)
        o_ref[j] = _pack_pair(back[:r], back[r:])


def _long_conv_gate(u, x0, kfilt, bias, batch, length):
    c = u.shape[1]
    pairs = batch // 2
    n1 = 2 * length // DFT_RADIX
    half = n1 // 2
    r, s8, cb, kb = DFT_RADIX, V7X_SUBLANES, min(DFT_LANES, c), min(DFT_K1_PER_STEP, n1)
    cb2 = min(V7X_MXU_DIM, c)
    assert batch % 2 == 0 and n1 % kb == 0 and c % cb == 0 and c % cb2 == 0
    first_data, first_filt, last, fwd, inv = (jnp.asarray(t, BF16) for t in _dft_tables(length))
    groups = r // s8
    big = 48 * 1024 * 1024

    u5 = u.reshape(pairs, 2, half, r, c)
    x05 = x0.reshape(pairs, 2, half, r, c)
    a_data = pl.pallas_call(
        _kron_kernel,
        grid=(pairs, groups, c // cb),
        in_specs=[_resident(first_data.shape),
                  pl.BlockSpec((None, 2, half, s8, cb), lambda p, g, l: (p, 0, 0, g, l))],
        out_specs=pl.BlockSpec((None, n1, s8, cb), lambda p, g, l: (p, 0, g, l)),
        out_shape=jax.ShapeDtypeStruct((pairs, n1, r, c), jnp.uint32),
        name="dft_first",
        compiler_params=_params(("parallel", "parallel", "parallel"), big),
    )(first_data, u5)

    a_filt = pl.pallas_call(
        _kron_kernel,
        grid=(groups, c // cb),
        in_specs=[_resident(first_filt.shape),
                  pl.BlockSpec((n1, s8, cb), lambda g, l: (0, g, l))],
        out_specs=pl.BlockSpec((n1, s8, cb), lambda g, l: (0, g, l)),
        out_shape=jax.ShapeDtypeStruct((n1, r, c), jnp.uint32),
        name="dft_first_filter",
        compiler_params=_params(("parallel", "parallel"), big),
    )(first_filt, kfilt.reshape(n1, r, c))

    b_data = pl.pallas_call(
        _second_stage_kernel,
        grid=(n1 // kb, c // cb2, pairs),
        in_specs=[pl.BlockSpec((None, kb, r, cb2), lambda k, l, p: (p, k, 0, l)),
                  pl.BlockSpec((kb, r, cb2), lambda k, l, p: (k, 0, l)),
                  pl.BlockSpec((kb, 2 * r, 2 * r), lambda k, l, p: (k, 0, 0)),
                  pl.BlockSpec((kb, 2 * r, 2 * r), lambda k, l, p: (k, 0, 0))],
        out_specs=pl.BlockSpec((None, kb, r, cb2), lambda k, l, p: (p, k, 0, l)),
        out_shape=jax.ShapeDtypeStruct((pairs, n1, r, c), jnp.uint32),
        scratch_shapes=[pltpu.VMEM((kb, 2 * r, cb2), F32)],
        name="dft_second",
        compiler_params=_params(("arbitrary", "arbitrary", "arbitrary"), big),
    )(a_data, a_filt, fwd, inv)

    z = pl.pallas_call(
        _kron_out_kernel,
        grid=(pairs, groups, c // cb),
        in_specs=[_resident(last.shape),
                  pl.BlockSpec((None, n1, s8, cb), lambda p, g, l: (p, 0, g, l)),
                  pl.BlockSpec((None, 2, half, s8, cb), lambda p, g, l: (p, 0, 0, g, l)),
                  pl.BlockSpec((None, 2, half, s8, cb), lambda p, g, l: (p, 0, 0, g, l)),
                  pl.BlockSpec((1, cb), lambda p, g, l: (0, l))],
        out_specs=pl.BlockSpec((None, 2, half, s8, cb), lambda p, g, l: (p, 0, 0, g, l)),
        out_shape=jax.ShapeDtypeStruct((pairs, 2, half, r, c), F32),
        name="dft_last",
        compiler_params=_params(("parallel", "parallel", "parallel"), big),
    )(last, b_data, x05, u5, bias.reshape(1, c))
    return z.reshape(batch * length, c)


@functools.lru_cache(maxsize=None)
def _dense_dft_tables(length):
    n = 2 * length
    idx = np.arange(n)
    f = np.exp(-2j * np.pi * np.outer(idx, idx) / n)
    fwd_data = _complex_as_real(f[:, :length])
    fwd_filt = np.concatenate([f.real, f.imag], axis=0)
    inv = _complex_as_real(np.conj(f[:length, :]) / n)
    return tuple(np.asarray(a, np.float32) for a in (fwd_data, fwd_filt, inv))


def _short_conv_gate_kernel(fd_ref, ff_ref, inv_ref, u_ref, x0_ref, k_ref, bias_ref, o_ref):
    n = k_ref.shape[0]
    u = u_ref[...]
    spec = _mdot(fd_ref[...], u.astype(BF16))
    kf = _mdot(ff_ref[...], k_ref[...].astype(BF16))
    xr, xi, kr, ki = spec[:n], spec[n:], kf[:n], kf[n:]
    prod = jnp.concatenate([xr * kr - xi * ki, xr * ki + xi * kr], axis=0).astype(BF16)
    o_ref[...] = x0_ref[...] * (_mdot(inv_ref[...], prod) + u * bias_ref[...])


def _long_conv_gate_short(u, x0, kfilt, bias, batch, length):
    c = u.shape[1]
    pairs = batch // 2
    cb = min(DFT_LANES, c)
    fwd_data, fwd_filt, inv = (jnp.asarray(t, BF16) for t in _dense_dft_tables(length))
    pair_spec = pl.BlockSpec((None, 2 * length, cb), lambda p, l: (p, 0, l))
    z = pl.pallas_call(
        _short_conv_gate_kernel,
        grid=(pairs, c // cb),
        in_specs=[_resident(fwd_data.shape), _resident(fwd_filt.shape), _resident(inv.shape),
                  pair_spec, pair_spec,
                  pl.BlockSpec((2 * length, cb), lambda p, l: (0, l)),
                  pl.BlockSpec((1, cb), lambda p, l: (0, l))],
        out_specs=pair_spec,
        out_shape=jax.ShapeDtypeStruct((pairs, 2 * length, c), F32),
        name="short_conv_gate",
        compiler_params=_params(("parallel", "parallel"), 0),
    )(fwd_data, fwd_filt, inv, u.reshape(pairs, 2 * length, c), x0.reshape(pairs, 2 * length, c),
      kfilt, bias.reshape(1, c))
    return z.reshape(batch * length, c)


def kernel(x, c, ctx, c_ctx, w_mod, b_mod, norm_g, ffn_w_gate, ffn_w_up, ffn_w_down, ab_w_in, na_rpb,
           sc_conv_w, ab_w_out, hy_w_in, hy_short_w, hy_f_w1, hy_f_b1, hy_f_w2, hy_f_b2, hy_f_w3, hy_f_b3,
           hy_f_w4, hy_sin_freq, hy_bias, hy_w_out):
    batch, seq, d = x.shape
    ctx_len = ctx.shape[1]
    depth = w_mod.shape[0]
    assert batch + 1 <= V7X_SUBLANES and seq % ROW_TILE == 0
    last_attn = (depth - 1) - (depth - 1) % 2
    sc_width = sc_conv_w.shape[-1]

    cc = jnp.zeros((V7X_SUBLANES, d), F32).at[:batch].set(c).at[batch].set(c_ctx)
    mod = _modulation(cc, w_mod, b_mod).reshape(depth, V7X_SUBLANES, N_MOD, d)
    lat_batch = lambda i, tm: i // (seq // tm)
    ctx_batch = lambda i, tm: batch

    xs = x.reshape(batch * seq, d)
    cs = ctx.reshape(batch * ctx_len, d)
    ffn_w = (ffn_w_gate.astype(BF16), ffn_w_up.astype(BF16), ffn_w_down.astype(BF16))

    for i in range(depth):
        j = i // 2
        g = norm_g[i]
        m_i = mod[i]
        keep_ctx = i <= last_attn
        upd_ctx = i < last_attn

        xs = _ffn(xs, m_i, g, *ffn_w, (i, 0), mrow=0, grow=0, batch_of=lat_batch)
        if keep_ctx:
            cs = _ffn(cs, m_i, g, *ffn_w, (i, 0), mrow=0, grow=0, batch_of=ctx_batch)

        if i % 2 == 0:
            w_in = ab_w_in[j].astype(BF16)
            w_out = ab_w_out[j].astype(BF16)
            ab_outs = [(2 * NA_WIDTH, BF16, False), (NA_WIDTH, BF16, True), (sc_width, BF16, False)]
            w_vt = w_in[:, 2 * NA_WIDTH:3 * NA_WIDTH].T
            in_ab = functools.partial(_inproj_call, _inproj_ab_kernel, "inproj_ab", mrow=3, grow=2, extra=(w_vt,))
            qk, vt, b_lat = in_ab(xs, m_i, g, w_in, sc_conv_w[j], ab_outs, seq_len=seq, batch_of=lat_batch)
            if upd_ctx:
                qk_c, vt_c, b_ctx = in_ab(cs, m_i, g, w_in, sc_conv_w[j], ab_outs, seq_len=ctx_len,
                                          batch_of=ctx_batch)
            else:
                qk_c, vt_c = in_ab(cs, m_i, g, w_in, sc_conv_w[j], ab_outs[:2], seq_len=ctx_len,
                                   batch_of=ctx_batch)
            a_lat = _neighbourhood_attention(qk, vt, qk_c, vt_c, na_rpb[j], batch, seq, ctx_len)
            mix_lat = ([a_lat, b_lat], w_out)
            if upd_ctx:
                mix_ctx = ([_context_attention(qk_c, vt_c, batch, ctx_len), b_ctx], w_out)
        else:
            w_in = hy_w_in[j].astype(BF16)
            w_out = hy_w_out[j].astype(BF16)
            hw = w_out.shape[0]
            filt = lambda n: _hyena_filter(n, hy_f_w1[j], hy_f_b1[j], hy_f_w2[j], hy_f_b2[j], hy_f_w3[j],
                                           hy_f_b3[j], hy_f_w4[j], hy_sin_freq[j])
            in_hy = functools.partial(_inproj_call, _inproj_hyena_kernel, "inproj_hyena", mrow=3, grow=2)
            x0, u = in_hy(xs, m_i, g, w_in, hy_short_w[j], [(hw, F32, False)] * 2, seq_len=seq,
                          batch_of=lat_batch)
            mix_lat = ([_long_conv_gate(u, x0, filt(seq), hy_bias[j], batch, seq)], w_out)
            if upd_ctx:
                x0_c, u_c = in_hy(cs, m_i, g, w_in, hy_short_w[j], [(hw, F32, False)] * 2, seq_len=ctx_len,
                                  batch_of=ctx_batch)
                mix_ctx = ([_long_conv_gate_short(u_c, x0_c, filt(ctx_len), hy_bias[j], batch, ctx_len)], w_out)

        xs = _ffn(xs, m_i, g, *ffn_w, (i, 1), mrow=6, grow=4, batch_of=lat_batch, mixer=mix_lat,
                  mix_mrow=5, mix_grow=3)
        if upd_ctx:
            cs = _ffn(cs, m_i, g, *ffn_w, (i, 1), mrow=6, grow=4, batch_of=ctx_batch, mixer=mix_ctx,
                      mix_mrow=5, mix_grow=3)

    return xs.reshape(batch, seq, d)
```

```python
import functools
import math

import numpy as np
import jax
import jax.numpy as jnp
from jax import lax
from jax.experimental import pallas as pl
from jax.experimental.pallas import tpu as pltpu

F32 = jnp.float32
BF16 = jnp.bfloat16

GRID_W = 64
NA_HEADS = 8
NA_HEAD_DIM = 64
NA_WIDTH = NA_HEADS * NA_HEAD_DIM
NA_KH = 8
NA_KW = 16
FFN_RES = 0.5
N_MOD = 9
RMS_EPS = 1e-6
HY_EMB = 33
HY_BANDS = (HY_EMB - 1) // 2
HY_TARGET = 1e-2
HY_FAST_PCT = 0.3
HY_SLOW_PCT = 1.5

V7X_LANES = 128
V7X_SUBLANES = 8
V7X_MXU_DIM = 256
V7X_VMEM_BYTES = 64 * 1024 * 1024

ROW_TILE = 512
FFN_CHUNK = 256
FFN_ROW_TILE = 1024
FFN_SUBBLOCKS = 2
FFN_SIDE_PIECES = 8
PROJ_CHUNK = 512
NA_ROWS_PER_STEP = 32
NA_SLAB_ROWS = NA_KH + 2
DFT_RADIX = 128
DFT_LANES = 1024
DFT_K1_PER_STEP = 16
FILTER_ROWS = 512
MASK_VALUE = -1e30


def _vmem_limit(nbytes):
    return int(min(max(nbytes, 32 * 1024 * 1024), V7X_VMEM_BYTES - 8 * 1024 * 1024))


def _params(semantics, vmem_bytes):
    return pltpu.CompilerParams(dimension_semantics=semantics, vmem_limit_bytes=_vmem_limit(vmem_bytes))


def _resident(shape):
    zeros = (0,) * len(shape)
    return pl.BlockSpec(shape, lambda *_: zeros, pipeline_mode=pl.Buffered(1))


def _rms(x):
    return x * lax.rsqrt(jnp.mean(x * x, axis=-1, keepdims=True) + RMS_EPS)


def _mdot(a, b):
    return jnp.dot(a, b, preferred_element_type=F32)


def _mod_kernel(c_ref, w_ref, b_ref, o_ref):
    c = c_ref[...]
    s = (c * jax.nn.sigmoid(c)).astype(BF16)
    o_ref[...] = _mdot(s, w_ref[...].astype(BF16)) + b_ref[...]


def _modulation(cc, w_mod, b_mod):
    depth, d, nd = w_mod.shape
    return pl.pallas_call(
        _mod_kernel,
        grid=(depth, nd // d),
        in_specs=[pl.BlockSpec((V7X_SUBLANES, d), lambda i, j: (0, 0)),
                  pl.BlockSpec((None, d, d), lambda i, j: (i, 0, j)),
                  pl.BlockSpec((None, 1, d), lambda i, j: (i, 0, j))],
        out_specs=pl.BlockSpec((None, V7X_SUBLANES, d), lambda i, j: (i, 0, j)),
        out_shape=jax.ShapeDtypeStruct((depth, V7X_SUBLANES, nd), F32),
        name="modulation",
        compiler_params=_params(("parallel", "parallel"), 4 * d * d * 4),
    )(cc, w_mod, b_mod.reshape(depth, 1, nd))


def _ffn_kernel(*refs, n_parts, mix_mrow, mix_grow, mrow, grow):
    part_refs = refs[:n_parts]
    w_out_ref = refs[n_parts] if n_parts else None
    x_ref, m_ref, g_ref, wg_ref, wu_ref, wd_ref, o_ref, acc_ref, h_ref = refs[n_parts + bool(n_parts):]
    shift, scale, gate = m_ref[mrow:mrow + 1, :], m_ref[mrow + 1:mrow + 2, :], m_ref[mrow + 2:mrow + 3, :]
    pre_gain = g_ref[grow:grow + 1, :] * (1.0 + scale)
    post_gain = (FFN_RES * gate) * g_ref[grow + 1:grow + 2, :]
    hm = x_ref.shape[0] // FFN_SUBBLOCKS
    subs = [slice(s * hm, (s + 1) * hm) for s in range(FFN_SUBBLOCKS)]
    n_chunks = wg_ref.shape[1] // FFN_CHUNK

    pieces = min(FFN_SIDE_PIECES, n_chunks - 1)
    pm = hm // pieces
    tile = (slice(0, V7X_SUBLANES), slice(0, V7X_LANES))

    def mixer_out(rows):
        if not n_parts:
            return None
        y, r0 = None, 0
        for p_ref in part_refs:
            r1 = r0 + p_ref.shape[1]
            term = _mdot(p_ref[rows, :].astype(BF16), w_out_ref[r0:r1, :])
            y = term if y is None else y + term
            r0 = r1
        return y

    def prologue(rows, y):
        x = x_ref[rows, :]
        if n_parts:
            x = x + m_ref[mix_mrow:mix_mrow + 1, :] * (_rms(y) * g_ref[mix_grow:mix_grow + 1, :])
            o_ref[rows, :] = x
        h = _rms(x) * pre_gain + shift
        h_ref[rows, :] = h.astype(BF16)
        return fold(h)

    def epilogue(rows):
        base = o_ref[rows, :] if n_parts else x_ref[rows, :]
        out = base + _rms(acc_ref[rows, :]) * post_gain
        o_ref[rows, :] = out
        return fold(out)

    def fold(v):
        total = None
        for r0 in range(0, v.shape[0], V7X_SUBLANES):
            for c0 in range(0, v.shape[1], V7X_LANES):
                t = v[r0:r0 + V7X_SUBLANES, c0:c0 + V7X_LANES]
                total = t if total is None else total + t
        return total

    def piece(rows, p):
        return slice(rows.start + p * pm, rows.start + (p + 1) * pm)

    prologue(subs[0], mixer_out(subs[0]))
    for s, rows in enumerate(subs):
        nxt = subs[s + 1] if s + 1 < len(subs) else None
        y_next = mixer_out(nxt) if nxt is not None else None
        for j in range(n_chunks):
            c0, c1 = j * FFN_CHUNK, (j + 1) * FFN_CHUNK
            if 1 <= j <= pieces:
                anchors = []
                if s > 0:
                    anchors.append(epilogue(piece(subs[s - 1], j - 1)))
                if nxt is not None:
                    rp = piece(nxt, j - 1)
                    anchors.append(prologue(rp, None if y_next is None else y_next[rp.start - nxt.start:
                                                                                   rp.stop - nxt.start]))
                for t in anchors:
                    zero = jnp.minimum(t, 0.0) * jnp.maximum(t, 0.0)
                    acc_ref[rows.start:rows.start + V7X_SUBLANES, 0:V7X_LANES] += zero
            h = h_ref[rows, :]
            g = _mdot(h, wg_ref[:, c0:c1])
            u = _mdot(h, wu_ref[:, c0:c1])
            a = (g * jax.nn.sigmoid(g) * u).astype(BF16)
            part = _mdot(a, wd_ref[c0:c1, :])
            if j == 0:
                acc_ref[rows, :] = part
            else:
                acc_ref[rows, :] += part
    epilogue(subs[-1])


def _layer_weight(w, index):
    lead = len(index)
    return pl.BlockSpec((None,) * lead + w.shape[lead:], lambda *_: tuple(index) + (0, 0),
                        pipeline_mode=pl.Buffered(1))


def _ffn(x, mod_i, g, wg, wu, wd, layer, *, mrow, grow, batch_of, mixer=None, mix_mrow=None, mix_grow=None):
    m, d = x.shape
    f = wg.shape[-1]
    tm = min(FFN_ROW_TILE, m)
    hm = tm // FFN_SUBBLOCKS
    parts, w_out = mixer if mixer else ((), None)
    kern = functools.partial(_ffn_kernel, n_parts=len(parts), mix_mrow=mix_mrow, mix_grow=mix_grow,
                             mrow=mrow, grow=grow)
    part_bytes = sum(p.shape[1] * p.dtype.itemsize for p in parts)
    vmem = (3 * d * f * 2 + 4 * tm * d * 4 + tm * d * 4 + 6 * hm * FFN_CHUNK * 4 + hm * d * 12
            + 2 * tm * part_bytes + (d * d * 2 if parts else 0))
    mix_specs = [pl.BlockSpec((tm, p.shape[1]), lambda i: (i, 0)) for p in parts]
    if parts:
        mix_specs.append(_resident(w_out.shape))
    return pl.pallas_call(
        kern,
        grid=(m // tm,),
        in_specs=mix_specs + [
            pl.BlockSpec((tm, d), lambda i: (i, 0)),
            pl.BlockSpec((None, N_MOD, d), lambda i: (batch_of(i, tm), 0, 0)),
            _resident(g.shape), _layer_weight(wg, layer), _layer_weight(wu, layer), _layer_weight(wd, layer)],
        out_specs=pl.BlockSpec((tm, d), lambda i: (i, 0)),
        out_shape=jax.ShapeDtypeStruct((m, d), F32),
        scratch_shapes=[pltpu.VMEM((tm, d), F32), pltpu.VMEM((tm, d), BF16)],
        name="ffn_mix" if parts else "ffn",
        compiler_params=_params(("parallel",), vmem),
    )(*parts, *([w_out] if parts else []), x, mod_i, g, wg, wu, wd)


def _conv3_rows(pe, w0, w1, w2, tm):
    n = pe.shape[0]
    lo = V7X_SUBLANES
    down = pltpu.roll(pe, 1, 0)[lo:lo + tm]
    up = pltpu.roll(pe, n - 1, 0)[lo:lo + tm]
    return down * w0 + pe[lo:lo + tm] * w1 + up * w2


def _normed_with_halo(x_ref, prev_ref, next_ref, m_ref, g_ref, mrow, grow, seq_len):
    tm = x_ref.shape[0]
    lo = V7X_SUBLANES
    shift, scale = m_ref[mrow:mrow + 1, :], m_ref[mrow + 1:mrow + 2, :]
    xe = jnp.concatenate([prev_ref[...], x_ref[...], next_ref[...]], axis=0)
    he = (_rms(xe) * g_ref[grow:grow + 1, :]) * (1.0 + scale) + shift
    start = (pl.program_id(0) * tm) & (seq_len - 1)
    keep_prev = (start != 0).astype(F32)
    keep_next = (start + tm != seq_len).astype(F32)
    row = lax.broadcasted_iota(jnp.int32, (tm + 2 * lo, 1), 0)
    keep = jnp.where(row < lo, keep_prev, jnp.where(row >= lo + tm, keep_next, 1.0))
    return he[lo:lo + tm].astype(BF16), (he * keep).astype(BF16)


def _inproj_ab_kernel(x_ref, prev_ref, next_ref, m_ref, g_ref, w_ref, cw_ref, wvt_ref, qk_ref, vt_ref, *b_refs,
                      mrow, grow, seq_len):
    tm = x_ref.shape[0]
    h, he = _normed_with_halo(x_ref, prev_ref, next_ref, m_ref, g_ref, mrow, grow, seq_len)
    nqk = qk_ref.shape[1]
    nq = nqk + vt_ref.shape[0]
    for c in range(0, nqk, PROJ_CHUNK):
        qk_ref[:, c:c + PROJ_CHUNK] = _mdot(h, w_ref[:, c:c + PROJ_CHUNK]).astype(qk_ref.dtype)
    vt_ref[...] = lax.dot_general(wvt_ref[...], h, (((1,), (1,)), ((), ())),
                                  preferred_element_type=F32).astype(vt_ref.dtype)
    if b_refs:
        (b_ref,) = b_refs
        sw = cw_ref.shape[1]
        gate = _mdot(h, w_ref[:, nq:nq + sw])
        p = _mdot(he, w_ref[:, nq + sw:nq + 2 * sw]) * _mdot(he, w_ref[:, nq + 2 * sw:nq + 3 * sw])
        conv = _conv3_rows(p, cw_ref[0:1, :], cw_ref[1:2, :], cw_ref[2:3, :], tm)
        b_ref[...] = (gate * conv).astype(b_ref.dtype)


def _inproj_hyena_kernel(x_ref, prev_ref, next_ref, m_ref, g_ref, w_ref, cw_ref, x0_ref, u_ref,
                         *, mrow, grow, seq_len):
    tm = x_ref.shape[0]
    _, he = _normed_with_halo(x_ref, prev_ref, next_ref, m_ref, g_ref, mrow, grow, seq_len)
    c = x0_ref.shape[1]

    def conv_cols(c0):
        cols = slice(c0, c0 + PROJ_CHUNK)
        return _conv3_rows(_mdot(he, w_ref[:, cols]), cw_ref[0:1, cols], cw_ref[1:2, cols], cw_ref[2:3, cols], tm)

    for c0 in range(0, c, PROJ_CHUNK):
        x0_ref[:, c0:c0 + PROJ_CHUNK] = conv_cols(c0)
        u_ref[:, c0:c0 + PROJ_CHUNK] = conv_cols(2 * c + c0) * conv_cols(c + c0)


def _halo_specs(tm, m, width):
    blocks = m // V7X_SUBLANES
    per = tm // V7X_SUBLANES
    prev = pl.BlockSpec((V7X_SUBLANES, width), lambda i: (jnp.maximum(i * per - 1, 0), 0))
    nxt = pl.BlockSpec((V7X_SUBLANES, width), lambda i: (jnp.minimum((i + 1) * per, blocks - 1), 0))
    return prev, nxt


def _inproj_call(kern, name, x, mod_i, g, w, conv_w, outs, *, mrow, grow, seq_len, batch_of, extra=()):
    m, d = x.shape
    tm = min(ROW_TILE, seq_len)
    assert seq_len & (seq_len - 1) == 0 and seq_len % tm == 0 and m % seq_len == 0
    prev, nxt = _halo_specs(tm, m, d)
    out_bytes = sum(cols * jnp.dtype(dt).itemsize for cols, dt, _ in outs)
    vmem = w.size * 2 + 2 * tm * d * 4 + 2 * tm * out_bytes + 6 * tm * d * 4
    out_specs = [pl.BlockSpec((cols, tm), lambda i: (0, i)) if tr else pl.BlockSpec((tm, cols), lambda i: (i, 0))
                 for cols, _, tr in outs]
    out_shape = [jax.ShapeDtypeStruct((cols, m) if tr else (m, cols), dt) for cols, dt, tr in outs]
    return pl.pallas_call(
        functools.partial(kern, mrow=mrow, grow=grow, seq_len=seq_len),
        grid=(m // tm,),
        in_specs=[pl.BlockSpec((tm, d), lambda i: (i, 0)), prev, nxt,
                  pl.BlockSpec((None, N_MOD, d), lambda i: (batch_of(i, tm), 0, 0)),
                  _resident(g.shape), _resident(w.shape), _resident(conv_w.shape)]
        + [_resident(e.shape) for e in extra],
        out_specs=out_specs,
        out_shape=out_shape,
        name=name,
        compiler_params=_params(("parallel",), vmem),
    )(x, x, x, mod_i, g, w, conv_w, *extra)


def _split_heads_on_rows(q, low_lanes):
    zero = jnp.zeros_like(q)
    return jnp.concatenate([jnp.where(low_lanes, q, zero), jnp.where(low_lanes, zero, q)], axis=0)


def _na_kernel(pat_ref, q_ref, k_ref, vt_ref, kc_ref, vct_ref, bt_ref, o_ref, *, rows):
    rb = pl.program_id(2)
    step_rows = q_ref.shape[0] // GRID_W
    low_lanes = lax.broadcasted_iota(jnp.int32, (GRID_W, V7X_LANES), 1) < NA_HEAD_DIM
    kc = kc_ref[...]
    vct = vct_ref[...]
    nt = (((1,), (1,)), ((), ()))

    for t in range(step_rows // 2):
        r = rb * step_rows + 2 * t
        base = jnp.clip(r - NA_KH // 2, 0, rows - NA_SLAB_ROWS)
        k0 = pl.multiple_of(base * GRID_W, 2 * GRID_W)
        qa = q_ref[2 * t * GRID_W:(2 * t + 1) * GRID_W, :] * (NA_HEAD_DIM ** -0.5)
        qb = q_ref[(2 * t + 1) * GRID_W:(2 * t + 2) * GRID_W, :] * (NA_HEAD_DIM ** -0.5)
        q4 = jnp.concatenate([_split_heads_on_rows(qa, low_lanes), _split_heads_on_rows(qb, low_lanes)], axis=0)
        kk = k_ref[pl.ds(k0, NA_SLAB_ROWS * GRID_W), :]
        s = lax.dot_general(kk, q4, nt, preferred_element_type=F32) + bt_ref[pat_ref[r // 2]]
        sc = lax.dot_general(kc, q4, nt, preferred_element_type=F32)
        mx = jnp.maximum(jnp.max(s, axis=0, keepdims=True), jnp.max(sc, axis=0, keepdims=True))
        p = jnp.exp(s - mx)
        pc = jnp.exp(sc - mx)
        den = jnp.sum(p, axis=0, keepdims=True) + jnp.sum(pc, axis=0, keepdims=True)
        vv = vt_ref[:, pl.ds(k0, NA_SLAB_ROWS * GRID_W)]
        ot = (_mdot(vv, p.astype(BF16)) + _mdot(vct, pc.astype(BF16))) / den
        o4 = ot.T
        oa = jnp.where(low_lanes, o4[0:GRID_W], o4[GRID_W:2 * GRID_W])
        ob = jnp.where(low_lanes, o4[2 * GRID_W:3 * GRID_W], o4[3 * GRID_W:4 * GRID_W])
        o_ref[2 * t * GRID_W:(2 * t + 2) * GRID_W, :] = jnp.concatenate([oa, ob], axis=0).astype(o_ref.dtype)


def _na_window_row(r, rows):
    return min(max(r - NA_KH // 2, 0), rows - NA_KH)


def _na_bias_table(rpb, rows):
    h = rpb.shape[0]
    c = np.arange(GRID_W)[:, None]
    kc = np.arange(GRID_W)[None, :]
    start = np.clip(c - NA_KW // 2, 0, GRID_W - NA_KW)
    valid = (kc >= start) & (kc < start + NA_KW)
    onehot = ((kc - c + (NA_KW - 1))[:, :, None] == np.arange(2 * NA_KW - 1)) & valid[:, :, None]
    dense = jnp.einsum("hdx,ckx->hdkc", rpb, jnp.asarray(onehot, F32), precision=lax.Precision.HIGHEST)
    dense = dense + jnp.asarray(np.where(valid.T, 0.0, MASK_VALUE), F32)
    masked = jnp.full((h, GRID_W, GRID_W), MASK_VALUE, F32)
    patterns, ids = [], []
    for r in range(0, rows, 2):
        base = min(max(r - NA_KH // 2, 0), rows - NA_SLAB_ROWS)
        key = tuple((_na_window_row(r + s, rows) - base, _na_window_row(r + s, rows) - (r + s) + NA_KH - 1)
                    for s in (0, 1))
        if key not in patterns:
            patterns.append(key)
        ids.append(patterns.index(key))
    tabs = []
    for key in patterns:
        per_row = []
        for off, dy0 in key:
            blocks = [dense[:, dy0 + y - off] if off <= y < off + NA_KH else masked for y in range(NA_SLAB_ROWS)]
            per_row.append(jnp.concatenate(blocks, axis=1))
        t = jnp.stack(per_row, axis=1).reshape(h // 2, 2, 2, NA_SLAB_ROWS * GRID_W, GRID_W)
        tabs.append(t.transpose(0, 3, 2, 1, 4).reshape(h // 2, NA_SLAB_ROWS * GRID_W, 4 * GRID_W))
    return jnp.stack(tabs, axis=1), np.asarray(ids, np.int32)


def _neighbourhood_attention(qk, vt, qk_c, vt_c, rpb, batch, seq, ctx_len):
    rows = seq // GRID_W
    step_rows = min(NA_ROWS_PER_STEP, rows)
    assert rows >= NA_SLAB_ROWS and rows % step_rows == 0 and step_rows % 2 == 0
    nq = NA_WIDTH // V7X_LANES
    qk3 = qk.reshape(batch, seq, 2 * NA_WIDTH)
    qkc3 = qk_c.reshape(batch, ctx_len, 2 * NA_WIDTH)
    tq = step_rows * GRID_W
    table, pattern_ids = _na_bias_table(rpb, rows)
    vmem = 4 * seq * V7X_LANES * 2 + 2 * table[0].size * 4 + 16 * 1024 * 1024
    out = pl.pallas_call(
        functools.partial(_na_kernel, rows=rows),
        grid=(batch, nq, rows // step_rows),
        in_specs=[pl.BlockSpec(memory_space=pltpu.SMEM),
                  pl.BlockSpec((None, tq, V7X_LANES), lambda b, h, r: (b, r, h)),
                  pl.BlockSpec((None, seq, V7X_LANES), lambda b, h, r: (b, 0, nq + h)),
                  pl.BlockSpec((V7X_LANES, seq), lambda b, h, r: (h, b)),
                  pl.BlockSpec((None, ctx_len, V7X_LANES), lambda b, h, r: (b, 0, nq + h)),
                  pl.BlockSpec((V7X_LANES, ctx_len), lambda b, h, r: (h, b)),
                  pl.BlockSpec((None,) + table.shape[1:], lambda b, h, r: (h, 0, 0, 0))],
        out_specs=pl.BlockSpec((None, tq, V7X_LANES), lambda b, h, r: (b, r, h)),
        out_shape=jax.ShapeDtypeStruct((batch, seq, NA_WIDTH), BF16),
        name="na_attn",
        compiler_params=_params(("parallel", "parallel", "arbitrary"), vmem),
    )(jnp.asarray(pattern_ids), qk3, qk3, vt, qkc3, vt_c, table)
    return out.reshape(batch * seq, NA_WIDTH)


def _ctx_attn_kernel(q_ref, k_ref, vt_ref, o_ref):
    n = q_ref.shape[0]
    nt = (((1,), (1,)), ((), ()))
    low_lanes = lax.broadcasted_iota(jnp.int32, (n, V7X_LANES), 1) < NA_HEAD_DIM
    q2 = _split_heads_on_rows(q_ref[...] * (NA_HEAD_DIM ** -0.5), low_lanes)
    s = lax.dot_general(q2, k_ref[...], nt, preferred_element_type=F32)
    p = jnp.exp(s - jnp.max(s, axis=-1, keepdims=True))
    o2 = lax.dot_general(p.astype(BF16), vt_ref[...], nt, preferred_element_type=F32)
    o2 = o2 / jnp.sum(p, axis=-1, keepdims=True)
    o_ref[...] = jnp.where(low_lanes, o2[:n], o2[n:]).astype(o_ref.dtype)


def _context_attention(qk_c, vt_c, batch, ctx_len):
    nq = NA_WIDTH // V7X_LANES
    qkc3 = qk_c.reshape(batch, ctx_len, 2 * NA_WIDTH)
    out = pl.pallas_call(
        _ctx_attn_kernel,
        grid=(batch, nq),
        in_specs=[pl.BlockSpec((None, ctx_len, V7X_LANES), lambda b, h: (b, 0, h)),
                  pl.BlockSpec((None, ctx_len, V7X_LANES), lambda b, h: (b, 0, nq + h)),
                  pl.BlockSpec((V7X_LANES, ctx_len), lambda b, h: (h, b))],
        out_specs=pl.BlockSpec((None, ctx_len, V7X_LANES), lambda b, h: (b, 0, h)),
        out_shape=jax.ShapeDtypeStruct((batch, ctx_len, NA_WIDTH), BF16),
        name="ctx_attn",
        compiler_params=_params(("parallel", "parallel"), 0),
    )(qkc3, qkc3, vt_c)
    return out.reshape(batch * ctx_len, NA_WIDTH)


def _filter_kernel(z_ref, w1_ref, b1_ref, w2_ref, b2_ref, w3_ref, b3_ref, w4lo_ref, w4hi_ref, fr_ref, dec_ref,
                   o_ref, *, length):
    hi = lax.Precision.HIGHEST
    half = V7X_LANES // 2
    z = z_ref[...]
    fr = fr_ref[...]
    h = jnp.sin(fr * (jnp.dot(z, w1_ref[...], precision=hi, preferred_element_type=F32) + b1_ref[...]))
    h = jnp.sin(fr * (jnp.dot(h, w2_ref[...], precision=hi, preferred_element_type=F32) + b2_ref[...]))
    h = jnp.sin(fr * (jnp.dot(h, w3_ref[...], precision=hi, preferred_element_type=F32) + b3_ref[...]))
    hb = h.astype(BF16)
    n = z.shape[0]
    first = pl.program_id(0) * 2 * n
    for part, (w4_ref, t_lane) in enumerate(((w4lo_ref, 0), (w4hi_ref, half))):
        out = _mdot(hb, w4_ref[...].astype(BF16)) * jnp.exp(-z[:, t_lane:t_lane + 1] * dec_ref[...])
        row = first + part * n + lax.broadcasted_iota(jnp.int32, out.shape, 0)
        o_ref[part * n:(part + 1) * n, :] = jnp.where(row == length, 0.0, out)


@functools.lru_cache(maxsize=None)
def _filter_positions(length, block):
    half = V7X_LANES // 2
    t = np.linspace(0.0, 1.0, length)[:, None]
    w = 2.0 * math.pi * np.arange(length)[:, None] / length
    bands = np.linspace(1e-4, HY_BANDS - 1, HY_BANDS)[None, :]
    z = np.concatenate([t, np.cos(bands * w), -np.sin(bands * w)], axis=-1)
    z2 = np.concatenate([z, z[0:1], z[:0:-1]], axis=0)
    feat = np.zeros((2 * length, half), np.float32)
    feat[:, :HY_EMB] = z2
    feat = feat.reshape(2 * length // block, 2, block // 2, half)
    return np.ascontiguousarray(feat.transpose(0, 2, 1, 3)).reshape(length, V7X_LANES)


def _hyena_filter(length, w1, b1, w2, b2, w3, b3, w4, freq):
    c = w4.shape[1] // 2
    ffn = w2.shape[0]
    half = V7X_LANES // 2
    assert ffn == half and w1.shape[0] <= half
    tr = min(FILTER_ROWS, length)
    per_half = length // tr
    zpos = jnp.asarray(_filter_positions(length, tr))
    twice = lambda m: jnp.zeros((2 * m.shape[0], 2 * m.shape[1]), F32).at[:m.shape[0], :m.shape[1]].set(m).at[
        m.shape[0]:, m.shape[1]:].set(m)
    w1d = twice(jnp.pad(w1, ((0, half - w1.shape[0]), (0, 0))))
    w4lo = jnp.pad(w4, ((0, ffn), (0, 0)))
    w4hi = jnp.pad(w4, ((ffn, 0), (0, 0)))
    decay = np.abs(np.linspace(math.log(HY_TARGET) / HY_SLOW_PCT, math.log(HY_TARGET) / HY_FAST_PCT, c))
    decay = jnp.asarray(decay[None, :], F32)
    row2 = lambda v: jnp.tile(v.reshape(1, -1), (1, 2))
    small = lambda shape: pl.BlockSpec(shape, lambda i: (0, 0))
    mat, vec = small((V7X_LANES, V7X_LANES)), small((1, V7X_LANES))
    w4_spec = pl.BlockSpec((V7X_LANES, c), lambda i: (0, i // per_half))
    return pl.pallas_call(
        functools.partial(_filter_kernel, length=length),
        grid=(2 * per_half,),
        in_specs=[pl.BlockSpec((tr // 2, V7X_LANES), lambda i: (i, 0)),
                  mat, vec, mat, vec, mat, vec, w4_spec, w4_spec, vec, small((1, c))],
        out_specs=pl.BlockSpec((tr, c), lambda i: (i, 0)),
        out_shape=jax.ShapeDtypeStruct((2 * length, c), F32),
        name="hyena_filter",
        compiler_params=_params(("parallel",), 0),
    )(zpos, w1d, row2(b1), twice(w2), row2(b2), twice(w3), row2(b3), w4lo, w4hi, row2(freq), decay)


def _complex_as_real(m):
    return np.block([[m.real, -m.imag], [m.imag, m.real]])


@functools.lru_cache(maxsize=None)
def _dft_tables(length):
    n = 2 * length
    n1 = n // DFT_RADIX
    half = n1 // 2
    eye = np.eye(V7X_SUBLANES)
    k1 = np.arange(n1)
    f1 = np.exp(-2j * np.pi * np.outer(k1, k1) / n1)
    first_data = np.kron(_complex_as_real(f1[:, :half]), eye)
    first_filt = np.kron(np.concatenate([f1.real, f1.imag], axis=0), eye)
    inv1 = np.exp(2j * np.pi * np.outer(np.arange(half), k1) / n1) / n
    last = np.kron(_complex_as_real(inv1), eye)
    i2 = np.arange(DFT_RADIX)
    phase = (np.outer(i2, i2)[None] / DFT_RADIX + (i2[None, None, :] * k1[:, None, None]) / n)
    second = np.exp(-2j * np.pi * phase)
    fwd = np.stack([_complex_as_real(second[k]) for k in range(n1)])
    inv = np.stack([_complex_as_real(np.conj(second[k]).T) for k in range(n1)])
    return tuple(np.asarray(a, np.float32) for a in (first_data, first_filt, last, fwd, inv))


def _pack_pair(re, im):
    re_bits = lax.bitcast_convert_type(re.astype(BF16).astype(F32), jnp.uint32)
    im_bits = lax.bitcast_convert_type(im.astype(BF16).astype(F32), jnp.uint32)
    return (re_bits >> 16) | im_bits


def _unpack_pair(packed):
    re = lax.bitcast_convert_type(packed << 16, F32)
    im = lax.bitcast_convert_type(packed & jnp.uint32(0xFFFF0000), F32)
    return jnp.concatenate([re, im], axis=0).astype(BF16)


def _kron_kernel(mat_ref, x_ref, o_ref):
    lanes = x_ref.shape[-1]
    spec = _mdot(mat_ref[...], x_ref[...].reshape(-1, lanes).astype(BF16))
    rows = spec.shape[0] // 2
    o_ref[...] = _pack_pair(spec[:rows], spec[rows:]).reshape(o_ref.shape)


def _kron_out_kernel(mat_ref, b_ref, x0_ref, u_ref, bias_ref, o_ref):
    lanes = b_ref.shape[-1]
    y = _mdot(mat_ref[...], _unpack_pair(b_ref[...].reshape(-1, lanes))).reshape(o_ref.shape)
    o_ref[...] = x0_ref[...] * (y + u_ref[...] * bias_ref[...])


def _second_stage_kernel(a_ref, af_ref, g_ref, gi_ref, o_ref, kf_ref):
    r = DFT_RADIX
    steps = a_ref.shape[0]

    @pl.when(pl.program_id(2) == 0)
    def _():
        for j in range(steps):
            kf_ref[j] = _mdot(g_ref[j], _unpack_pair(af_ref[j]))

    for j in range(steps):
        spec = _mdot(g_ref[j], _unpack_pair(a_ref[j]))
        xr, xi = spec[:r], spec[r:]
        kr, ki = kf_ref[j, :r], kf_ref[j, r:]
        prod = jnp.concatenate([xr * kr - xi * ki, xr * ki + xi * kr], axis=0).astype(BF16)
        back = _mdot(gi_ref[j], prod)
        o_ref[j] = _pack_pair(back[:r], back[r:])


def _long_conv_gate(u, x0, kfilt, bias, batch, length):
    c = u.shape[1]
    pairs = batch // 2
    n1 = 2 * length // DFT_RADIX
    half = n1 // 2
    r, s8, cb, kb = DFT_RADIX, V7X_SUBLANES, min(DFT_LANES, c), min(DFT_K1_PER_STEP, n1)
    cb2 = min(V7X_MXU_DIM, c)
    assert batch % 2 == 0 and n1 % kb == 0 and c % cb == 0 and c % cb2 == 0
    first_data, first_filt, last, fwd, inv = (jnp.asarray(t, BF16) for t in _dft_tables(length))
    groups = r // s8
    big = 48 * 1024 * 1024

    u5 = u.reshape(pairs, 2, half, r, c)
    x05 = x0.reshape(pairs, 2, half, r, c)
    a_data = pl.pallas_call(
        _kron_kernel,
        grid=(pairs, groups, c // cb),
        in_specs=[_resident(first_data.shape),
                  pl.BlockSpec((None, 2, half, s8, cb), lambda p, g, l: (p, 0, 0, g, l))],
        out_specs=pl.BlockSpec((None, n1, s8, cb), lambda p, g, l: (p, 0, g, l)),
        out_shape=jax.ShapeDtypeStruct((pairs, n1, r, c), jnp.uint32),
        name="dft_first",
        compiler_params=_params(("parallel", "parallel", "parallel"), big),
    )(first_data, u5)

    a_filt = pl.pallas_call(
        _kron_kernel,
        grid=(groups, c // cb),
        in_specs=[_resident(first_filt.shape),
                  pl.BlockSpec((n1, s8, cb), lambda g, l: (0, g, l))],
        out_specs=pl.BlockSpec((n1, s8, cb), lambda g, l: (0, g, l)),
        out_shape=jax.ShapeDtypeStruct((n1, r, c), jnp.uint32),
        name="dft_first_filter",
        compiler_params=_params(("parallel", "parallel"), big),
    )(first_filt, kfilt.reshape(n1, r, c))

    b_data = pl.pallas_call(
        _second_stage_kernel,
        grid=(n1 // kb, c // cb2, pairs),
        in_specs=[pl.BlockSpec((None, kb, r, cb2), lambda k, l, p: (p, k, 0, l)),
                  pl.BlockSpec((kb, r, cb2), lambda k, l, p: (k, 0, l)),
                  pl.BlockSpec((kb, 2 * r, 2 * r), lambda k, l, p: (k, 0, 0)),
                  pl.BlockSpec((kb, 2 * r, 2 * r), lambda k, l, p: (k, 0, 0))],
        out_specs=pl.BlockSpec((None, kb, r, cb2), lambda k, l, p: (p, k, 0, l)),
        out_shape=jax.ShapeDtypeStruct((pairs, n1, r, c), jnp.uint32),
        scratch_shapes=[pltpu.VMEM((kb, 2 * r, cb2), F32)],
        name="dft_second",
        compiler_params=_params(("arbitrary", "arbitrary", "arbitrary"), big),
    )(a_data, a_filt, fwd, inv)

    z = pl.pallas_call(
        _kron_out_kernel,
        grid=(pairs, groups, c // cb),
        in_specs=[_resident(last.shape),
                  pl.BlockSpec((None, n1, s8, cb), lambda p, g, l: (p, 0, g, l)),
                  pl.BlockSpec((None, 2, half, s8, cb), lambda p, g, l: (p, 0, 0, g, l)),
                  pl.BlockSpec((None, 2, half, s8, cb), lambda p, g, l: (p, 0, 0, g, l)),
                  pl.BlockSpec((1, cb), lambda p, g, l: (0, l))],
        out_specs=pl.BlockSpec((None, 2, half, s8, cb), lambda p, g, l: (p, 0, 0, g, l)),
        out_shape=jax.ShapeDtypeStruct((pairs, 2, half, r, c), F32),
        name="dft_last",
        compiler_params=_params(("parallel", "parallel", "parallel"), big),
    )(last, b_data, x05, u5, bias.reshape(1, c))
    return z.reshape(batch * length, c)


@functools.lru_cache(maxsize=None)
def _dense_dft_tables(length):
    n = 2 * length
    idx = np.arange(n)
    f = np.exp(-2j * np.pi * np.outer(idx, idx) / n)
    fwd_data = _complex_as_real(f[:, :length])
    fwd_filt = np.concatenate([f.real, f.imag], axis=0)
    inv = _complex_as_real(np.conj(f[:length, :]) / n)
    return tuple(np.asarray(a, np.float32) for a in (fwd_data, fwd_filt, inv))


def _short_conv_gate_kernel(fd_ref, ff_ref, inv_ref, u_ref, x0_ref, k_ref, bias_ref, o_ref):
    n = k_ref.shape[0]
    u = u_ref[...]
    spec = _mdot(fd_ref[...], u.astype(BF16))
    kf = _mdot(ff_ref[...], k_ref[...].astype(BF16))
    xr, xi, kr, ki = spec[:n], spec[n:], kf[:n], kf[n:]
    prod = jnp.concatenate([xr * kr - xi * ki, xr * ki + xi * kr], axis=0).astype(BF16)
    o_ref[...] = x0_ref[...] * (_mdot(inv_ref[...], prod) + u * bias_ref[...])


def _long_conv_gate_short(u, x0, kfilt, bias, batch, length):
    c = u.shape[1]
    pairs = batch // 2
    cb = min(DFT_LANES, c)
    fwd_data, fwd_filt, inv = (jnp.asarray(t, BF16) for t in _dense_dft_tables(length))
    pair_spec = pl.BlockSpec((None, 2 * length, cb), lambda p, l: (p, 0, l))
    z = pl.pallas_call(
        _short_conv_gate_kernel,
        grid=(pairs, c // cb),
        in_specs=[_resident(fwd_data.shape), _resident(fwd_filt.shape), _resident(inv.shape),
                  pair_spec, pair_spec,
                  pl.BlockSpec((2 * length, cb), lambda p, l: (0, l)),
                  pl.BlockSpec((1, cb), lambda p, l: (0, l))],
        out_specs=pair_spec,
        out_shape=jax.ShapeDtypeStruct((pairs, 2 * length, c), F32),
        name="short_conv_gate",
        compiler_params=_params(("parallel", "parallel"), 0),
    )(fwd_data, fwd_filt, inv, u.reshape(pairs, 2 * length, c), x0.reshape(pairs, 2 * length, c),
      kfilt, bias.reshape(1, c))
    return z.reshape(batch * length, c)


def kernel(x, c, ctx, c_ctx, w_mod, b_mod, norm_g, ffn_w_gate, ffn_w_up, ffn_w_down, ab_w_in, na_rpb,
           sc_conv_w, ab_w_out, hy_w_in, hy_short_w, hy_f_w1, hy_f_b1, hy_f_w2, hy_f_b2, hy_f_w3, hy_f_b3,
           hy_f_w4, hy_sin_freq, hy_bias, hy_w_out):
    batch, seq, d = x.shape
    ctx_len = ctx.shape[1]
    depth = w_mod.shape[0]
    assert batch + 1 <= V7X_SUBLANES and seq % ROW_TILE == 0
    last_attn = (depth - 1) - (depth - 1) % 2
    sc_width = sc_conv_w.shape[-1]

    cc = jnp.zeros((V7X_SUBLANES, d), F32).at[:batch].set(c).at[batch].set(c_ctx)
    mod = _modulation(cc, w_mod, b_mod).reshape(depth, V7X_SUBLANES, N_MOD, d)
    lat_batch = lambda i, tm: i // (seq // tm)
    ctx_batch = lambda i, tm: batch

    xs = x.reshape(batch * seq, d)
    cs = ctx.reshape(batch * ctx_len, d)
    ffn_w = (ffn_w_gate.astype(BF16), ffn_w_up.astype(BF16), ffn_w_down.astype(BF16))

    for i in range(depth):
        j = i // 2
        g = norm_g[i]
        m_i = mod[i]
        keep_ctx = i <= last_attn
        upd_ctx = i < last_attn

        xs = _ffn(xs, m_i, g, *ffn_w, (i, 0), mrow=0, grow=0, batch_of=lat_batch)
        if keep_ctx:
            cs = _ffn(cs, m_i, g, *ffn_w, (i, 0), mrow=0, grow=0, batch_of=ctx_batch)

        if i % 2 == 0:
            w_in = ab_w_in[j].astype(BF16)
            w_out = ab_w_out[j].astype(BF16)
            ab_outs = [(2 * NA_WIDTH, BF16, False), (NA_WIDTH, BF16, True), (sc_width, BF16, False)]
            w_vt = w_in[:, 2 * NA_WIDTH:3 * NA_WIDTH].T
            in_ab = functools.partial(_inproj_call, _inproj_ab_kernel, "inproj_ab", mrow=3, grow=2, extra=(w_vt,))
            qk, vt, b_lat = in_ab(xs, m_i, g, w_in, sc_conv_w[j], ab_outs, seq_len=seq, batch_of=lat_batch)
            if upd_ctx:
                qk_c, vt_c, b_ctx = in_ab(cs, m_i, g, w_in, sc_conv_w[j], ab_outs, seq_len=ctx_len,
                                          batch_of=ctx_batch)
            else:
                qk_c, vt_c = in_ab(cs, m_i, g, w_in, sc_conv_w[j], ab_outs[:2], seq_len=ctx_len,
                                   batch_of=ctx_batch)
            a_lat = _neighbourhood_attention(qk, vt, qk_c, vt_c, na_rpb[j], batch, seq, ctx_len)
            mix_lat = ([a_lat, b_lat], w_out)
            if upd_ctx:
                mix_ctx = ([_context_attention(qk_c, vt_c, batch, ctx_len), b_ctx], w_out)
        else:
            w_in = hy_w_in[j].astype(BF16)
            w_out = hy_w_out[j].astype(BF16)
            hw = w_out.shape[0]
            filt = lambda n: _hyena_filter(n, hy_f_w1[j], hy_f_b1[j], hy_f_w2[j], hy_f_b2[j], hy_f_w3[j],
                                           hy_f_b3[j], hy_f_w4[j], hy_sin_freq[j])
            in_hy = functools.partial(_inproj_call, _inproj_hyena_kernel, "inproj_hyena", mrow=3, grow=2)
            x0, u = in_hy(xs, m_i, g, w_in, hy_short_w[j], [(hw, F32, False)] * 2, seq_len=seq,
                          batch_of=lat_batch)
            mix_lat = ([_long_conv_gate(u, x0, filt(seq), hy_bias[j], batch, seq)], w_out)
            if upd_ctx:
                x0_c, u_c = in_hy(cs, m_i, g, w_in, hy_short_w[j], [(hw, F32, False)] * 2, seq_len=ctx_len,
                                  batch_of=ctx_batch)
                mix_ctx = ([_long_conv_gate_short(u_c, x0_c, filt(ctx_len), hy_bias[j], batch, ctx_len)], w_out)

        xs = _ffn(xs, m_i, g, *ffn_w, (i, 1), mrow=6, grow=4, batch_of=lat_batch, mixer=mix_lat,
                  mix_mrow=5, mix_grow=3)
        if upd_ctx:
            cs = _ffn(cs, m_i, g, *ffn_w, (i, 1), mrow=6, grow=4, batch_of=ctx_batch, mixer=mix_ctx,
                      mix_mrow=5, mix_grow=3)

    return xs.reshape(batch, seq, d)
```

```python
import functools
import math

import numpy as np
import jax
import jax.numpy as jnp
from jax import lax
from jax.experimental import pallas as pl
from jax.experimental.pallas import tpu as pltpu

F32 = jnp.float32
BF16 = jnp.bfloat16

GRID_W = 64
NA_HEADS = 8
NA_HEAD_DIM = 64
NA_WIDTH = NA_HEADS * NA_HEAD_DIM
NA_KH = 8
NA_KW = 16
FFN_RES = 0.5
N_MOD = 9
RMS_EPS = 1e-6
HY_EMB = 33
HY_BANDS = (HY_EMB - 1) // 2
HY_TARGET = 1e-2
HY_FAST_PCT = 0.3
HY_SLOW_PCT = 1.5

V7X_LANES = 128
V7X_SUBLANES = 8
V7X_MXU_DIM = 256
V7X_VMEM_BYTES = 64 * 1024 * 1024

ROW_TILE = 512
FFN_CHUNK = 256
FFN_ROW_TILE = 1024
FFN_SUBBLOCKS = 2
FFN_SIDE_PIECES = 8
PROJ_CHUNK = 256
NA_ROWS_PER_STEP = 64
NA_SLAB_ROWS = NA_KH + 2
DFT_RADIX = 128
DFT_LANES = 1024
DFT_K1_PER_STEP = 16
FILTER_ROWS = 512
MASK_VALUE = -1e30


def _vmem_limit(nbytes):
    return int(min(max(nbytes, 32 * 1024 * 1024), V7X_VMEM_BYTES - 8 * 1024 * 1024))


def _params(semantics, vmem_bytes):
    return pltpu.CompilerParams(dimension_semantics=semantics, vmem_limit_bytes=_vmem_limit(vmem_bytes))


def _resident(shape):
    zeros = (0,) * len(shape)
    return pl.BlockSpec(shape, lambda *_: zeros, pipeline_mode=pl.Buffered(1))


def _rms(x):
    return x * lax.rsqrt(jnp.mean(x * x, axis=-1, keepdims=True) + RMS_EPS)


def _mdot(a, b):
    return jnp.dot(a, b, preferred_element_type=F32)


def _mod_kernel(c_ref, w_ref, b_ref, o_ref):
    c = c_ref[...]
    s = (c * jax.nn.sigmoid(c)).astype(BF16)
    o_ref[...] = _mdot(s, w_ref[...].astype(BF16)) + b_ref[...]


def _modulation(cc, w_mod, b_mod):
    depth, d, nd = w_mod.shape
    return pl.pallas_call(
        _mod_kernel,
        grid=(depth, nd // d),
        in_specs=[pl.BlockSpec((V7X_SUBLANES, d), lambda i, j: (0, 0)),
                  pl.BlockSpec((None, d, d), lambda i, j: (i, 0, j)),
                  pl.BlockSpec((None, 1, d), lambda i, j: (i, 0, j))],
        out_specs=pl.BlockSpec((None, V7X_SUBLANES, d), lambda i, j: (i, 0, j)),
        out_shape=jax.ShapeDtypeStruct((depth, V7X_SUBLANES, nd), F32),
        name="modulation",
        compiler_params=_params(("parallel", "parallel"), 4 * d * d * 4),
    )(cc, w_mod, b_mod.reshape(depth, 1, nd))


def _ffn_kernel(*refs, n_parts, mix_mrow, mix_grow, mrow, grow):
    part_refs = refs[:n_parts]
    w_out_ref = refs[n_parts] if n_parts else None
    x_ref, m_ref, g_ref, wg_ref, wu_ref, wd_ref, o_ref, acc_ref, h_ref = refs[n_parts + bool(n_parts):]
    shift, scale, gate = m_ref[mrow:mrow + 1, :], m_ref[mrow + 1:mrow + 2, :], m_ref[mrow + 2:mrow + 3, :]
    pre_gain = g_ref[grow:grow + 1, :] * (1.0 + scale)
    post_gain = (FFN_RES * gate) * g_ref[grow + 1:grow + 2, :]
    hm = x_ref.shape[0] // FFN_SUBBLOCKS
    subs = [slice(s * hm, (s + 1) * hm) for s in range(FFN_SUBBLOCKS)]
    n_chunks = wg_ref.shape[1] // FFN_CHUNK

    pieces = min(FFN_SIDE_PIECES, n_chunks - 1)
    pm = hm // pieces
    tile = (slice(0, V7X_SUBLANES), slice(0, V7X_LANES))

    def mixer_out(rows):
        if not n_parts:
            return None
        y, r0 = None, 0
        for p_ref in part_refs:
            r1 = r0 + p_ref.shape[1]
            term = _mdot(p_ref[rows, :].astype(BF16), w_out_ref[r0:r1, :])
            y = term if y is None else y + term
            r0 = r1
        return y

    def prologue(rows, y):
        x = x_ref[rows, :]
        if n_parts:
            x = x + m_ref[mix_mrow:mix_mrow + 1, :] * (_rms(y) * g_ref[mix_grow:mix_grow + 1, :])
            o_ref[rows, :] = x
        h = _rms(x) * pre_gain + shift
        h_ref[rows, :] = h.astype(BF16)
        return fold(h)

    def epilogue(rows):
        base = o_ref[rows, :] if n_parts else x_ref[rows, :]
        out = base + _rms(acc_ref[rows, :]) * post_gain
        o_ref[rows, :] = out
        return fold(out)

    def fold(v):
        total = None
        for r0 in range(0, v.shape[0], V7X_SUBLANES):
            for c0 in range(0, v.shape[1], V7X_LANES):
                t = v[r0:r0 + V7X_SUBLANES, c0:c0 + V7X_LANES]
                total = t if total is None else total + t
        return total

    def piece(rows, p):
        return slice(rows.start + p * pm, rows.start + (p + 1) * pm)

    prologue(subs[0], mixer_out(subs[0]))
    for s, rows in enumerate(subs):
        nxt = subs[s + 1] if s + 1 < len(subs) else None
        y_next = mixer_out(nxt) if nxt is not None else None
        for j in range(n_chunks):
            c0, c1 = j * FFN_CHUNK, (j + 1) * FFN_CHUNK
            if 1 <= j <= pieces:
                anchors = []
                if s > 0:
                    anchors.append(epilogue(piece(subs[s - 1], j - 1)))
                if nxt is not None:
                    rp = piece(nxt, j - 1)
                    anchors.append(prologue(rp, None if y_next is None else y_next[rp.start - nxt.start:
                                                                                   rp.stop - nxt.start]))
                for t in anchors:
                    zero = jnp.minimum(t, 0.0) * jnp.maximum(t, 0.0)
                    acc_ref[rows.start:rows.start + V7X_SUBLANES, 0:V7X_LANES] += zero
            h = h_ref[rows, :]
            g = _mdot(h, wg_ref[:, c0:c1])
            u = _mdot(h, wu_ref[:, c0:c1])
            a = (g * jax.nn.sigmoid(g) * u).astype(BF16)
            part = _mdot(a, wd_ref[c0:c1, :])
            if j == 0:
                acc_ref[rows, :] = part
            else:
                acc_ref[rows, :] += part
    epilogue(subs[-1])


def _layer_weight(w, index):
    lead = len(index)
    return pl.BlockSpec((None,) * lead + w.shape[lead:], lambda *_: tuple(index) + (0, 0),
                        pipeline_mode=pl.Buffered(1))


def _ffn(x, mod_i, g, wg, wu, wd, layer, *, mrow, grow, batch_of, mixer=None, mix_mrow=None, mix_grow=None):
    m, d = x.shape
    f = wg.shape[-1]
    tm = min(FFN_ROW_TILE, m)
    hm = tm // FFN_SUBBLOCKS
    parts, w_out = mixer if mixer else ((), None)
    kern = functools.partial(_ffn_kernel, n_parts=len(parts), mix_mrow=mix_mrow, mix_grow=mix_grow,
                             mrow=mrow, grow=grow)
    part_bytes = sum(p.shape[1] * p.dtype.itemsize for p in parts)
    vmem = (3 * d * f * 2 + 4 * tm * d * 4 + tm * d * 4 + 6 * hm * FFN_CHUNK * 4 + hm * d * 12
            + 2 * tm * part_bytes + (d * d * 2 if parts else 0))
    mix_specs = [pl.BlockSpec((tm, p.shape[1]), lambda i: (i, 0)) for p in parts]
    if parts:
        mix_specs.append(_resident(w_out.shape))
    return pl.pallas_call(
        kern,
        grid=(m // tm,),
        in_specs=mix_specs + [
            pl.BlockSpec((tm, d), lambda i: (i, 0)),
            pl.BlockSpec((None, N_MOD, d), lambda i: (batch_of(i, tm), 0, 0)),
            _resident(g.shape), _layer_weight(wg, layer), _layer_weight(wu, layer), _layer_weight(wd, layer)],
        out_specs=pl.BlockSpec((tm, d), lambda i: (i, 0)),
        out_shape=jax.ShapeDtypeStruct((m, d), F32),
        scratch_shapes=[pltpu.VMEM((tm, d), F32), pltpu.VMEM((tm, d), BF16)],
        name="ffn_mix" if parts else "ffn",
        compiler_params=_params(("parallel",), vmem),
    )(*parts, *([w_out] if parts else []), x, mod_i, g, wg, wu, wd)


def _conv3_rows(pe, w0, w1, w2, tm):
    n = pe.shape[0]
    lo = V7X_SUBLANES
    down = pltpu.roll(pe, 1, 0)[lo:lo + tm]
    up = pltpu.roll(pe, n - 1, 0)[lo:lo + tm]
    return down * w0 + pe[lo:lo + tm] * w1 + up * w2


def _normed_with_halo(x_ref, prev_ref, next_ref, m_ref, g_ref, mrow, grow, seq_len):
    tm = x_ref.shape[0]
    lo = V7X_SUBLANES
    shift, scale = m_ref[mrow:mrow + 1, :], m_ref[mrow + 1:mrow + 2, :]
    xe = jnp.concatenate([prev_ref[...], x_ref[...], next_ref[...]], axis=0)
    he = (_rms(xe) * g_ref[grow:grow + 1, :]) * (1.0 + scale) + shift
    start = (pl.program_id(0) * tm) & (seq_len - 1)
    keep_prev = (start != 0).astype(F32)
    keep_next = (start + tm != seq_len).astype(F32)
    row = lax.broadcasted_iota(jnp.int32, (tm + 2 * lo, 1), 0)
    keep = jnp.where(row < lo, keep_prev, jnp.where(row >= lo + tm, keep_next, 1.0))
    return he[lo:lo + tm].astype(BF16), (he * keep).astype(BF16)


def _inproj_ab_kernel(x_ref, prev_ref, next_ref, m_ref, g_ref, w_ref, cw_ref, wvt_ref, qk_ref, vt_ref, *b_refs,
                      mrow, grow, seq_len):
    tm = x_ref.shape[0]
    h, he = _normed_with_halo(x_ref, prev_ref, next_ref, m_ref, g_ref, mrow, grow, seq_len)
    nqk = qk_ref.shape[1]
    nq = nqk + vt_ref.shape[0]
    for c in range(0, nqk, PROJ_CHUNK):
        qk_ref[:, c:c + PROJ_CHUNK] = _mdot(h, w_ref[:, c:c + PROJ_CHUNK]).astype(qk_ref.dtype)
    vt_ref[...] = lax.dot_general(wvt_ref[...], h, (((1,), (1,)), ((), ())),
                                  preferred_element_type=F32).astype(vt_ref.dtype)
    if b_refs:
        (b_ref,) = b_refs
        sw = cw_ref.shape[1]
        gate = _mdot(h, w_ref[:, nq:nq + sw])
        p = _mdot(he, w_ref[:, nq + sw:nq + 2 * sw]) * _mdot(he, w_ref[:, nq + 2 * sw:nq + 3 * sw])
        conv = _conv3_rows(p, cw_ref[0:1, :], cw_ref[1:2, :], cw_ref[2:3, :], tm)
        b_ref[...] = (gate * conv).astype(b_ref.dtype)


def _inproj_hyena_kernel(x_ref, prev_ref, next_ref, m_ref, g_ref, w_ref, cw_ref, x0_ref, u_ref,
                         *, mrow, grow, seq_len):
    tm = x_ref.shape[0]
    _, he = _normed_with_halo(x_ref, prev_ref, next_ref, m_ref, g_ref, mrow, grow, seq_len)
    c = x0_ref.shape[1]

    def conv_cols(c0):
        cols = slice(c0, c0 + PROJ_CHUNK)
        return _conv3_rows(_mdot(he, w_ref[:, cols]), cw_ref[0:1, cols], cw_ref[1:2, cols], cw_ref[2:3, cols], tm)

    for c0 in range(0, c, PROJ_CHUNK):
        x0_ref[:, c0:c0 + PROJ_CHUNK] = conv_cols(c0)
        u_ref[:, c0:c0 + PROJ_CHUNK] = conv_cols(2 * c + c0) * conv_cols(c + c0)


def _halo_specs(tm, m, width):
    blocks = m // V7X_SUBLANES
    per = tm // V7X_SUBLANES
    prev = pl.BlockSpec((V7X_SUBLANES, width), lambda i: (jnp.maximum(i * per - 1, 0), 0))
    nxt = pl.BlockSpec((V7X_SUBLANES, width), lambda i: (jnp.minimum((i + 1) * per, blocks - 1), 0))
    return prev, nxt


def _inproj_call(kern, name, x, mod_i, g, w, conv_w, outs, *, mrow, grow, seq_len, batch_of, extra=()):
    m, d = x.shape
    tm = min(ROW_TILE, seq_len)
    assert seq_len & (seq_len - 1) == 0 and seq_len % tm == 0 and m % seq_len == 0
    prev, nxt = _halo_specs(tm, m, d)
    out_bytes = sum(cols * jnp.dtype(dt).itemsize for cols, dt, _ in outs)
    vmem = w.size * 2 + 2 * tm * d * 4 + 2 * tm * out_bytes + 6 * tm * d * 4
    out_specs = [pl.BlockSpec((cols, tm), lambda i: (0, i)) if tr else pl.BlockSpec((tm, cols), lambda i: (i, 0))
                 for cols, _, tr in outs]
    out_shape = [jax.ShapeDtypeStruct((cols, m) if tr else (m, cols), dt) for cols, dt, tr in outs]
    return pl.pallas_call(
        functools.partial(kern, mrow=mrow, grow=grow, seq_len=seq_len),
        grid=(m // tm,),
        in_specs=[pl.BlockSpec((tm, d), lambda i: (i, 0)), prev, nxt,
                  pl.BlockSpec((None, N_MOD, d), lambda i: (batch_of(i, tm), 0, 0)),
                  _resident(g.shape), _resident(w.shape), _resident(conv_w.shape)]
        + [_resident(e.shape) for e in extra],
        out_specs=out_specs,
        out_shape=out_shape,
        name=name,
        compiler_params=_params(("parallel",), vmem),
    )(x, x, x, mod_i, g, w, conv_w, *extra)


def _split_heads_on_rows(q, low_lanes):
    zero = jnp.zeros_like(q)
    return jnp.concatenate([jnp.where(low_lanes, q, zero), jnp.where(low_lanes, zero, q)], axis=0)


def _na_kernel(pat_ref, q_ref, k_ref, vt_ref, kc_ref, vct_ref, bt_ref, o_ref, *, rows):
    rb = pl.program_id(2)
    step_rows = q_ref.shape[0] // GRID_W
    low_lanes = lax.broadcasted_iota(jnp.int32, (GRID_W, V7X_LANES), 1) < NA_HEAD_DIM
    kc = kc_ref[...]
    vct = vct_ref[...]
    nt = (((1,), (1,)), ((), ()))

    for t in range(step_rows // 2):
        r = rb * step_rows + 2 * t
        base = jnp.clip(r - NA_KH // 2, 0, rows - NA_SLAB_ROWS)
        k0 = pl.multiple_of(base * GRID_W, 2 * GRID_W)
        qa = q_ref[2 * t * GRID_W:(2 * t + 1) * GRID_W, :] * (NA_HEAD_DIM ** -0.5)
        qb = q_ref[(2 * t + 1) * GRID_W:(2 * t + 2) * GRID_W, :] * (NA_HEAD_DIM ** -0.5)
        q4 = jnp.concatenate([_split_heads_on_rows(qa, low_lanes), _split_heads_on_rows(qb, low_lanes)], axis=0)
        kk = k_ref[pl.ds(k0, NA_SLAB_ROWS * GRID_W), :]
        s = lax.dot_general(kk, q4, nt, preferred_element_type=F32) + bt_ref[pat_ref[r // 2]]
        sc = lax.dot_general(kc, q4, nt, preferred_element_type=F32)
        mx = jnp.maximum(jnp.max(s, axis=0, keepdims=True), jnp.max(sc, axis=0, keepdims=True))
        p = jnp.exp(s - mx)
        pc = jnp.exp(sc - mx)
        den = jnp.sum(p, axis=0, keepdims=True) + jnp.sum(pc, axis=0, keepdims=True)
        vv = vt_ref[:, pl.ds(k0, NA_SLAB_ROWS * GRID_W)]
        ot = (_mdot(vv, p.astype(BF16)) + _mdot(vct, pc.astype(BF16))) / den
        o4 = ot.T
        oa = jnp.where(low_lanes, o4[0:GRID_W], o4[GRID_W:2 * GRID_W])
        ob = jnp.where(low_lanes, o4[2 * GRID_W:3 * GRID_W], o4[3 * GRID_W:4 * GRID_W])
        o_ref[2 * t * GRID_W:(2 * t + 2) * GRID_W, :] = jnp.concatenate([oa, ob], axis=0).astype(o_ref.dtype)


def _na_window_row(r, rows):
    return min(max(r - NA_KH // 2, 0), rows - NA_KH)


def _na_bias_table(rpb, rows):
    h = rpb.shape[0]
    c = np.arange(GRID_W)[:, None]
    kc = np.arange(GRID_W)[None, :]
    start = np.clip(c - NA_KW // 2, 0, GRID_W - NA_KW)
    valid = (kc >= start) & (kc < start + NA_KW)
    onehot = ((kc - c + (NA_KW - 1))[:, :, None] == np.arange(2 * NA_KW - 1)) & valid[:, :, None]
    dense = jnp.einsum("hdx,ckx->hdkc", rpb, jnp.asarray(onehot, F32), precision=lax.Precision.HIGHEST)
    dense = dense + jnp.asarray(np.where(valid.T, 0.0, MASK_VALUE), F32)
    masked = jnp.full((h, GRID_W, GRID_W), MASK_VALUE, F32)
    patterns, ids = [], []
    for r in range(0, rows, 2):
        base = min(max(r - NA_KH // 2, 0), rows - NA_SLAB_ROWS)
        key = tuple((_na_window_row(r + s, rows) - base, _na_window_row(r + s, rows) - (r + s) + NA_KH - 1)
                    for s in (0, 1))
        if key not in patterns:
            patterns.append(key)
        ids.append(patterns.index(key))
    tabs = []
    for key in patterns:
        per_row = []
        for off, dy0 in key:
            blocks = [dense[:, dy0 + y - off] if off <= y < off + NA_KH else masked for y in range(NA_SLAB_ROWS)]
            per_row.append(jnp.concatenate(blocks, axis=1))
        t = jnp.stack(per_row, axis=1).reshape(h // 2, 2, 2, NA_SLAB_ROWS * GRID_W, GRID_W)
        tabs.append(t.transpose(0, 3, 2, 1, 4).reshape(h // 2, NA_SLAB_ROWS * GRID_W, 4 * GRID_W))
    return jnp.stack(tabs, axis=1), np.asarray(ids, np.int32)


def _neighbourhood_attention(qk, vt, qk_c, vt_c, rpb, batch, seq, ctx_len):
    rows = seq // GRID_W
    step_rows = min(NA_ROWS_PER_STEP, rows)
    assert rows >= NA_SLAB_ROWS and rows % step_rows == 0 and step_rows % 2 == 0
    nq = NA_WIDTH // V7X_LANES
    qk3 = qk.reshape(batch, seq, 2 * NA_WIDTH)
    qkc3 = qk_c.reshape(batch, ctx_len, 2 * NA_WIDTH)
    tq = step_rows * GRID_W
    table, pattern_ids = _na_bias_table(rpb, rows)
    vmem = 4 * seq * V7X_LANES * 2 + 2 * table[0].size * 4 + 16 * 1024 * 1024
    out = pl.pallas_call(
        functools.partial(_na_kernel, rows=rows),
        grid=(batch, nq, rows // step_rows),
        in_specs=[pl.BlockSpec(memory_space=pltpu.SMEM),
                  pl.BlockSpec((None, tq, V7X_LANES), lambda b, h, r: (b, r, h)),
                  pl.BlockSpec((None, seq, V7X_LANES), lambda b, h, r: (b, 0, nq + h)),
                  pl.BlockSpec((V7X_LANES, seq), lambda b, h, r: (h, b)),
                  pl.BlockSpec((None, ctx_len, V7X_LANES), lambda b, h, r: (b, 0, nq + h)),
                  pl.BlockSpec((V7X_LANES, ctx_len), lambda b, h, r: (h, b)),
                  pl.BlockSpec((None,) + table.shape[1:], lambda b, h, r: (h, 0, 0, 0))],
        out_specs=pl.BlockSpec((None, tq, V7X_LANES), lambda b, h, r: (b, r, h)),
        out_shape=jax.ShapeDtypeStruct((batch, seq, NA_WIDTH), BF16),
        name="na_attn",
        compiler_params=_params(("parallel", "parallel", "arbitrary"), vmem),
    )(jnp.asarray(pattern_ids), qk3, qk3, vt, qkc3, vt_c, table)
    return out.reshape(batch * seq, NA_WIDTH)


def _ctx_attn_kernel(q_ref, k_ref, vt_ref, o_ref):
    n = q_ref.shape[0]
    nt = (((1,), (1,)), ((), ()))
    low_lanes = lax.broadcasted_iota(jnp.int32, (n, V7X_LANES), 1) < NA_HEAD_DIM
    q2 = _split_heads_on_rows(q_ref[...] * (NA_HEAD_DIM ** -0.5), low_lanes)
    s = lax.dot_general(q2, k_ref[...], nt, preferred_element_type=F32)
    p = jnp.exp(s - jnp.max(s, axis=-1, keepdims=True))
    o2 = lax.dot_general(p.astype(BF16), vt_ref[...], nt, preferred_element_type=F32)
    o2 = o2 / jnp.sum(p, axis=-1, keepdims=True)
    o_ref[...] = jnp.where(low_lanes, o2[:n], o2[n:]).astype(o_ref.dtype)


def _context_attention(qk_c, vt_c, batch, ctx_len):
    nq = NA_WIDTH // V7X_LANES
    qkc3 = qk_c.reshape(batch, ctx_len, 2 * NA_WIDTH)
    out = pl.pallas_call(
        _ctx_attn_kernel,
        grid=(batch, nq),
        in_specs=[pl.BlockSpec((None, ctx_len, V7X_LANES), lambda b, h: (b, 0, h)),
                  pl.BlockSpec((None, ctx_len, V7X_LANES), lambda b, h: (b, 0, nq + h)),
                  pl.BlockSpec((V7X_LANES, ctx_len), lambda b, h: (h, b))],
        out_specs=pl.BlockSpec((None, ctx_len, V7X_LANES), lambda b, h: (b, 0, h)),
        out_shape=jax.ShapeDtypeStruct((batch, ctx_len, NA_WIDTH), BF16),
        name="ctx_attn",
        compiler_params=_params(("parallel", "parallel"), 0),
    )(qkc3, qkc3, vt_c)
    return out.reshape(batch * ctx_len, NA_WIDTH)


def _filter_kernel(z_ref, w1_ref, b1_ref, w2_ref, b2_ref, w3_ref, b3_ref, w4lo_ref, w4hi_ref, fr_ref, dec_ref,
                   o_ref, *, length):
    hi = lax.Precision.HIGHEST
    half = V7X_LANES // 2
    z = z_ref[...]
    fr = fr_ref[...]
    h = jnp.sin(fr * (jnp.dot(z, w1_ref[...], precision=hi, preferred_element_type=F32) + b1_ref[...]))
    h = jnp.sin(fr * (jnp.dot(h, w2_ref[...], precision=hi, preferred_element_type=F32) + b2_ref[...]))
    h = jnp.sin(fr * (jnp.dot(h, w3_ref[...], precision=hi, preferred_element_type=F32) + b3_ref[...]))
    hb = h.astype(BF16)
    n = z.shape[0]
    first = pl.program_id(0) * 2 * n
    for part, (w4_ref, t_lane) in enumerate(((w4lo_ref, 0), (w4hi_ref, half))):
        out = _mdot(hb, w4_ref[...].astype(BF16)) * jnp.exp(-z[:, t_lane:t_lane + 1] * dec_ref[...])
        row = first + part * n + lax.broadcasted_iota(jnp.int32, out.shape, 0)
        o_ref[part * n:(part + 1) * n, :] = jnp.where(row == length, 0.0, out)


@functools.lru_cache(maxsize=None)
def _filter_positions(length, block):
    half = V7X_LANES // 2
    t = np.linspace(0.0, 1.0, length)[:, None]
    w = 2.0 * math.pi * np.arange(length)[:, None] / length
    bands = np.linspace(1e-4, HY_BANDS - 1, HY_BANDS)[None, :]
    z = np.concatenate([t, np.cos(bands * w), -np.sin(bands * w)], axis=-1)
    z2 = np.concatenate([z, z[0:1], z[:0:-1]], axis=0)
    feat = np.zeros((2 * length, half), np.float32)
    feat[:, :HY_EMB] = z2
    feat = feat.reshape(2 * length // block, 2, block // 2, half)
    return np.ascontiguousarray(feat.transpose(0, 2, 1, 3)).reshape(length, V7X_LANES)


def _hyena_filter(length, w1, b1, w2, b2, w3, b3, w4, freq):
    c = w4.shape[1] // 2
    ffn = w2.shape[0]
    half = V7X_LANES // 2
    assert ffn == half and w1.shape[0] <= half
    tr = min(FILTER_ROWS, length)
    per_half = length // tr
    zpos = jnp.asarray(_filter_positions(length, tr))
    twice = lambda m: jnp.zeros((2 * m.shape[0], 2 * m.shape[1]), F32).at[:m.shape[0], :m.shape[1]].set(m).at[
        m.shape[0]:, m.shape[1]:].set(m)
    w1d = twice(jnp.pad(w1, ((0, half - w1.shape[0]), (0, 0))))
    w4lo = jnp.pad(w4, ((0, ffn), (0, 0)))
    w4hi = jnp.pad(w4, ((ffn, 0), (0, 0)))
    decay = np.abs(np.linspace(math.log(HY_TARGET) / HY_SLOW_PCT, math.log(HY_TARGET) / HY_FAST_PCT, c))
    decay = jnp.asarray(decay[None, :], F32)
    row2 = lambda v: jnp.tile(v.reshape(1, -1), (1, 2))
    small = lambda shape: pl.BlockSpec(shape, lambda i: (0, 0))
    mat, vec = small((V7X_LANES, V7X_LANES)), small((1, V7X_LANES))
    w4_spec = pl.BlockSpec((V7X_LANES, c), lambda i: (0, i // per_half))
    return pl.pallas_call(
        functools.partial(_filter_kernel, length=length),
        grid=(2 * per_half,),
        in_specs=[pl.BlockSpec((tr // 2, V7X_LANES), lambda i: (i, 0)),
                  mat, vec, mat, vec, mat, vec, w4_spec, w4_spec, vec, small((1, c))],
        out_specs=pl.BlockSpec((tr, c), lambda i: (i, 0)),
        out_shape=jax.ShapeDtypeStruct((2 * length, c), F32),
        name="hyena_filter",
        compiler_params=_params(("parallel",), 0),
    )(zpos, w1d, row2(b1), twice(w2), row2(b2), twice(w3), row2(b3), w4lo, w4hi, row2(freq), decay)


def _complex_as_real(m):
    return np.block([[m.real, -m.imag], [m.imag, m.real]])


@functools.lru_cache(maxsize=None)
def _dft_tables(length):
    n = 2 * length
    n1 = n // DFT_RADIX
    half = n1 // 2
    eye = np.eye(V7X_SUBLANES)
    k1 = np.arange(n1)
    f1 = np.exp(-2j * np.pi * np.outer(k1, k1) / n1)
    first_data = np.kron(_complex_as_real(f1[:, :half]), eye)
    first_filt = np.kron(np.concatenate([f1.real, f1.imag], axis=0), eye)
    inv1 = np.exp(2j * np.pi * np.outer(np.arange(half), k1) / n1) / n
    last = np.kron(_complex_as_real(inv1), eye)
    i2 = np.arange(DFT_RADIX)
    phase = (np.outer(i2, i2)[None] / DFT_RADIX + (i2[None, None, :] * k1[:, None, None]) / n)
    second = np.exp(-2j * np.pi * phase)
    fwd = np.stack([_complex_as_real(second[k]) for k in range(n1)])
    inv = np.stack([_complex_as_real(np.conj(second[k]).T) for k in range(n1)])
    return tuple(np.asarray(a, np.float32) for a in (first_data, first_filt, last, fwd, inv))


def _pack_pair(re, im):
    re_bits = lax.bitcast_convert_type(re.astype(BF16).astype(F32), jnp.uint32)
    im_bits = lax.bitcast_convert_type(im.astype(BF16).astype(F32), jnp.uint32)
    return (re_bits >> 16) | im_bits


def _unpack_pair(packed):
    re = lax.bitcast_convert_type(packed << 16, F32)
    im = lax.bitcast_convert_type(packed & jnp.uint32(0xFFFF0000), F32)
    return jnp.concatenate([re, im], axis=0).astype(BF16)


def _kron_kernel(mat_ref, x_ref, o_ref):
    lanes = x_ref.shape[-1]
    spec = _mdot(mat_ref[...], x_ref[...].reshape(-1, lanes).astype(BF16))
    rows = spec.shape[0] // 2
    o_ref[...] = _pack_pair(spec[:rows], spec[rows:]).reshape(o_ref.shape)


def _kron_out_kernel(mat_ref, b_ref, x0_ref, u_ref, bias_ref, o_ref):
    lanes = b_ref.shape[-1]
    y = _mdot(mat_ref[...], _unpack_pair(b_ref[...].reshape(-1, lanes))).reshape(o_ref.shape)
    o_ref[...] = x0_ref[...] * (y + u_ref[...] * bias_ref[...])


def _second_stage_kernel(a_ref, af_ref, g_ref, gi_ref, o_ref, kf_ref):
    r = DFT_RADIX
    steps = a_ref.shape[0]

    @pl.when(pl.program_id(2) == 0)
    def _():
        for j in range(steps):
            kf_ref[j] = _mdot(g_ref[j], _unpack_pair(af_ref[j]))

    for j in range(steps):
        spec = _mdot(g_ref[j], _unpack_pair(a_ref[j]))
        xr, xi = spec[:r], spec[r:]
        kr, ki = kf_ref[j, :r], kf_ref[j, r:]
        prod = jnp.concatenate([xr * kr - xi * ki, xr * ki + xi * kr], axis=0).astype(BF16)
        back = _mdot(gi_ref[j], prod)
        o_ref[j] = _pack_pair(back[:r], back[r:])


def _long_conv_gate(u, x0, kfilt, bias, batch, length):
    c = u.shape[1]
    pairs = batch // 2
    n1 = 2 * length // DFT_RADIX
    half = n1 // 2
    r, s8, cb, kb = DFT_RADIX, V7X_SUBLANES, min(DFT_LANES, c), min(DFT_K1_PER_STEP, n1)
    cb2 = min(V7X_MXU_DIM, c)
    assert batch % 2 == 0 and n1 % kb == 0 and c % cb == 0 and c % cb2 == 0
    first_data, first_filt, last, fwd, inv = (jnp.asarray(t, BF16) for t in _dft_tables(length))
    groups = r // s8
    big = 48 * 1024 * 1024

    u5 = u.reshape(pairs, 2, half, r, c)
    x05 = x0.reshape(pairs, 2, half, r, c)
    a_data = pl.pallas_call(
        _kron_kernel,
        grid=(pairs, groups, c // cb),
        in_specs=[_resident(first_data.shape),
                  pl.BlockSpec((None, 2, half, s8, cb), lambda p, g, l: (p, 0, 0, g, l))],
        out_specs=pl.BlockSpec((None, n1, s8, cb), lambda p, g, l: (p, 0, g, l)),
        out_shape=jax.ShapeDtypeStruct((pairs, n1, r, c), jnp.uint32),
        name="dft_first",
        compiler_params=_params(("parallel", "parallel", "parallel"), big),
    )(first_data, u5)

    a_filt = pl.pallas_call(
        _kron_kernel,
        grid=(groups, c // cb),
        in_specs=[_resident(first_filt.shape),
                  pl.BlockSpec((n1, s8, cb), lambda g, l: (0, g, l))],
        out_specs=pl.BlockSpec((n1, s8, cb), lambda g, l: (0, g, l)),
        out_shape=jax.ShapeDtypeStruct((n1, r, c), jnp.uint32),
        name="dft_first_filter",
        compiler_params=_params(("parallel", "parallel"), big),
    )(first_filt, kfilt.reshape(n1, r, c))

    b_data = pl.pallas_call(
        _second_stage_kernel,
        grid=(n1 // kb, c // cb2, pairs),
        in_specs=[pl.BlockSpec((None, kb, r, cb2), lambda k, l, p: (p, k, 0, l)),
                  pl.BlockSpec((kb, r, cb2), lambda k, l, p: (k, 0, l)),
                  pl.BlockSpec((kb, 2 * r, 2 * r), lambda k, l, p: (k, 0, 0)),
                  pl.BlockSpec((kb, 2 * r, 2 * r), lambda k, l, p: (k, 0, 0))],
        out_specs=pl.BlockSpec((None, kb, r, cb2), lambda k, l, p: (p, k, 0, l)),
        out_shape=jax.ShapeDtypeStruct((pairs, n1, r, c), jnp.uint32),
        scratch_shapes=[pltpu.VMEM((kb, 2 * r, cb2), F32)],
        name="dft_second",
        compiler_params=_params(("arbitrary", "arbitrary", "arbitrary"), big),
    )(a_data, a_filt, fwd, inv)

    z = pl.pallas_call(
        _kron_out_kernel,
        grid=(pairs, groups, c // cb),
        in_specs=[_resident(last.shape),
                  pl.BlockSpec((None, n1, s8, cb), lambda p, g, l: (p, 0, g, l)),
                  pl.BlockSpec((None, 2, half, s8, cb), lambda p, g, l: (p, 0, 0, g, l)),
                  pl.BlockSpec((None, 2, half, s8, cb), lambda p, g, l: (p, 0, 0, g, l)),
                  pl.BlockSpec((1, cb), lambda p, g, l: (0, l))],
        out_specs=pl.BlockSpec((None, 2, half, s8, cb), lambda p, g, l: (p, 0, 0, g, l)),
        out_shape=jax.ShapeDtypeStruct((pairs, 2, half, r, c), F32),
        name="dft_last",
        compiler_params=_params(("parallel", "parallel", "parallel"), big),
    )(last, b_data, x05, u5, bias.reshape(1, c))
    return z.reshape(batch * length, c)


@functools.lru_cache(maxsize=None)
def _dense_dft_tables(length):
    n = 2 * length
    idx = np.arange(n)
    f = np.exp(-2j * np.pi * np.outer(idx, idx) / n)
    fwd_data = _complex_as_real(f[:, :length])
    fwd_filt = np.concatenate([f.real, f.imag], axis=0)
    inv = _complex_as_real(np.conj(f[:length, :]) / n)
    return tuple(np.asarray(a, np.float32) for a in (fwd_data, fwd_filt, inv))


def _short_conv_gate_kernel(fd_ref, ff_ref, inv_ref, u_ref, x0_ref, k_ref, bias_ref, o_ref):
    n = k_ref.shape[0]
    u = u_ref[...]
    spec = _mdot(fd_ref[...], u.astype(BF16))
    kf = _mdot(ff_ref[...], k_ref[...].astype(BF16))
    xr, xi, kr, ki = spec[:n], spec[n:], kf[:n], kf[n:]
    prod = jnp.concatenate([xr * kr - xi * ki, xr * ki + xi * kr], axis=0).astype(BF16)
    o_ref[...] = x0_ref[...] * (_mdot(inv_ref[...], prod) + u * bias_ref[...])


def _long_conv_gate_short(u, x0, kfilt, bias, batch, length):
    c = u.shape[1]
    pairs = batch // 2
    cb = min(DFT_LANES, c)
    fwd_data, fwd_filt, inv = (jnp.asarray(t, BF16) for t in _dense_dft_tables(length))
    pair_spec = pl.BlockSpec((None, 2 * length, cb), lambda p, l: (p, 0, l))
    z = pl.pallas_call(
        _short_conv_gate_kernel,
        grid=(pairs, c // cb),
        in_specs=[_resident(fwd_data.shape), _resident(fwd_filt.shape), _resident(inv.shape),
                  pair_spec, pair_spec,
                  pl.BlockSpec((2 * length, cb), lambda p, l: (0, l)),
                  pl.BlockSpec((1, cb), lambda p, l: (0, l))],
        out_specs=pair_spec,
        out_shape=jax.ShapeDtypeStruct((pairs, 2 * length, c), F32),
        name="short_conv_gate",
        compiler_params=_params(("parallel", "parallel"), 0),
    )(fwd_data, fwd_filt, inv, u.reshape(pairs, 2 * length, c), x0.reshape(pairs, 2 * length, c),
      kfilt, bias.reshape(1, c))
    return z.reshape(batch * length, c)


def kernel(x, c, ctx, c_ctx, w_mod, b_mod, norm_g, ffn_w_gate, ffn_w_up, ffn_w_down, ab_w_in, na_rpb,
           sc_conv_w, ab_w_out, hy_w_in, hy_short_w, hy_f_w1, hy_f_b1, hy_f_w2, hy_f_b2, hy_f_w3, hy_f_b3,
           hy_f_w4, hy_sin_freq, hy_bias, hy_w_out):
    batch, seq, d = x.shape
    ctx_len = ctx.shape[1]
    depth = w_mod.shape[0]
    assert batch + 1 <= V7X_SUBLANES and seq % ROW_TILE == 0
    last_attn = (depth - 1) - (depth - 1) % 2
    sc_width = sc_conv_w.shape[-1]

    cc = jnp.zeros((V7X_SUBLANES, d), F32).at[:batch].set(c).at[batch].set(c_ctx)
    mod = _modulation(cc, w_mod, b_mod).reshape(depth, V7X_SUBLANES, N_MOD, d)
    lat_batch = lambda i, tm: i // (seq // tm)
    ctx_batch = lambda i, tm: batch

    xs = x.reshape(batch * seq, d)
    cs = ctx.reshape(batch * ctx_len, d)
    ffn_w = (ffn_w_gate.astype(BF16), ffn_w_up.astype(BF16), ffn_w_down.astype(BF16))

    for i in range(depth):
        j = i // 2
        g = norm_g[i]
        m_i = mod[i]
        keep_ctx = i <= last_attn
        upd_ctx = i < last_attn

        xs = _ffn(xs, m_i, g, *ffn_w, (i, 0), mrow=0, grow=0, batch_of=lat_batch)
        if keep_ctx:
            cs = _ffn(cs, m_i, g, *ffn_w, (i, 0), mrow=0, grow=0, batch_of=ctx_batch)

        if i % 2 == 0:
            w_in = ab_w_in[j].astype(BF16)
            w_out = ab_w_out[j].astype(BF16)
            ab_outs = [(2 * NA_WIDTH, BF16, False), (NA_WIDTH, BF16, True), (sc_width, BF16, False)]
            w_vt = w_in[:, 2 * NA_WIDTH:3 * NA_WIDTH].T
            in_ab = functools.partial(_inproj_call, _inproj_ab_kernel, "inproj_ab", mrow=3, grow=2, extra=(w_vt,))
            qk, vt, b_lat = in_ab(xs, m_i, g, w_in, sc_conv_w[j], ab_outs, seq_len=seq, batch_of=lat_batch)
            if upd_ctx:
                qk_c, vt_c, b_ctx = in_ab(cs, m_i, g, w_in, sc_conv_w[j], ab_outs, seq_len=ctx_len,
                                          batch_of=ctx_batch)
            else:
                qk_c, vt_c = in_ab(cs, m_i, g, w_in, sc_conv_w[j], ab_outs[:2], seq_len=ctx_len,
                                   batch_of=ctx_batch)
            a_lat = _neighbourhood_attention(qk, vt, qk_c, vt_c, na_rpb[j], batch, seq, ctx_len)
            mix_lat = ([a_lat, b_lat], w_out)
            if upd_ctx:
                mix_ctx = ([_context_attention(qk_c, vt_c, batch, ctx_len), b_ctx], w_out)
        else:
            w_in = hy_w_in[j].astype(BF16)
            w_out = hy_w_out[j].astype(BF16)
            hw = w_out.shape[0]
            filt = lambda n: _hyena_filter(n, hy_f_w1[j], hy_f_b1[j], hy_f_w2[j], hy_f_b2[j], hy_f_w3[j],
                                           hy_f_b3[j], hy_f_w4[j], hy_sin_freq[j])
            in_hy = functools.partial(_inproj_call, _inproj_hyena_kernel, "inproj_hyena", mrow=3, grow=2)
            x0, u = in_hy(xs, m_i, g, w_in, hy_short_w[j], [(hw, F32, False)] * 2, seq_len=seq,
                          batch_of=lat_batch)
            mix_lat = ([_long_conv_gate(u, x0, filt(seq), hy_bias[j], batch, seq)], w_out)
            if upd_ctx:
                x0_c, u_c = in_hy(cs, m_i, g, w_in, hy_short_w[j], [(hw, F32, False)] * 2, seq_len=ctx_len,
                                  batch_of=ctx_batch)
                mix_ctx = ([_long_conv_gate_short(u_c, x0_c, filt(ctx_len), hy_bias[j], batch, ctx_len)], w_out)

        xs = _ffn(xs, m_i, g, *ffn_w, (i, 1), mrow=6, grow=4, batch_of=lat_batch, mixer=mix_lat,
                  mix_mrow=5, mix_grow=3)
        if upd_ctx:
            cs = _ffn(cs, m_i, g, *ffn_w, (i, 1), mrow=6, grow=4, batch_of=ctx_batch, mixer=mix_ctx,
                      mix_mrow=5, mix_grow=3)

    return xs.reshape(batch, seq, d)
```

```python
import functools
import math

import numpy as np
import jax
import jax.numpy as jnp
from jax import lax
from jax.experimental import pallas as pl
from jax.experimental.pallas import tpu as pltpu

F32 = jnp.float32
BF16 = jnp.bfloat16

GRID_W = 64
NA_HEADS = 8
NA_HEAD_DIM = 64
NA_WIDTH = NA_HEADS * NA_HEAD_DIM
NA_KH = 8
NA_KW = 16
FFN_RES = 0.5
N_MOD = 9
RMS_EPS = 1e-6
HY_EMB = 33
HY_BANDS = (HY_EMB - 1) // 2
HY_TARGET = 1e-2
HY_FAST_PCT = 0.3
HY_SLOW_PCT = 1.5

V7X_LANES = 128
V7X_SUBLANES = 8
V7X_MXU_DIM = 256
V7X_VMEM_BYTES = 64 * 1024 * 1024

ROW_TILE = 1024
FFN_CHUNK = 256
FFN_ROW_TILE = 1024
FFN_SUBBLOCKS = 2
FFN_SIDE_PIECES = 8
PROJ_CHUNK = 256
NA_ROWS_PER_STEP = 64
NA_SLAB_ROWS = NA_KH + 2
DFT_RADIX = 128
DFT_LANES = 1024
DFT_K1_PER_STEP = 16
FILTER_ROWS = 512
MASK_VALUE = -1e30


def _vmem_limit(nbytes):
    return int(min(max(nbytes, 32 * 1024 * 1024), V7X_VMEM_BYTES - 8 * 1024 * 1024))


def _params(semantics, vmem_bytes):
    return pltpu.CompilerParams(dimension_semantics=semantics, vmem_limit_bytes=_vmem_limit(vmem_bytes))


def _resident(shape):
    zeros = (0,) * len(shape)
    return pl.BlockSpec(shape, lambda *_: zeros, pipeline_mode=pl.Buffered(1))


def _rms(x):
    return x * lax.rsqrt(jnp.mean(x * x, axis=-1, keepdims=True) + RMS_EPS)


def _mdot(a, b):
    return jnp.dot(a, b, preferred_element_type=F32)


def _mod_kernel(c_ref, w_ref, b_ref, o_ref):
    c = c_ref[...]
    s = (c * jax.nn.sigmoid(c)).astype(BF16)
    o_ref[...] = _mdot(s, w_ref[...].astype(BF16)) + b_ref[...]


def _modulation(cc, w_mod, b_mod):
    depth, d, nd = w_mod.shape
    return pl.pallas_call(
        _mod_kernel,
        grid=(depth, nd // d),
        in_specs=[pl.BlockSpec((V7X_SUBLANES, d), lambda i, j: (0, 0)),
                  pl.BlockSpec((None, d, d), lambda i, j: (i, 0, j)),
                  pl.BlockSpec((None, 1, d), lambda i, j: (i, 0, j))],
        out_specs=pl.BlockSpec((None, V7X_SUBLANES, d), lambda i, j: (i, 0, j)),
        out_shape=jax.ShapeDtypeStruct((depth, V7X_SUBLANES, nd), F32),
        name="modulation",
        compiler_params=_params(("parallel", "parallel"), 4 * d * d * 4),
    )(cc, w_mod, b_mod.reshape(depth, 1, nd))


def _ffn_kernel(*refs, n_parts, mix_mrow, mix_grow, mrow, grow):
    part_refs = refs[:n_parts]
    w_out_ref = refs[n_parts] if n_parts else None
    x_ref, m_ref, g_ref, wg_ref, wu_ref, wd_ref, o_ref, acc_ref, h_ref = refs[n_parts + bool(n_parts):]
    shift, scale, gate = m_ref[mrow:mrow + 1, :], m_ref[mrow + 1:mrow + 2, :], m_ref[mrow + 2:mrow + 3, :]
    pre_gain = g_ref[grow:grow + 1, :] * (1.0 + scale)
    post_gain = (FFN_RES * gate) * g_ref[grow + 1:grow + 2, :]
    hm = x_ref.shape[0] // FFN_SUBBLOCKS
    subs = [slice(s * hm, (s + 1) * hm) for s in range(FFN_SUBBLOCKS)]
    n_chunks = wg_ref.shape[1] // FFN_CHUNK

    pieces = min(FFN_SIDE_PIECES, n_chunks - 1)
    pm = hm // pieces
    tile = (slice(0, V7X_SUBLANES), slice(0, V7X_LANES))

    def mixer_out(rows):
        if not n_parts:
            return None
        y, r0 = None, 0
        for p_ref in part_refs:
            r1 = r0 + p_ref.shape[1]
            term = _mdot(p_ref[rows, :].astype(BF16), w_out_ref[r0:r1, :])
            y = term if y is None else y + term
            r0 = r1
        return y

    def prologue(rows, y):
        x = x_ref[rows, :]
        if n_parts:
            x = x + m_ref[mix_mrow:mix_mrow + 1, :] * (_rms(y) * g_ref[mix_grow:mix_grow + 1, :])
            o_ref[rows, :] = x
        h = _rms(x) * pre_gain + shift
        h_ref[rows, :] = h.astype(BF16)
        return fold(h)

    def epilogue(rows):
        base = o_ref[rows, :] if n_parts else x_ref[rows, :]
        out = base + _rms(acc_ref[rows, :]) * post_gain
        o_ref[rows, :] = out
        return fold(out)

    def fold(v):
        total = None
        for r0 in range(0, v.shape[0], V7X_SUBLANES):
            for c0 in range(0, v.shape[1], V7X_LANES):
                t = v[r0:r0 + V7X_SUBLANES, c0:c0 + V7X_LANES]
                total = t if total is None else total + t
        return total

    def piece(rows, p):
        return slice(rows.start + p * pm, rows.start + (p + 1) * pm)

    prologue(subs[0], mixer_out(subs[0]))
    for s, rows in enumerate(subs):
        nxt = subs[s + 1] if s + 1 < len(subs) else None
        y_next = mixer_out(nxt) if nxt is not None else None
        for j in range(n_chunks):
            c0, c1 = j * FFN_CHUNK, (j + 1) * FFN_CHUNK
            if 1 <= j <= pieces:
                anchors = []
                if s > 0:
                    anchors.append(epilogue(piece(subs[s - 1], j - 1)))
                if nxt is not None:
                    rp = piece(nxt, j - 1)
                    anchors.append(prologue(rp, None if y_next is None else y_next[rp.start - nxt.start:
                                                                                   rp.stop - nxt.start]))
                for t in anchors:
                    zero = jnp.minimum(t, 0.0) * jnp.maximum(t, 0.0)
                    acc_ref[rows.start:rows.start + V7X_SUBLANES, 0:V7X_LANES] += zero
            h = h_ref[rows, :]
            g = _mdot(h, wg_ref[:, c0:c1])
            u = _mdot(h, wu_ref[:, c0:c1])
            a = (g * jax.nn.sigmoid(g) * u).astype(BF16)
            part = _mdot(a, wd_ref[c0:c1, :])
            if j == 0:
                acc_ref[rows, :] = part
            else:
                acc_ref[rows, :] += part
    epilogue(subs[-1])


def _layer_weight(w, index):
    lead = len(index)
    return pl.BlockSpec((None,) * lead + w.shape[lead:], lambda *_: tuple(index) + (0, 0),
                        pipeline_mode=pl.Buffered(1))


def _ffn(x, mod_i, g, wg, wu, wd, layer, *, mrow, grow, batch_of, mixer=None, mix_mrow=None, mix_grow=None):
    m, d = x.shape
    f = wg.shape[-1]
    tm = min(FFN_ROW_TILE, m)
    hm = tm // FFN_SUBBLOCKS
    parts, w_out = mixer if mixer else ((), None)
    kern = functools.partial(_ffn_kernel, n_parts=len(parts), mix_mrow=mix_mrow, mix_grow=mix_grow,
                             mrow=mrow, grow=grow)
    part_bytes = sum(p.shape[1] * p.dtype.itemsize for p in parts)
    vmem = (3 * d * f * 2 + 4 * tm * d * 4 + tm * d * 4 + 6 * hm * FFN_CHUNK * 4 + hm * d * 12
            + 2 * tm * part_bytes + (d * d * 2 if parts else 0))
    mix_specs = [pl.BlockSpec((tm, p.shape[1]), lambda i: (i, 0)) for p in parts]
    if parts:
        mix_specs.append(_resident(w_out.shape))
    return pl.pallas_call(
        kern,
        grid=(m // tm,),
        in_specs=mix_specs + [
            pl.BlockSpec((tm, d), lambda i: (i, 0)),
            pl.BlockSpec((None, N_MOD, d), lambda i: (batch_of(i, tm), 0, 0)),
            _resident(g.shape), _layer_weight(wg, layer), _layer_weight(wu, layer), _layer_weight(wd, layer)],
        out_specs=pl.BlockSpec((tm, d), lambda i: (i, 0)),
        out_shape=jax.ShapeDtypeStruct((m, d), F32),
        scratch_shapes=[pltpu.VMEM((tm, d), F32), pltpu.VMEM((tm, d), BF16)],
        name="ffn_mix" if parts else "ffn",
        compiler_params=_params(("parallel",), vmem),
    )(*parts, *([w_out] if parts else []), x, mod_i, g, wg, wu, wd)


def _conv3_rows(pe, w0, w1, w2, tm):
    n = pe.shape[0]
    lo = V7X_SUBLANES
    down = pltpu.roll(pe, 1, 0)[lo:lo + tm]
    up = pltpu.roll(pe, n - 1, 0)[lo:lo + tm]
    return down * w0 + pe[lo:lo + tm] * w1 + up * w2


def _normed_with_halo(x_ref, prev_ref, next_ref, m_ref, g_ref, mrow, grow, seq_len):
    tm = x_ref.shape[0]
    lo = V7X_SUBLANES
    shift, scale = m_ref[mrow:mrow + 1, :], m_ref[mrow + 1:mrow + 2, :]
    xe = jnp.concatenate([prev_ref[...], x_ref[...], next_ref[...]], axis=0)
    he = (_rms(xe) * g_ref[grow:grow + 1, :]) * (1.0 + scale) + shift
    start = (pl.program_id(0) * tm) & (seq_len - 1)
    keep_prev = (start != 0).astype(F32)
    keep_next = (start + tm != seq_len).astype(F32)
    row = lax.broadcasted_iota(jnp.int32, (tm + 2 * lo, 1), 0)
    keep = jnp.where(row < lo, keep_prev, jnp.where(row >= lo + tm, keep_next, 1.0))
    return he[lo:lo + tm].astype(BF16), (he * keep).astype(BF16)


def _inproj_ab_kernel(x_ref, prev_ref, next_ref, m_ref, g_ref, w_ref, cw_ref, wvt_ref, qk_ref, vt_ref, *b_refs,
                      mrow, grow, seq_len):
    tm = x_ref.shape[0]
    h, he = _normed_with_halo(x_ref, prev_ref, next_ref, m_ref, g_ref, mrow, grow, seq_len)
    nqk = qk_ref.shape[1]
    nq = nqk + vt_ref.shape[0]
    for c in range(0, nqk, PROJ_CHUNK):
        qk_ref[:, c:c + PROJ_CHUNK] = _mdot(h, w_ref[:, c:c + PROJ_CHUNK]).astype(qk_ref.dtype)
    vt_ref[...] = lax.dot_general(wvt_ref[...], h, (((1,), (1,)), ((), ())),
                                  preferred_element_type=F32).astype(vt_ref.dtype)
    if b_refs:
        (b_ref,) = b_refs
        sw = cw_ref.shape[1]
        gate = _mdot(h, w_ref[:, nq:nq + sw])
        p = _mdot(he, w_ref[:, nq + sw:nq + 2 * sw]) * _mdot(he, w_ref[:, nq + 2 * sw:nq + 3 * sw])
        conv = _conv3_rows(p, cw_ref[0:1, :], cw_ref[1:2, :], cw_ref[2:3, :], tm)
        b_ref[...] = (gate * conv).astype(b_ref.dtype)


def _inproj_hyena_kernel(x_ref, prev_ref, next_ref, m_ref, g_ref, w_ref, cw_ref, x0_ref, u_ref,
                         *, mrow, grow, seq_len):
    tm = x_ref.shape[0]
    _, he = _normed_with_halo(x_ref, prev_ref, next_ref, m_ref, g_ref, mrow, grow, seq_len)
    c = x0_ref.shape[1]

    def conv_cols(c0):
        cols = slice(c0, c0 + PROJ_CHUNK)
        return _conv3_rows(_mdot(he, w_ref[:, cols]), cw_ref[0:1, cols], cw_ref[1:2, cols], cw_ref[2:3, cols], tm)

    for c0 in range(0, c, PROJ_CHUNK):
        x0_ref[:, c0:c0 + PROJ_CHUNK] = conv_cols(c0)
        u_ref[:, c0:c0 + PROJ_CHUNK] = conv_cols(2 * c + c0) * conv_cols(c + c0)


def _halo_specs(tm, m, width):
    blocks = m // V7X_SUBLANES
    per = tm // V7X_SUBLANES
    prev = pl.BlockSpec((V7X_SUBLANES, width), lambda i: (jnp.maximum(i * per - 1, 0), 0))
    nxt = pl.BlockSpec((V7X_SUBLANES, width), lambda i: (jnp.minimum((i + 1) * per, blocks - 1), 0))
    return prev, nxt


def _inproj_call(kern, name, x, mod_i, g, w, conv_w, outs, *, mrow, grow, seq_len, batch_of, extra=()):
    m, d = x.shape
    tm = min(ROW_TILE, seq_len)
    assert seq_len & (seq_len - 1) == 0 and seq_len % tm == 0 and m % seq_len == 0
    prev, nxt = _halo_specs(tm, m, d)
    out_bytes = sum(cols * jnp.dtype(dt).itemsize for cols, dt, _ in outs)
    vmem = w.size * 2 + 2 * tm * d * 4 + 2 * tm * out_bytes + 6 * tm * d * 4
    out_specs = [pl.BlockSpec((cols, tm), lambda i: (0, i)) if tr else pl.BlockSpec((tm, cols), lambda i: (i, 0))
                 for cols, _, tr in outs]
    out_shape = [jax.ShapeDtypeStruct((cols, m) if tr else (m, cols), dt) for cols, dt, tr in outs]
    return pl.pallas_call(
        functools.partial(kern, mrow=mrow, grow=grow, seq_len=seq_len),
        grid=(m // tm,),
        in_specs=[pl.BlockSpec((tm, d), lambda i: (i, 0)), prev, nxt,
                  pl.BlockSpec((None, N_MOD, d), lambda i: (batch_of(i, tm), 0, 0)),
                  _resident(g.shape), _resident(w.shape), _resident(conv_w.shape)]
        + [_resident(e.shape) for e in extra],
        out_specs=out_specs,
        out_shape=out_shape,
        name=name,
        compiler_params=_params(("parallel",), vmem),
    )(x, x, x, mod_i, g, w, conv_w, *extra)


def _split_heads_on_rows(q, low_lanes):
    zero = jnp.zeros_like(q)
    return jnp.concatenate([jnp.where(low_lanes, q, zero), jnp.where(low_lanes, zero, q)], axis=0)


def _na_kernel(pat_ref, q_ref, k_ref, vt_ref, kc_ref, vct_ref, bt_ref, o_ref, *, rows):
    rb = pl.program_id(2)
    step_rows = q_ref.shape[0] // GRID_W
    low_lanes = lax.broadcasted_iota(jnp.int32, (GRID_W, V7X_LANES), 1) < NA_HEAD_DIM
    kc = kc_ref[...]
    vct = vct_ref[...]
    nt = (((1,), (1,)), ((), ()))

    for t in range(step_rows // 2):
        r = rb * step_rows + 2 * t
        base = jnp.clip(r - NA_KH // 2, 0, rows - NA_SLAB_ROWS)
        k0 = pl.multiple_of(base * GRID_W, 2 * GRID_W)
        qa = q_ref[2 * t * GRID_W:(2 * t + 1) * GRID_W, :] * (NA_HEAD_DIM ** -0.5)
        qb = q_ref[(2 * t + 1) * GRID_W:(2 * t + 2) * GRID_W, :] * (NA_HEAD_DIM ** -0.5)
        q4 = jnp.concatenate([_split_heads_on_rows(qa, low_lanes), _split_heads_on_rows(qb, low_lanes)], axis=0)
        kk = k_ref[pl.ds(k0, NA_SLAB_ROWS * GRID_W), :]
        s = lax.dot_general(kk, q4, nt, preferred_element_type=F32) + bt_ref[pat_ref[r // 2]]
        sc = lax.dot_general(kc, q4, nt, preferred_element_type=F32)
        mx = jnp.maximum(jnp.max(s, axis=0, keepdims=True), jnp.max(sc, axis=0, keepdims=True))
        p = jnp.exp(s - mx)
        pc = jnp.exp(sc - mx)
        den = jnp.sum(p, axis=0, keepdims=True) + jnp.sum(pc, axis=0, keepdims=True)
        vv = vt_ref[:, pl.ds(k0, NA_SLAB_ROWS * GRID_W)]
        ot = (_mdot(vv, p.astype(BF16)) + _mdot(vct, pc.astype(BF16))) / den
        o4 = ot.T
        oa = jnp.where(low_lanes, o4[0:GRID_W], o4[GRID_W:2 * GRID_W])
        ob = jnp.where(low_lanes, o4[2 * GRID_W:3 * GRID_W], o4[3 * GRID_W:4 * GRID_W])
        o_ref[2 * t * GRID_W:(2 * t + 2) * GRID_W, :] = jnp.concatenate([oa, ob], axis=0).astype(o_ref.dtype)


def _na_window_row(r, rows):
    return min(max(r - NA_KH // 2, 0), rows - NA_KH)


def _na_bias_table(rpb, rows):
    h = rpb.shape[0]
    c = np.arange(GRID_W)[:, None]
    kc = np.arange(GRID_W)[None, :]
    start = np.clip(c - NA_KW // 2, 0, GRID_W - NA_KW)
    valid = (kc >= start) & (kc < start + NA_KW)
    onehot = ((kc - c + (NA_KW - 1))[:, :, None] == np.arange(2 * NA_KW - 1)) & valid[:, :, None]
    dense = jnp.einsum("hdx,ckx->hdkc", rpb, jnp.asarray(onehot, F32), precision=lax.Precision.HIGHEST)
    dense = dense + jnp.asarray(np.where(valid.T, 0.0, MASK_VALUE), F32)
    masked = jnp.full((h, GRID_W, GRID_W), MASK_VALUE, F32)
    patterns, ids = [], []
    for r in range(0, rows, 2):
        base = min(max(r - NA_KH // 2, 0), rows - NA_SLAB_ROWS)
        key = tuple((_na_window_row(r + s, rows) - base, _na_window_row(r + s, rows) - (r + s) + NA_KH - 1)
                    for s in (0, 1))
        if key not in patterns:
            patterns.append(key)
        ids.append(patterns.index(key))
    tabs = []
    for key in patterns:
        per_row = []
        for off, dy0 in key:
            blocks = [dense[:, dy0 + y - off] if off <= y < off + NA_KH else masked for y in range(NA_SLAB_ROWS)]
            per_row.append(jnp.concatenate(blocks, axis=1))
        t = jnp.stack(per_row, axis=1).reshape(h // 2, 2, 2, NA_SLAB_ROWS * GRID_W, GRID_W)
        tabs.append(t.transpose(0, 3, 2, 1, 4).reshape(h // 2, NA_SLAB_ROWS * GRID_W, 4 * GRID_W))
    return jnp.stack(tabs, axis=1), np.asarray(ids, np.int32)


def _neighbourhood_attention(qk, vt, qk_c, vt_c, rpb, batch, seq, ctx_len):
    rows = seq // GRID_W
    step_rows = min(NA_ROWS_PER_STEP, rows)
    assert rows >= NA_SLAB_ROWS and rows % step_rows == 0 and step_rows % 2 == 0
    nq = NA_WIDTH // V7X_LANES
    qk3 = qk.reshape(batch, seq, 2 * NA_WIDTH)
    qkc3 = qk_c.reshape(batch, ctx_len, 2 * NA_WIDTH)
    tq = step_rows * GRID_W
    table, pattern_ids = _na_bias_table(rpb, rows)
    vmem = 4 * seq * V7X_LANES * 2 + 2 * table[0].size * 4 + 16 * 1024 * 1024
    out = pl.pallas_call(
        functools.partial(_na_kernel, rows=rows),
        grid=(batch, nq, rows // step_rows),
        in_specs=[pl.BlockSpec(memory_space=pltpu.SMEM),
                  pl.BlockSpec((None, tq, V7X_LANES), lambda b, h, r: (b, r, h)),
                  pl.BlockSpec((None, seq, V7X_LANES), lambda b, h, r: (b, 0, nq + h)),
                  pl.BlockSpec((V7X_LANES, seq), lambda b, h, r: (h, b)),
                  pl.BlockSpec((None, ctx_len, V7X_LANES), lambda b, h, r: (b, 0, nq + h)),
                  pl.BlockSpec((V7X_LANES, ctx_len), lambda b, h, r: (h, b)),
                  pl.BlockSpec((None,) + table.shape[1:], lambda b, h, r: (h, 0, 0, 0))],
        out_specs=pl.BlockSpec((None, tq, V7X_LANES), lambda b, h, r: (b, r, h)),
        out_shape=jax.ShapeDtypeStruct((batch, seq, NA_WIDTH), BF16),
        name="na_attn",
        compiler_params=_params(("parallel", "parallel", "arbitrary"), vmem),
    )(jnp.asarray(pattern_ids), qk3, qk3, vt, qkc3, vt_c, table)
    return out.reshape(batch * seq, NA_WIDTH)


def _ctx_attn_kernel(q_ref, k_ref, vt_ref, o_ref):
    n = q_ref.shape[0]
    nt = (((1,), (1,)), ((), ()))
    low_lanes = lax.broadcasted_iota(jnp.int32, (n, V7X_LANES), 1) < NA_HEAD_DIM
    q2 = _split_heads_on_rows(q_ref[...] * (NA_HEAD_DIM ** -0.5), low_lanes)
    s = lax.dot_general(q2, k_ref[...], nt, preferred_element_type=F32)
    p = jnp.exp(s - jnp.max(s, axis=-1, keepdims=True))
    o2 = lax.dot_general(p.astype(BF16), vt_ref[...], nt, preferred_element_type=F32)
    o2 = o2 / jnp.sum(p, axis=-1, keepdims=True)
    o_ref[...] = jnp.where(low_lanes, o2[:n], o2[n:]).astype(o_ref.dtype)


def _context_attention(qk_c, vt_c, batch, ctx_len):
    nq = NA_WIDTH // V7X_LANES
    qkc3 = qk_c.reshape(batch, ctx_len, 2 * NA_WIDTH)
    out = pl.pallas_call(
        _ctx_attn_kernel,
        grid=(batch, nq),
        in_specs=[pl.BlockSpec((None, ctx_len, V7X_LANES), lambda b, h: (b, 0, h)),
                  pl.BlockSpec((None, ctx_len, V7X_LANES), lambda b, h: (b, 0, nq + h)),
                  pl.BlockSpec((V7X_LANES, ctx_len), lambda b, h: (h, b))],
        out_specs=pl.BlockSpec((None, ctx_len, V7X_LANES), lambda b, h: (b, 0, h)),
        out_shape=jax.ShapeDtypeStruct((batch, ctx_len, NA_WIDTH), BF16),
        name="ctx_attn",
        compiler_params=_params(("parallel", "parallel"), 0),
    )(qkc3, qkc3, vt_c)
    return out.reshape(batch * ctx_len, NA_WIDTH)


def _filter_kernel(z_ref, w1_ref, b1_ref, w2_ref, b2_ref, w3_ref, b3_ref, w4lo_ref, w4hi_ref, fr_ref, dec_ref,
                   o_ref, *, length):
    hi = lax.Precision.HIGHEST
    half = V7X_LANES // 2
    z = z_ref[...]
    fr = fr_ref[...]
    h = jnp.sin(fr * (jnp.dot(z, w1_ref[...], precision=hi, preferred_element_type=F32) + b1_ref[...]))
    h = jnp.sin(fr * (jnp.dot(h, w2_ref[...], precision=hi, preferred_element_type=F32) + b2_ref[...]))
    h = jnp.sin(fr * (jnp.dot(h, w3_ref[...], precision=hi, preferred_element_type=F32) + b3_ref[...]))
    hb = h.astype(BF16)
    n = z.shape[0]
    first = pl.program_id(0) * 2 * n
    for part, (w4_ref, t_lane) in enumerate(((w4lo_ref, 0), (w4hi_ref, half))):
        out = _mdot(hb, w4_ref[...].astype(BF16)) * jnp.exp(-z[:, t_lane:t_lane + 1] * dec_ref[...])
        row = first + part * n + lax.broadcasted_iota(jnp.int32, out.shape, 0)
        o_ref[part * n:(part + 1) * n, :] = jnp.where(row == length, 0.0, out)


@functools.lru_cache(maxsize=None)
def _filter_positions(length, block):
    half = V7X_LANES // 2
    t = np.linspace(0.0, 1.0, length)[:, None]
    w = 2.0 * math.pi * np.arange(length)[:, None] / length
    bands = np.linspace(1e-4, HY_BANDS - 1, HY_BANDS)[None, :]
    z = np.concatenate([t, np.cos(bands * w), -np.sin(bands * w)], axis=-1)
    z2 = np.concatenate([z, z[0:1], z[:0:-1]], axis=0)
    feat = np.zeros((2 * length, half), np.float32)
    feat[:, :HY_EMB] = z2
    feat = feat.reshape(2 * length // block, 2, block // 2, half)
    return np.ascontiguousarray(feat.transpose(0, 2, 1, 3)).reshape(length, V7X_LANES)


def _hyena_filter(length, w1, b1, w2, b2, w3, b3, w4, freq):
    c = w4.shape[1] // 2
    ffn = w2.shape[0]
    half = V7X_LANES // 2
    assert ffn == half and w1.shape[0] <= half
    tr = min(FILTER_ROWS, length)
    per_half = length // tr
    zpos = jnp.asarray(_filter_positions(length, tr))
    twice = lambda m: jnp.zeros((2 * m.shape[0], 2 * m.shape[1]), F32).at[:m.shape[0], :m.shape[1]].set(m).at[
        m.shape[0]:, m.shape[1]:].set(m)
    w1d = twice(jnp.pad(w1, ((0, half - w1.shape[0]), (0, 0))))
    w4lo = jnp.pad(w4, ((0, ffn), (0, 0)))
    w4hi = jnp.pad(w4, ((ffn, 0), (0, 0)))
    decay = np.abs(np.linspace(math.log(HY_TARGET) / HY_SLOW_PCT, math.log(HY_TARGET) / HY_FAST_PCT, c))
    decay = jnp.asarray(decay[None, :], F32)
    row2 = lambda v: jnp.tile(v.reshape(1, -1), (1, 2))
    small = lambda shape: pl.BlockSpec(shape, lambda i: (0, 0))
    mat, vec = small((V7X_LANES, V7X_LANES)), small((1, V7X_LANES))
    w4_spec = pl.BlockSpec((V7X_LANES, c), lambda i: (0, i // per_half))
    return pl.pallas_call(
        functools.partial(_filter_kernel, length=length),
        grid=(2 * per_half,),
        in_specs=[pl.BlockSpec((tr // 2, V7X_LANES), lambda i: (i, 0)),
                  mat, vec, mat, vec, mat, vec, w4_spec, w4_spec, vec, small((1, c))],
        out_specs=pl.BlockSpec((tr, c), lambda i: (i, 0)),
        out_shape=jax.ShapeDtypeStruct((2 * length, c), F32),
        name="hyena_filter",
        compiler_params=_params(("parallel",), 0),
    )(zpos, w1d, row2(b1), twice(w2), row2(b2), twice(w3), row2(b3), w4lo, w4hi, row2(freq), decay)


def _complex_as_real(m):
    return np.block([[m.real, -m.imag], [m.imag, m.real]])


@functools.lru_cache(maxsize=None)
def _dft_tables(length):
    n = 2 * length
    n1 = n // DFT_RADIX
    half = n1 // 2
    eye = np.eye(V7X_SUBLANES)
    k1 = np.arange(n1)
    f1 = np.exp(-2j * np.pi * np.outer(k1, k1) / n1)
    first_data = np.kron(_complex_as_real(f1[:, :half]), eye)
    first_filt = np.kron(np.concatenate([f1.real, f1.imag], axis=0), eye)
    inv1 = np.exp(2j * np.pi * np.outer(np.arange(half), k1) / n1) / n
    last = np.kron(_complex_as_real(inv1), eye)
    i2 = np.arange(DFT_RADIX)
    phase = (np.outer(i2, i2)[None] / DFT_RADIX + (i2[None, None, :] * k1[:, None, None]) / n)
    second = np.exp(-2j * np.pi * phase)
    fwd = np.stack([_complex_as_real(second[k]) for k in range(n1)])
    inv = np.stack([_complex_as_real(np.conj(second[k]).T) for k in range(n1)])
    return tuple(np.asarray(a, np.float32) for a in (first_data, first_filt, last, fwd, inv))


def _pack_pair(re, im):
    re_bits = lax.bitcast_convert_type(re.astype(BF16).astype(F32), jnp.uint32)
    im_bits = lax.bitcast_convert_type(im.astype(BF16).astype(F32), jnp.uint32)
    return (re_bits >> 16) | im_bits


def _unpack_pair(packed):
    re = lax.bitcast_convert_type(packed << 16, F32)
    im = lax.bitcast_convert_type(packed & jnp.uint32(0xFFFF0000), F32)
    return jnp.concatenate([re, im], axis=0).astype(BF16)


def _kron_kernel(mat_ref, x_ref, o_ref):
    lanes = x_ref.shape[-1]
    spec = _mdot(mat_ref[...], x_ref[...].reshape(-1, lanes).astype(BF16))
    rows = spec.shape[0] // 2
    o_ref[...] = _pack_pair(spec[:rows], spec[rows:]).reshape(o_ref.shape)


def _kron_out_kernel(mat_ref, b_ref, x0_ref, u_ref, bias_ref, o_ref):
    lanes = b_ref.shape[-1]
    y = _mdot(mat_ref[...], _unpack_pair(b_ref[...].reshape(-1, lanes))).reshape(o_ref.shape)
    o_ref[...] = x0_ref[...] * (y + u_ref[...] * bias_ref[...])


def _second_stage_kernel(a_ref, af_ref, g_ref, gi_ref, o_ref, kf_ref):
    r = DFT_RADIX
    steps = a_ref.shape[0]

    @pl.when(pl.program_id(2) == 0)
    def _():
        for j in range(steps):
            kf_ref[j] = _mdot(g_ref[j], _unpack_pair(af_ref[j]))

    for j in range(steps):
        spec = _mdot(g_ref[j], _unpack_pair(a_ref[j]))
        xr, xi = spec[:r], spec[r:]
        kr, ki = kf_ref[j, :r], kf_ref[j, r:]
        prod = jnp.concatenate([xr * kr - xi * ki, xr * ki + xi * kr], axis=0).astype(BF16)
        back = _mdot(gi_ref[j], prod)
        o_ref[j] = _pack_pair(back[:r], back[r:])


def _long_conv_gate(u, x0, kfilt, bias, batch, length):
    c = u.shape[1]
    pairs = batch // 2
    n1 = 2 * length // DFT_RADIX
    half = n1 // 2
    r, s8, cb, kb = DFT_RADIX, V7X_SUBLANES, min(DFT_LANES, c), min(DFT_K1_PER_STEP, n1)
    cb2 = min(V7X_MXU_DIM, c)
    assert batch % 2 == 0 and n1 % kb == 0 and c % cb == 0 and c % cb2 == 0
    first_data, first_filt, last, fwd, inv = (jnp.asarray(t, BF16) for t in _dft_tables(length))
    groups = r // s8
    big = 48 * 1024 * 1024

    u5 = u.reshape(pairs, 2, half, r, c)
    x05 = x0.reshape(pairs, 2, half, r, c)
    a_data = pl.pallas_call(
        _kron_kernel,
        grid=(pairs, groups, c // cb),
        in_specs=[_resident(first_data.shape),
                  pl.BlockSpec((None, 2, half, s8, cb), lambda p, g, l: (p, 0, 0, g, l))],
        out_specs=pl.BlockSpec((None, n1, s8, cb), lambda p, g, l: (p, 0, g, l)),
        out_shape=jax.ShapeDtypeStruct((pairs, n1, r, c), jnp.uint32),
        name="dft_first",
        compiler_params=_params(("parallel", "parallel", "parallel"), big),
    )(first_data, u5)

    a_filt = pl.pallas_call(
        _kron_kernel,
        grid=(groups, c // cb),
        in_specs=[_resident(first_filt.shape),
                  pl.BlockSpec((n1, s8, cb), lambda g, l: (0, g, l))],
        out_specs=pl.BlockSpec((n1, s8, cb), lambda g, l: (0, g, l)),
        out_shape=jax.ShapeDtypeStruct((n1, r, c), jnp.uint32),
        name="dft_first_filter",
        compiler_params=_params(("parallel", "parallel"), big),
    )(first_filt, kfilt.reshape(n1, r, c))

    b_data = pl.pallas_call(
        _second_stage_kernel,
        grid=(n1 // kb, c // cb2, pairs),
        in_specs=[pl.BlockSpec((None, kb, r, cb2), lambda k, l, p: (p, k, 0, l)),
                  pl.BlockSpec((kb, r, cb2), lambda k, l, p: (k, 0, l)),
                  pl.BlockSpec((kb, 2 * r, 2 * r), lambda k, l, p: (k, 0, 0)),
                  pl.BlockSpec((kb, 2 * r, 2 * r), lambda k, l, p: (k, 0, 0))],
        out_specs=pl.BlockSpec((None, kb, r, cb2), lambda k, l, p: (p, k, 0, l)),
        out_shape=jax.ShapeDtypeStruct((pairs, n1, r, c), jnp.uint32),
        scratch_shapes=[pltpu.VMEM((kb, 2 * r, cb2), F32)],
        name="dft_second",
        compiler_params=_params(("arbitrary", "arbitrary", "arbitrary"), big),
    )(a_data, a_filt, fwd, inv)

    z = pl.pallas_call(
        _kron_out_kernel,
        grid=(pairs, groups, c // cb),
        in_specs=[_resident(last.shape),
                  pl.BlockSpec((None, n1, s8, cb), lambda p, g, l: (p, 0, g, l)),
                  pl.BlockSpec((None, 2, half, s8, cb), lambda p, g, l: (p, 0, 0, g, l)),
                  pl.BlockSpec((None, 2, half, s8, cb), lambda p, g, l: (p, 0, 0, g, l)),
                  pl.BlockSpec((1, cb), lambda p, g, l: (0, l))],
        out_specs=pl.BlockSpec((None, 2, half, s8, cb), lambda p, g, l: (p, 0, 0, g, l)),
        out_shape=jax.ShapeDtypeStruct((pairs, 2, half, r, c), F32),
        name="dft_last",
        compiler_params=_params(("parallel", "parallel", "parallel"), big),
    )(last, b_data, x05, u5, bias.reshape(1, c))
    return z.reshape(batch * length, c)


@functools.lru_cache(maxsize=None)
def _dense_dft_tables(length):
    n = 2 * length
    idx = np.arange(n)
    f = np.exp(-2j * np.pi * np.outer(idx, idx) / n)
    fwd_data = _complex_as_real(f[:, :length])
    fwd_filt = np.concatenate([f.real, f.imag], axis=0)
    inv = _complex_as_real(np.conj(f[:length, :]) / n)
    return tuple(np.asarray(a, np.float32) for a in (fwd_data, fwd_filt, inv))


def _short_conv_gate_kernel(fd_ref, ff_ref, inv_ref, u_ref, x0_ref, k_ref, bias_ref, o_ref):
    n = k_ref.shape[0]
    u = u_ref[...]
    spec = _mdot(fd_ref[...], u.astype(BF16))
    kf = _mdot(ff_ref[...], k_ref[...].astype(BF16))
    xr, xi, kr, ki = spec[:n], spec[n:], kf[:n], kf[n:]
    prod = jnp.concatenate([xr * kr - xi * ki, xr * ki + xi * kr], axis=0).astype(BF16)
    o_ref[...] = x0_ref[...] * (_mdot(inv_ref[...], prod) + u * bias_ref[...])


def _long_conv_gate_short(u, x0, kfilt, bias, batch, length):
    c = u.shape[1]
    pairs = batch // 2
    cb = min(DFT_LANES, c)
    fwd_data, fwd_filt, inv = (jnp.asarray(t, BF16) for t in _dense_dft_tables(length))
    pair_spec = pl.BlockSpec((None, 2 * length, cb), lambda p, l: (p, 0, l))
    z = pl.pallas_call(
        _short_conv_gate_kernel,
        grid=(pairs, c // cb),
        in_specs=[_resident(fwd_data.shape), _resident(fwd_filt.shape), _resident(inv.shape),
                  pair_spec, pair_spec,
                  pl.BlockSpec((2 * length, cb), lambda p, l: (0, l)),
                  pl.BlockSpec((1, cb), lambda p, l: (0, l))],
        out_specs=pair_spec,
        out_shape=jax.ShapeDtypeStruct((pairs, 2 * length, c), F32),
        name="short_conv_gate",
        compiler_params=_params(("parallel", "parallel"), 0),
    )(fwd_data, fwd_filt, inv, u.reshape(pairs, 2 * length, c), x0.reshape(pairs, 2 * length, c),
      kfilt, bias.reshape(1, c))
    return z.reshape(batch * length, c)


def kernel(x, c, ctx, c_ctx, w_mod, b_mod, norm_g, ffn_w_gate, ffn_w_up, ffn_w_down, ab_w_in, na_rpb,
           sc_conv_w, ab_w_out, hy_w_in, hy_short_w, hy_f_w1, hy_f_b1, hy_f_w2, hy_f_b2, hy_f_w3, hy_f_b3,
           hy_f_w4, hy_sin_freq, hy_bias, hy_w_out):
    batch, seq, d = x.shape
    ctx_len = ctx.shape[1]
    depth = w_mod.shape[0]
    assert batch + 1 <= V7X_SUBLANES and seq % ROW_TILE == 0
    last_attn = (depth - 1) - (depth - 1) % 2
    sc_width = sc_conv_w.shape[-1]

    cc = jnp.zeros((V7X_SUBLANES, d), F32).at[:batch].set(c).at[batch].set(c_ctx)
    mod = _modulation(cc, w_mod, b_mod).reshape(depth, V7X_SUBLANES, N_MOD, d)
    lat_batch = lambda i, tm: i // (seq // tm)
    ctx_batch = lambda i, tm: batch

    xs = x.reshape(batch * seq, d)
    cs = ctx.reshape(batch * ctx_len, d)
    ffn_w = (ffn_w_gate.astype(BF16), ffn_w_up.astype(BF16), ffn_w_down.astype(BF16))

    for i in range(depth):
        j = i // 2
        g = norm_g[i]
        m_i = mod[i]
        keep_ctx = i <= last_attn
        upd_ctx = i < last_attn

        xs = _ffn(xs, m_i, g, *ffn_w, (i, 0), mrow=0, grow=0, batch_of=lat_batch)
        if keep_ctx:
            cs = _ffn(cs, m_i, g, *ffn_w, (i, 0), mrow=0, grow=0, batch_of=ctx_batch)

        if i % 2 == 0:
            w_in = ab_w_in[j].astype(BF16)
            w_out = ab_w_out[j].astype(BF16)
            ab_outs = [(2 * NA_WIDTH, BF16, False), (NA_WIDTH, BF16, True), (sc_width, BF16, False)]
            w_vt = w_in[:, 2 * NA_WIDTH:3 * NA_WIDTH].T
            in_ab = functools.partial(_inproj_call, _inproj_ab_kernel, "inproj_ab", mrow=3, grow=2, extra=(w_vt,))
            qk, vt, b_lat = in_ab(xs, m_i, g, w_in, sc_conv_w[j], ab_outs, seq_len=seq, batch_of=lat_batch)
            if upd_ctx:
                qk_c, vt_c, b_ctx = in_ab(cs, m_i, g, w_in, sc_conv_w[j], ab_outs, seq_len=ctx_len,
                                          batch_of=ctx_batch)
            else:
                qk_c, vt_c = in_ab(cs, m_i, g, w_in, sc_conv_w[j], ab_outs[:2], seq_len=ctx_len,
                                   batch_of=ctx_batch)
            a_lat = _neighbourhood_attention(qk, vt, qk_c, vt_c, na_rpb[j], batch, seq, ctx_len)
            mix_lat = ([a_lat, b_lat], w_out)
            if upd_ctx:
                mix_ctx = ([_context_attention(qk_c, vt_c, batch, ctx_len), b_ctx], w_out)
        else:
            w_in = hy_w_in[j].astype(BF16)
            w_out = hy_w_out[j].astype(BF16)
            hw = w_out.shape[0]
            filt = lambda n: _hyena_filter(n, hy_f_w1[j], hy_f_b1[j], hy_f_w2[j], hy_f_b2[j], hy_f_w3[j],
                                           hy_f_b3[j], hy_f_w4[j], hy_sin_freq[j])
            in_hy = functools.partial(_inproj_call, _inproj_hyena_kernel, "inproj_hyena", mrow=3, grow=2)
            x0, u = in_hy(xs, m_i, g, w_in, hy_short_w[j], [(hw, F32, False)] * 2, seq_len=seq,
                          batch_of=lat_batch)
            mix_lat = ([_long_conv_gate(u, x0, filt(seq), hy_bias[j], batch, seq)], w_out)
            if upd_ctx:
                x0_c, u_c = in_hy(cs, m_i, g, w_in, hy_short_w[j], [(hw, F32, False)] * 2, seq_len=ctx_len,
                                  batch_of=ctx_batch)
                mix_ctx = ([_long_conv_gate_short(u_c, x0_c, filt(ctx_len), hy_bias[j], batch, ctx_len)], w_out)

        xs = _ffn(xs, m_i, g, *ffn_w, (i, 1), mrow=6, grow=4, batch_of=lat_batch, mixer=mix_lat,
                  mix_mrow=5, mix_grow=3)
        if upd_ctx:
            cs = _ffn(cs, m_i, g, *ffn_w, (i, 1), mrow=6, grow=4, batch_of=ctx_batch, mixer=mix_ctx,
                      mix_mrow=5, mix_grow=3)

    return xs.reshape(batch, seq, d)
```

```python
import functools
import math

import numpy as np
import jax
import jax.numpy as jnp
from jax import lax
from jax.experimental import pallas as pl
from jax.experimental.pallas import tpu as pltpu

F32 = jnp.float32
BF16 = jnp.bfloat16

GRID_W = 64
NA_HEADS = 8
NA_HEAD_DIM = 64
NA_WIDTH = NA_HEADS * NA_HEAD_DIM
NA_KH = 8
NA_KW = 16
FFN_RES = 0.5
N_MOD = 9
RMS_EPS = 1e-6
HY_EMB = 33
HY_BANDS = (HY_EMB - 1) // 2
HY_TARGET = 1e-2
HY_FAST_PCT = 0.3
HY_SLOW_PCT = 1.5

V7X_LANES = 128
V7X_SUBLANES = 8
V7X_MXU_DIM = 256
V7X_VMEM_BYTES = 64 * 1024 * 1024

ROW_TILE = 1024
FFN_CHUNK = 256
FFN_ROW_TILE = 1024
FFN_SUBBLOCKS = 2
FFN_SIDE_PIECES = 8
PROJ_CHUNK = 256
NA_ROWS_PER_STEP = 128
NA_SLAB_ROWS = NA_KH + 2
DFT_RADIX = 128
DFT_LANES = 1024
DFT_K1_PER_STEP = 16
FILTER_ROWS = 1024
MASK_VALUE = -1e30


def _vmem_limit(nbytes):
    return int(min(max(nbytes, 32 * 1024 * 1024), V7X_VMEM_BYTES - 8 * 1024 * 1024))


def _params(semantics, vmem_bytes):
    return pltpu.CompilerParams(dimension_semantics=semantics, vmem_limit_bytes=_vmem_limit(vmem_bytes))


def _resident(shape):
    zeros = (0,) * len(shape)
    return pl.BlockSpec(shape, lambda *_: zeros, pipeline_mode=pl.Buffered(1))


def _rms(x):
    return x * lax.rsqrt(jnp.mean(x * x, axis=-1, keepdims=True) + RMS_EPS)


def _mdot(a, b):
    return jnp.dot(a, b, preferred_element_type=F32)


def _mod_kernel(c_ref, w_ref, b_ref, o_ref):
    c = c_ref[...]
    s = (c * jax.nn.sigmoid(c)).astype(BF16)
    o_ref[...] = _mdot(s, w_ref[...].astype(BF16)) + b_ref[...]


def _modulation(cc, w_mod, b_mod):
    depth, d, nd = w_mod.shape
    return pl.pallas_call(
        _mod_kernel,
        grid=(depth, nd // d),
        in_specs=[pl.BlockSpec((V7X_SUBLANES, d), lambda i, j: (0, 0)),
                  pl.BlockSpec((None, d, d), lambda i, j: (i, 0, j)),
                  pl.BlockSpec((None, 1, d), lambda i, j: (i, 0, j))],
        out_specs=pl.BlockSpec((None, V7X_SUBLANES, d), lambda i, j: (i, 0, j)),
        out_shape=jax.ShapeDtypeStruct((depth, V7X_SUBLANES, nd), F32),
        name="modulation",
        compiler_params=_params(("parallel", "parallel"), 4 * d * d * 4),
    )(cc, w_mod, b_mod.reshape(depth, 1, nd))


def _ffn_kernel(*refs, n_parts, mix_mrow, mix_grow, mrow, grow):
    part_refs = refs[:n_parts]
    w_out_ref = refs[n_parts] if n_parts else None
    x_ref, m_ref, g_ref, wg_ref, wu_ref, wd_ref, o_ref, acc_ref, h_ref = refs[n_parts + bool(n_parts):]
    shift, scale, gate = m_ref[mrow:mrow + 1, :], m_ref[mrow + 1:mrow + 2, :], m_ref[mrow + 2:mrow + 3, :]
    pre_gain = g_ref[grow:grow + 1, :] * (1.0 + scale)
    post_gain = (FFN_RES * gate) * g_ref[grow + 1:grow + 2, :]
    hm = x_ref.shape[0] // FFN_SUBBLOCKS
    subs = [slice(s * hm, (s + 1) * hm) for s in range(FFN_SUBBLOCKS)]
    n_chunks = wg_ref.shape[1] // FFN_CHUNK

    pieces = min(FFN_SIDE_PIECES, n_chunks - 1)
    pm = hm // pieces
    tile = (slice(0, V7X_SUBLANES), slice(0, V7X_LANES))

    def mixer_out(rows):
        if not n_parts:
            return None
        y, r0 = None, 0
        for p_ref in part_refs:
            r1 = r0 + p_ref.shape[1]
            term = _mdot(p_ref[rows, :].astype(BF16), w_out_ref[r0:r1, :])
            y = term if y is None else y + term
            r0 = r1
        return y

    def prologue(rows, y):
        x = x_ref[rows, :]
        if n_parts:
            x = x + m_ref[mix_mrow:mix_mrow + 1, :] * (_rms(y) * g_ref[mix_grow:mix_grow + 1, :])
            o_ref[rows, :] = x
        h = _rms(x) * pre_gain + shift
        h_ref[rows, :] = h.astype(BF16)
        return fold(h)

    def epilogue(rows):
        base = o_ref[rows, :] if n_parts else x_ref[rows, :]
        out = base + _rms(acc_ref[rows, :]) * post_gain
        o_ref[rows, :] = out
        return fold(out)

    def fold(v):
        total = None
        for r0 in range(0, v.shape[0], V7X_SUBLANES):
            for c0 in range(0, v.shape[1], V7X_LANES):
                t = v[r0:r0 + V7X_SUBLANES, c0:c0 + V7X_LANES]
                total = t if total is None else total + t
        return total

    def piece(rows, p):
        return slice(rows.start + p * pm, rows.start + (p + 1) * pm)

    prologue(subs[0], mixer_out(subs[0]))
    for s, rows in enumerate(subs):
        nxt = subs[s + 1] if s + 1 < len(subs) else None
        y_next = mixer_out(nxt) if nxt is not None else None
        for j in range(n_chunks):
            c0, c1 = j * FFN_CHUNK, (j + 1) * FFN_CHUNK
            if 1 <= j <= pieces:
                anchors = []
                if s > 0:
                    anchors.append(epilogue(piece(subs[s - 1], j - 1)))
                if nxt is not None:
                    rp = piece(nxt, j - 1)
                    anchors.append(prologue(rp, None if y_next is None else y_next[rp.start - nxt.start:
                                                                                   rp.stop - nxt.start]))
                for t in anchors:
                    zero = jnp.minimum(t, 0.0) * jnp.maximum(t, 0.0)
                    acc_ref[rows.start:rows.start + V7X_SUBLANES, 0:V7X_LANES] += zero
            h = h_ref[rows, :]
            g = _mdot(h, wg_ref[:, c0:c1])
            u = _mdot(h, wu_ref[:, c0:c1])
            a = (g * jax.nn.sigmoid(g) * u).astype(BF16)
            part = _mdot(a, wd_ref[c0:c1, :])
            if j == 0:
                acc_ref[rows, :] = part
            else:
                acc_ref[rows, :] += part
    epilogue(subs[-1])


def _layer_weight(w, index):
    lead = len(index)
    return pl.BlockSpec((None,) * lead + w.shape[lead:], lambda *_: tuple(index) + (0, 0),
                        pipeline_mode=pl.Buffered(1))


def _ffn(x, mod_i, g, wg, wu, wd, layer, *, mrow, grow, batch_of, mixer=None, mix_mrow=None, mix_grow=None):
    m, d = x.shape
    f = wg.shape[-1]
    tm = min(FFN_ROW_TILE, m)
    hm = tm // FFN_SUBBLOCKS
    parts, w_out = mixer if mixer else ((), None)
    kern = functools.partial(_ffn_kernel, n_parts=len(parts), mix_mrow=mix_mrow, mix_grow=mix_grow,
                             mrow=mrow, grow=grow)
    part_bytes = sum(p.shape[1] * p.dtype.itemsize for p in parts)
    vmem = (3 * d * f * 2 + 4 * tm * d * 4 + tm * d * 4 + 6 * hm * FFN_CHUNK * 4 + hm * d * 12
            + 2 * tm * part_bytes + (d * d * 2 if parts else 0))
    mix_specs = [pl.BlockSpec((tm, p.shape[1]), lambda i: (i, 0)) for p in parts]
    if parts:
        mix_specs.append(_resident(w_out.shape))
    return pl.pallas_call(
        kern,
        grid=(m // tm,),
        in_specs=mix_specs + [
            pl.BlockSpec((tm, d), lambda i: (i, 0)),
            pl.BlockSpec((None, N_MOD, d), lambda i: (batch_of(i, tm), 0, 0)),
            _resident(g.shape), _layer_weight(wg, layer), _layer_weight(wu, layer), _layer_weight(wd, layer)],
        out_specs=pl.BlockSpec((tm, d), lambda i: (i, 0)),
        out_shape=jax.ShapeDtypeStruct((m, d), F32),
        scratch_shapes=[pltpu.VMEM((tm, d), F32), pltpu.VMEM((tm, d), BF16)],
        name="ffn_mix" if parts else "ffn",
        compiler_params=_params(("parallel",), vmem),
    )(*parts, *([w_out] if parts else []), x, mod_i, g, wg, wu, wd)


def _conv3_rows(pe, w0, w1, w2, tm):
    n = pe.shape[0]
    lo = V7X_SUBLANES
    down = pltpu.roll(pe, 1, 0)[lo:lo + tm]
    up = pltpu.roll(pe, n - 1, 0)[lo:lo + tm]
    return down * w0 + pe[lo:lo + tm] * w1 + up * w2


def _normed_with_halo(x_ref, prev_ref, next_ref, m_ref, g_ref, mrow, grow, seq_len):
    tm = x_ref.shape[0]
    lo = V7X_SUBLANES
    shift, scale = m_ref[mrow:mrow + 1, :], m_ref[mrow + 1:mrow + 2, :]
    xe = jnp.concatenate([prev_ref[...], x_ref[...], next_ref[...]], axis=0)
    he = (_rms(xe) * g_ref[grow:grow + 1, :]) * (1.0 + scale) + shift
    start = (pl.program_id(0) * tm) & (seq_len - 1)
    keep_prev = (start != 0).astype(F32)
    keep_next = (start + tm != seq_len).astype(F32)
    row = lax.broadcasted_iota(jnp.int32, (tm + 2 * lo, 1), 0)
    keep = jnp.where(row < lo, keep_prev, jnp.where(row >= lo + tm, keep_next, 1.0))
    return he[lo:lo + tm].astype(BF16), (he * keep).astype(BF16)


def _inproj_ab_kernel(x_ref, prev_ref, next_ref, m_ref, g_ref, w_ref, cw_ref, wvt_ref, qk_ref, vt_ref, *b_refs,
                      mrow, grow, seq_len):
    tm = x_ref.shape[0]
    h, he = _normed_with_halo(x_ref, prev_ref, next_ref, m_ref, g_ref, mrow, grow, seq_len)
    nqk = qk_ref.shape[1]
    nq = nqk + vt_ref.shape[0]
    for c in range(0, nqk, PROJ_CHUNK):
        qk_ref[:, c:c + PROJ_CHUNK] = _mdot(h, w_ref[:, c:c + PROJ_CHUNK]).astype(qk_ref.dtype)
    vt_ref[...] = lax.dot_general(wvt_ref[...], h, (((1,), (1,)), ((), ())),
                                  preferred_element_type=F32).astype(vt_ref.dtype)
    if b_refs:
        (b_ref,) = b_refs
        sw = cw_ref.shape[1]
        gate = _mdot(h, w_ref[:, nq:nq + sw])
        p = _mdot(he, w_ref[:, nq + sw:nq + 2 * sw]) * _mdot(he, w_ref[:, nq + 2 * sw:nq + 3 * sw])
        conv = _conv3_rows(p, cw_ref[0:1, :], cw_ref[1:2, :], cw_ref[2:3, :], tm)
        b_ref[...] = (gate * conv).astype(b_ref.dtype)


def _inproj_hyena_kernel(x_ref, prev_ref, next_ref, m_ref, g_ref, w_ref, cw_ref, x0_ref, u_ref,
                         *, mrow, grow, seq_len):
    tm = x_ref.shape[0]
    _, he = _normed_with_halo(x_ref, prev_ref, next_ref, m_ref, g_ref, mrow, grow, seq_len)
    c = x0_ref.shape[1]

    def conv_cols(c0):
        cols = slice(c0, c0 + PROJ_CHUNK)
        return _conv3_rows(_mdot(he, w_ref[:, cols]), cw_ref[0:1, cols], cw_ref[1:2, cols], cw_ref[2:3, cols], tm)

    for c0 in range(0, c, PROJ_CHUNK):
        x0_ref[:, c0:c0 + PROJ_CHUNK] = conv_cols(c0)
        u_ref[:, c0:c0 + PROJ_CHUNK] = conv_cols(2 * c + c0) * conv_cols(c + c0)


def _halo_specs(tm, m, width):
    blocks = m // V7X_SUBLANES
    per = tm // V7X_SUBLANES
    prev = pl.BlockSpec((V7X_SUBLANES, width), lambda i: (jnp.maximum(i * per - 1, 0), 0))
    nxt = pl.BlockSpec((V7X_SUBLANES, width), lambda i: (jnp.minimum((i + 1) * per, blocks - 1), 0))
    return prev, nxt


def _inproj_call(kern, name, x, mod_i, g, w, conv_w, outs, *, mrow, grow, seq_len, batch_of, extra=()):
    m, d = x.shape
    tm = min(ROW_TILE, seq_len)
    assert seq_len & (seq_len - 1) == 0 and seq_len % tm == 0 and m % seq_len == 0
    prev, nxt = _halo_specs(tm, m, d)
    out_bytes = sum(cols * jnp.dtype(dt).itemsize for cols, dt, _ in outs)
    vmem = w.size * 2 + 2 * tm * d * 4 + 2 * tm * out_bytes + 6 * tm * d * 4
    out_specs = [pl.BlockSpec((cols, tm), lambda i: (0, i)) if tr else pl.BlockSpec((tm, cols), lambda i: (i, 0))
                 for cols, _, tr in outs]
    out_shape = [jax.ShapeDtypeStruct((cols, m) if tr else (m, cols), dt) for cols, dt, tr in outs]
    return pl.pallas_call(
        functools.partial(kern, mrow=mrow, grow=grow, seq_len=seq_len),
        grid=(m // tm,),
        in_specs=[pl.BlockSpec((tm, d), lambda i: (i, 0)), prev, nxt,
                  pl.BlockSpec((None, N_MOD, d), lambda i: (batch_of(i, tm), 0, 0)),
                  _resident(g.shape), _resident(w.shape), _resident(conv_w.shape)]
        + [_resident(e.shape) for e in extra],
        out_specs=out_specs,
        out_shape=out_shape,
        name=name,
        compiler_params=_params(("parallel",), vmem),
    )(x, x, x, mod_i, g, w, conv_w, *extra)


def _split_heads_on_rows(q, low_lanes):
    zero = jnp.zeros_like(q)
    return jnp.concatenate([jnp.where(low_lanes, q, zero), jnp.where(low_lanes, zero, q)], axis=0)


def _na_kernel(pat_ref, q_ref, k_ref, vt_ref, kc_ref, vct_ref, bt_ref, o_ref, *, rows):
    rb = pl.program_id(2)
    step_rows = q_ref.shape[0] // GRID_W
    low_lanes = lax.broadcasted_iota(jnp.int32, (GRID_W, V7X_LANES), 1) < NA_HEAD_DIM
    kc = kc_ref[...]
    vct = vct_ref[...]
    nt = (((1,), (1,)), ((), ()))

    for t in range(step_rows // 2):
        r = rb * step_rows + 2 * t
        base = jnp.clip(r - NA_KH // 2, 0, rows - NA_SLAB_ROWS)
        k0 = pl.multiple_of(base * GRID_W, 2 * GRID_W)
        qa = q_ref[2 * t * GRID_W:(2 * t + 1) * GRID_W, :] * (NA_HEAD_DIM ** -0.5)
        qb = q_ref[(2 * t + 1) * GRID_W:(2 * t + 2) * GRID_W, :] * (NA_HEAD_DIM ** -0.5)
        q4 = jnp.concatenate([_split_heads_on_rows(qa, low_lanes), _split_heads_on_rows(qb, low_lanes)], axis=0)
        kk = k_ref[pl.ds(k0, NA_SLAB_ROWS * GRID_W), :]
        s = lax.dot_general(kk, q4, nt, preferred_element_type=F32) + bt_ref[pat_ref[r // 2]]
        sc = lax.dot_general(kc, q4, nt, preferred_element_type=F32)
        mx = jnp.maximum(jnp.max(s, axis=0, keepdims=True), jnp.max(sc, axis=0, keepdims=True))
        p = jnp.exp(s - mx)
        pc = jnp.exp(sc - mx)
        den = jnp.sum(p, axis=0, keepdims=True) + jnp.sum(pc, axis=0, keepdims=True)
        vv = vt_ref[:, pl.ds(k0, NA_SLAB_ROWS * GRID_W)]
        ot = (_mdot(vv, p.astype(BF16)) + _mdot(vct, pc.astype(BF16))) / den
        o4 = ot.T
        oa = jnp.where(low_lanes, o4[0:GRID_W], o4[GRID_W:2 * GRID_W])
        ob = jnp.where(low_lanes, o4[2 * GRID_W:3 * GRID_W], o4[3 * GRID_W:4 * GRID_W])
        o_ref[2 * t * GRID_W:(2 * t + 2) * GRID_W, :] = jnp.concatenate([oa, ob], axis=0).astype(o_ref.dtype)


def _na_window_row(r, rows):
    return min(max(r - NA_KH // 2, 0), rows - NA_KH)


def _na_bias_table(rpb, rows):
    h = rpb.shape[0]
    c = np.arange(GRID_W)[:, None]
    kc = np.arange(GRID_W)[None, :]
    start = np.clip(c - NA_KW // 2, 0, GRID_W - NA_KW)
    valid = (kc >= start) & (kc < start + NA_KW)
    onehot = ((kc - c + (NA_KW - 1))[:, :, None] == np.arange(2 * NA_KW - 1)) & valid[:, :, None]
    dense = jnp.einsum("hdx,ckx->hdkc", rpb, jnp.asarray(onehot, F32), precision=lax.Precision.HIGHEST)
    dense = dense + jnp.asarray(np.where(valid.T, 0.0, MASK_VALUE), F32)
    masked = jnp.full((h, GRID_W, GRID_W), MASK_VALUE, F32)
    patterns, ids = [], []
    for r in range(0, rows, 2):
        base = min(max(r - NA_KH // 2, 0), rows - NA_SLAB_ROWS)
        key = tuple((_na_window_row(r + s, rows) - base, _na_window_row(r + s, rows) - (r + s) + NA_KH - 1)
                    for s in (0, 1))
        if key not in patterns:
            patterns.append(key)
        ids.append(patterns.index(key))
    tabs = []
    for key in patterns:
        per_row = []
        for off, dy0 in key:
            blocks = [dense[:, dy0 + y - off] if off <= y < off + NA_KH else masked for y in range(NA_SLAB_ROWS)]
            per_row.append(jnp.concatenate(blocks, axis=1))
        t = jnp.stack(per_row, axis=1).reshape(h // 2, 2, 2, NA_SLAB_ROWS * GRID_W, GRID_W)
        tabs.append(t.transpose(0, 3, 2, 1, 4).reshape(h // 2, NA_SLAB_ROWS * GRID_W, 4 * GRID_W))
    return jnp.stack(tabs, axis=1), np.asarray(ids, np.int32)


def _neighbourhood_attention(qk, vt, qk_c, vt_c, rpb, batch, seq, ctx_len):
    rows = seq // GRID_W
    step_rows = min(NA_ROWS_PER_STEP, rows)
    assert rows >= NA_SLAB_ROWS and rows % step_rows == 0 and step_rows % 2 == 0
    nq = NA_WIDTH // V7X_LANES
    qk3 = qk.reshape(batch, seq, 2 * NA_WIDTH)
    qkc3 = qk_c.reshape(batch, ctx_len, 2 * NA_WIDTH)
    tq = step_rows * GRID_W
    table, pattern_ids = _na_bias_table(rpb, rows)
    vmem = 4 * seq * V7X_LANES * 2 + 2 * table[0].size * 4 + 16 * 1024 * 1024
    out = pl.pallas_call(
        functools.partial(_na_kernel, rows=rows),
        grid=(batch, nq, rows // step_rows),
        in_specs=[pl.BlockSpec(memory_space=pltpu.SMEM),
                  pl.BlockSpec((None, tq, V7X_LANES), lambda b, h, r: (b, r, h)),
                  pl.BlockSpec((None, seq, V7X_LANES), lambda b, h, r: (b, 0, nq + h)),
                  pl.BlockSpec((V7X_LANES, seq), lambda b, h, r: (h, b)),
                  pl.BlockSpec((None, ctx_len, V7X_LANES), lambda b, h, r: (b, 0, nq + h)),
                  pl.BlockSpec((V7X_LANES, ctx_len), lambda b, h, r: (h, b)),
                  pl.BlockSpec((None,) + table.shape[1:], lambda b, h, r: (h, 0, 0, 0))],
        out_specs=pl.BlockSpec((None, tq, V7X_LANES), lambda b, h, r: (b, r, h)),
        out_shape=jax.ShapeDtypeStruct((batch, seq, NA_WIDTH), BF16),
        name="na_attn",
        compiler_params=_params(("parallel", "parallel", "arbitrary"), vmem),
    )(jnp.asarray(pattern_ids), qk3, qk3, vt, qkc3, vt_c, table)
    return out.reshape(batch * seq, NA_WIDTH)


def _ctx_attn_kernel(q_ref, k_ref, vt_ref, o_ref):
    n = q_ref.shape[0]
    nt = (((1,), (1,)), ((), ()))
    low_lanes = lax.broadcasted_iota(jnp.int32, (n, V7X_LANES), 1) < NA_HEAD_DIM
    q2 = _split_heads_on_rows(q_ref[...] * (NA_HEAD_DIM ** -0.5), low_lanes)
    s = lax.dot_general(q2, k_ref[...], nt, preferred_element_type=F32)
    p = jnp.exp(s - jnp.max(s, axis=-1, keepdims=True))
    o2 = lax.dot_general(p.astype(BF16), vt_ref[...], nt, preferred_element_type=F32)
    o2 = o2 / jnp.sum(p, axis=-1, keepdims=True)
    o_ref[...] = jnp.where(low_lanes, o2[:n], o2[n:]).astype(o_ref.dtype)


def _context_attention(qk_c, vt_c, batch, ctx_len):
    nq = NA_WIDTH // V7X_LANES
    qkc3 = qk_c.reshape(batch, ctx_len, 2 * NA_WIDTH)
    out = pl.pallas_call(
        _ctx_attn_kernel,
        grid=(batch, nq),
        in_specs=[pl.BlockSpec((None, ctx_len, V7X_LANES), lambda b, h: (b, 0, h)),
                  pl.BlockSpec((None, ctx_len, V7X_LANES), lambda b, h: (b, 0, nq + h)),
                  pl.BlockSpec((V7X_LANES, ctx_len), lambda b, h: (h, b))],
        out_specs=pl.BlockSpec((None, ctx_len, V7X_LANES), lambda b, h: (b, 0, h)),
        out_shape=jax.ShapeDtypeStruct((batch, ctx_len, NA_WIDTH), BF16),
        name="ctx_attn",
        compiler_params=_params(("parallel", "parallel"), 0),
    )(qkc3, qkc3, vt_c)
    return out.reshape(batch * ctx_len, NA_WIDTH)


def _filter_kernel(z_ref, w1_ref, b1_ref, w2_ref, b2_ref, w3_ref, b3_ref, w4lo_ref, w4hi_ref, fr_ref, dec_ref,
                   o_ref, *, length):
    hi = lax.Precision.HIGHEST
    half = V7X_LANES // 2
    z = z_ref[...]
    fr = fr_ref[...]
    h = jnp.sin(fr * (jnp.dot(z, w1_ref[...], precision=hi, preferred_element_type=F32) + b1_ref[...]))
    h = jnp.sin(fr * (jnp.dot(h, w2_ref[...], precision=hi, preferred_element_type=F32) + b2_ref[...]))
    h = jnp.sin(fr * (jnp.dot(h, w3_ref[...], precision=hi, preferred_element_type=F32) + b3_ref[...]))
    hb = h.astype(BF16)
    n = z.shape[0]
    first = pl.program_id(0) * 2 * n
    for part, (w4_ref, t_lane) in enumerate(((w4lo_ref, 0), (w4hi_ref, half))):
        out = _mdot(hb, w4_ref[...].astype(BF16)) * jnp.exp(-z[:, t_lane:t_lane + 1] * dec_ref[...])
        row = first + part * n + lax.broadcasted_iota(jnp.int32, out.shape, 0)
        o_ref[part * n:(part + 1) * n, :] = jnp.where(row == length, 0.0, out)


@functools.lru_cache(maxsize=None)
def _filter_positions(length, block):
    half = V7X_LANES // 2
    t = np.linspace(0.0, 1.0, length)[:, None]
    w = 2.0 * math.pi * np.arange(length)[:, None] / length
    bands = np.linspace(1e-4, HY_BANDS - 1, HY_BANDS)[None, :]
    z = np.concatenate([t, np.cos(bands * w), -np.sin(bands * w)], axis=-1)
    z2 = np.concatenate([z, z[0:1], z[:0:-1]], axis=0)
    feat = np.zeros((2 * length, half), np.float32)
    feat[:, :HY_EMB] = z2
    feat = feat.reshape(2 * length // block, 2, block // 2, half)
    return np.ascontiguousarray(feat.transpose(0, 2, 1, 3)).reshape(length, V7X_LANES)


def _hyena_filter(length, w1, b1, w2, b2, w3, b3, w4, freq):
    c = w4.shape[1] // 2
    ffn = w2.shape[0]
    half = V7X_LANES // 2
    assert ffn == half and w1.shape[0] <= half
    tr = min(FILTER_ROWS, length)
    per_half = length // tr
    zpos = jnp.asarray(_filter_positions(length, tr))
    twice = lambda m: jnp.zeros((2 * m.shape[0], 2 * m.shape[1]), F32).at[:m.shape[0], :m.shape[1]].set(m).at[
        m.shape[0]:, m.shape[1]:].set(m)
    w1d = twice(jnp.pad(w1, ((0, half - w1.shape[0]), (0, 0))))
    w4lo = jnp.pad(w4, ((0, ffn), (0, 0)))
    w4hi = jnp.pad(w4, ((ffn, 0), (0, 0)))
    decay = np.abs(np.linspace(math.log(HY_TARGET) / HY_SLOW_PCT, math.log(HY_TARGET) / HY_FAST_PCT, c))
    decay = jnp.asarray(decay[None, :], F32)
    row2 = lambda v: jnp.tile(v.reshape(1, -1), (1, 2))
    small = lambda shape: pl.BlockSpec(shape, lambda i: (0, 0))
    mat, vec = small((V7X_LANES, V7X_LANES)), small((1, V7X_LANES))
    w4_spec = pl.BlockSpec((V7X_LANES, c), lambda i: (0, i // per_half))
    return pl.pallas_call(
        functools.partial(_filter_kernel, length=length),
        grid=(2 * per_half,),
        in_specs=[pl.BlockSpec((tr // 2, V7X_LANES), lambda i: (i, 0)),
                  mat, vec, mat, vec, mat, vec, w4_spec, w4_spec, vec, small((1, c))],
        out_specs=pl.BlockSpec((tr, c), lambda i: (i, 0)),
        out_shape=jax.ShapeDtypeStruct((2 * length, c), F32),
        name="hyena_filter",
        compiler_params=_params(("parallel",), 0),
    )(zpos, w1d, row2(b1), twice(w2), row2(b2), twice(w3), row2(b3), w4lo, w4hi, row2(freq), decay)


def _complex_as_real(m):
    return np.block([[m.real, -m.imag], [m.imag, m.real]])


@functools.lru_cache(maxsize=None)
def _dft_tables(length):
    n = 2 * length
    n1 = n // DFT_RADIX
    half = n1 // 2
    eye = np.eye(V7X_SUBLANES)
    k1 = np.arange(n1)
    f1 = np.exp(-2j * np.pi * np.outer(k1, k1) / n1)
    first_data = np.kron(_complex_as_real(f1[:, :half]), eye)
    first_filt = np.kron(np.concatenate([f1.real, f1.imag], axis=0), eye)
    inv1 = np.exp(2j * np.pi * np.outer(np.arange(half), k1) / n1) / n
    last = np.kron(_complex_as_real(inv1), eye)
    i2 = np.arange(DFT_RADIX)
    phase = (np.outer(i2, i2)[None] / DFT_RADIX + (i2[None, None, :] * k1[:, None, None]) / n)
    second = np.exp(-2j * np.pi * phase)
    fwd = np.stack([_complex_as_real(second[k]) for k in range(n1)])
    inv = np.stack([_complex_as_real(np.conj(second[k]).T) for k in range(n1)])
    return tuple(np.asarray(a, np.float32) for a in (first_data, first_filt, last, fwd, inv))


def _pack_pair(re, im):
    re_bits = lax.bitcast_convert_type(re.astype(BF16).astype(F32), jnp.uint32)
    im_bits = lax.bitcast_convert_type(im.astype(BF16).astype(F32), jnp.uint32)
    return (re_bits >> 16) | im_bits


def _unpack_pair(packed):
    re = lax.bitcast_convert_type(packed << 16, F32)
    im = lax.bitcast_convert_type(packed & jnp.uint32(0xFFFF0000), F32)
    return jnp.concatenate([re, im], axis=0).astype(BF16)


def _kron_kernel(mat_ref, x_ref, o_ref):
    lanes = x_ref.shape[-1]
    spec = _mdot(mat_ref[...], x_ref[...].reshape(-1, lanes).astype(BF16))
    rows = spec.shape[0] // 2
    o_ref[...] = _pack_pair(spec[:rows], spec[rows:]).reshape(o_ref.shape)


def _kron_out_kernel(mat_ref, b_ref, x0_ref, u_ref, bias_ref, o_ref):
    lanes = b_ref.shape[-1]
    y = _mdot(mat_ref[...], _unpack_pair(b_ref[...].reshape(-1, lanes))).reshape(o_ref.shape)
    o_ref[...] = x0_ref[...] * (y + u_ref[...] * bias_ref[...])


def _second_stage_kernel(a_ref, af_ref, g_ref, gi_ref, o_ref, kf_ref):
    r = DFT_RADIX
    steps = a_ref.shape[0]

    @pl.when(pl.program_id(2) == 0)
    def _():
        for j in range(steps):
            kf_ref[j] = _mdot(g_ref[j], _unpack_pair(af_ref[j]))

    for j in range(steps):
        spec = _mdot(g_ref[j], _unpack_pair(a_ref[j]))
        xr, xi = spec[:r], spec[r:]
        kr, ki = kf_ref[j, :r], kf_ref[j, r:]
        prod = jnp.concatenate([xr * kr - xi * ki, xr * ki + xi * kr], axis=0).astype(BF16)
        back = _mdot(gi_ref[j], prod)
        o_ref[j] = _pack_pair(back[:r], back[r:])


def _long_conv_gate(u, x0, kfilt, bias, batch, length):
    c = u.shape[1]
    pairs = batch // 2
    n1 = 2 * length // DFT_RADIX
    half = n1 // 2
    r, s8, cb, kb = DFT_RADIX, V7X_SUBLANES, min(DFT_LANES, c), min(DFT_K1_PER_STEP, n1)
    cb2 = min(V7X_MXU_DIM, c)
    assert batch % 2 == 0 and n1 % kb == 0 and c % cb == 0 and c % cb2 == 0
    first_data, first_filt, last, fwd, inv = (jnp.asarray(t, BF16) for t in _dft_tables(length))
    groups = r // s8
    big = 48 * 1024 * 1024

    u5 = u.reshape(pairs, 2, half, r, c)
    x05 = x0.reshape(pairs, 2, half, r, c)
    a_data = pl.pallas_call(
        _kron_kernel,
        grid=(pairs, groups, c // cb),
        in_specs=[_resident(first_data.shape),
                  pl.BlockSpec((None, 2, half, s8, cb), lambda p, g, l: (p, 0, 0, g, l))],
        out_specs=pl.BlockSpec((None, n1, s8, cb), lambda p, g, l: (p, 0, g, l)),
        out_shape=jax.ShapeDtypeStruct((pairs, n1, r, c), jnp.uint32),
        name="dft_first",
        compiler_params=_params(("parallel", "parallel", "parallel"), big),
    )(first_data, u5)

    a_filt = pl.pallas_call(
        _kron_kernel,
        grid=(groups, c // cb),
        in_specs=[_resident(first_filt.shape),
                  pl.BlockSpec((n1, s8, cb), lambda g, l: (0, g, l))],
        out_specs=pl.BlockSpec((n1, s8, cb), lambda g, l: (0, g, l)),
        out_shape=jax.ShapeDtypeStruct((n1, r, c), jnp.uint32),
        name="dft_first_filter",
        compiler_params=_params(("parallel", "parallel"), big),
    )(first_filt, kfilt.reshape(n1, r, c))

    b_data = pl.pallas_call(
        _second_stage_kernel,
        grid=(n1 // kb, c // cb2, pairs),
        in_specs=[pl.BlockSpec((None, kb, r, cb2), lambda k, l, p: (p, k, 0, l)),
                  pl.BlockSpec((kb, r, cb2), lambda k, l, p: (k, 0, l)),
                  pl.BlockSpec((kb, 2 * r, 2 * r), lambda k, l, p: (k, 0, 0)),
                  pl.BlockSpec((kb, 2 * r, 2 * r), lambda k, l, p: (k, 0, 0))],
        out_specs=pl.BlockSpec((None, kb, r, cb2), lambda k, l, p: (p, k, 0, l)),
        out_shape=jax.ShapeDtypeStruct((pairs, n1, r, c), jnp.uint32),
        scratch_shapes=[pltpu.VMEM((kb, 2 * r, cb2), F32)],
        name="dft_second",
        compiler_params=_params(("arbitrary", "arbitrary", "arbitrary"), big),
    )(a_data, a_filt, fwd, inv)

    z = pl.pallas_call(
        _kron_out_kernel,
        grid=(pairs, groups, c // cb),
        in_specs=[_resident(last.shape),
                  pl.BlockSpec((None, n1, s8, cb), lambda p, g, l: (p, 0, g, l)),
                  pl.BlockSpec((None, 2, half, s8, cb), lambda p, g, l: (p, 0, 0, g, l)),
                  pl.BlockSpec((None, 2, half, s8, cb), lambda p, g, l: (p, 0, 0, g, l)),
                  pl.BlockSpec((1, cb), lambda p, g, l: (0, l))],
        out_specs=pl.BlockSpec((None, 2, half, s8, cb), lambda p, g, l: (p, 0, 0, g, l)),
        out_shape=jax.ShapeDtypeStruct((pairs, 2, half, r, c), F32),
        name="dft_last",
        compiler_params=_params(("parallel", "parallel", "parallel"), big),
    )(last, b_data, x05, u5, bias.reshape(1, c))
    return z.reshape(batch * length, c)


@functools.lru_cache(maxsize=None)
def _dense_dft_tables(length):
    n = 2 * length
    idx = np.arange(n)
    f = np.exp(-2j * np.pi * np.outer(idx, idx) / n)
    fwd_data = _complex_as_real(f[:, :length])
    fwd_filt = np.concatenate([f.real, f.imag], axis=0)
    inv = _complex_as_real(np.conj(f[:length, :]) / n)
    return tuple(np.asarray(a, np.float32) for a in (fwd_data, fwd_filt, inv))


def _short_conv_gate_kernel(fd_ref, ff_ref, inv_ref, u_ref, x0_ref, k_ref, bias_ref, o_ref):
    n = k_ref.shape[0]
    u = u_ref[...]
    spec = _mdot(fd_ref[...], u.astype(BF16))
    kf = _mdot(ff_ref[...], k_ref[...].astype(BF16))
    xr, xi, kr, ki = spec[:n], spec[n:], kf[:n], kf[n:]
    prod = jnp.concatenate([xr * kr - xi * ki, xr * ki + xi * kr], axis=0).astype(BF16)
    o_ref[...] = x0_ref[...] * (_mdot(inv_ref[...], prod) + u * bias_ref[...])


def _long_conv_gate_short(u, x0, kfilt, bias, batch, length):
    c = u.shape[1]
    pairs = batch // 2
    cb = min(DFT_LANES, c)
    fwd_data, fwd_filt, inv = (jnp.asarray(t, BF16) for t in _dense_dft_tables(length))
    pair_spec = pl.BlockSpec((None, 2 * length, cb), lambda p, l: (p, 0, l))
    z = pl.pallas_call(
        _short_conv_gate_kernel,
        grid=(pairs, c // cb),
        in_specs=[_resident(fwd_data.shape), _resident(fwd_filt.shape), _resident(inv.shape),
                  pair_spec, pair_spec,
                  pl.BlockSpec((2 * length, cb), lambda p, l: (0, l)),
                  pl.BlockSpec((1, cb), lambda p, l: (0, l))],
        out_specs=pair_spec,
        out_shape=jax.ShapeDtypeStruct((pairs, 2 * length, c), F32),
        name="short_conv_gate",
        compiler_params=_params(("parallel", "parallel"), 0),
    )(fwd_data, fwd_filt, inv, u.reshape(pairs, 2 * length, c), x0.reshape(pairs, 2 * length, c),
      kfilt, bias.reshape(1, c))
    return z.reshape(batch * length, c)


def kernel(x, c, ctx, c_ctx, w_mod, b_mod, norm_g, ffn_w_gate, ffn_w_up, ffn_w_down, ab_w_in, na_rpb,
           sc_conv_w, ab_w_out, hy_w_in, hy_short_w, hy_f_w1, hy_f_b1, hy_f_w2, hy_f_b2, hy_f_w3, hy_f_b3,
           hy_f_w4, hy_sin_freq, hy_bias, hy_w_out):
    batch, seq, d = x.shape
    ctx_len = ctx.shape[1]
    depth = w_mod.shape[0]
    assert batch + 1 <= V7X_SUBLANES and seq % ROW_TILE == 0
    last_attn = (depth - 1) - (depth - 1) % 2
    sc_width = sc_conv_w.shape[-1]

    cc = jnp.zeros((V7X_SUBLANES, d), F32).at[:batch].set(c).at[batch].set(c_ctx)
    mod = _modulation(cc, w_mod, b_mod).reshape(depth, V7X_SUBLANES, N_MOD, d)
    lat_batch = lambda i, tm: i // (seq // tm)
    ctx_batch = lambda i, tm: batch

    xs = x.reshape(batch * seq, d)
    cs = ctx.reshape(batch * ctx_len, d)
    ffn_w = (ffn_w_gate.astype(BF16), ffn_w_up.astype(BF16), ffn_w_down.astype(BF16))

    for i in range(depth):
        j = i // 2
        g = norm_g[i]
        m_i = mod[i]
        keep_ctx = i <= last_attn
        upd_ctx = i < last_attn

        xs = _ffn(xs, m_i, g, *ffn_w, (i, 0), mrow=0, grow=0, batch_of=lat_batch)
        if keep_ctx:
            cs = _ffn(cs, m_i, g, *ffn_w, (i, 0), mrow=0, grow=0, batch_of=ctx_batch)

        if i % 2 == 0:
            w_in = ab_w_in[j].astype(BF16)
            w_out = ab_w_out[j].astype(BF16)
            ab_outs = [(2 * NA_WIDTH, BF16, False), (NA_WIDTH, BF16, True), (sc_width, BF16, False)]
            w_vt = w_in[:, 2 * NA_WIDTH:3 * NA_WIDTH].T
            in_ab = functools.partial(_inproj_call, _inproj_ab_kernel, "inproj_ab", mrow=3, grow=2, extra=(w_vt,))
            qk, vt, b_lat = in_ab(xs, m_i, g, w_in, sc_conv_w[j], ab_outs, seq_len=seq, batch_of=lat_batch)
            if upd_ctx:
                qk_c, vt_c, b_ctx = in_ab(cs, m_i, g, w_in, sc_conv_w[j], ab_outs, seq_len=ctx_len,
                                          batch_of=ctx_batch)
            else:
                qk_c, vt_c = in_ab(cs, m_i, g, w_in, sc_conv_w[j], ab_outs[:2], seq_len=ctx_len,
                                   batch_of=ctx_batch)
            a_lat = _neighbourhood_attention(qk, vt, qk_c, vt_c, na_rpb[j], batch, seq, ctx_len)
            mix_lat = ([a_lat, b_lat], w_out)
            if upd_ctx:
                mix_ctx = ([_context_attention(qk_c, vt_c, batch, ctx_len), b_ctx], w_out)
        else:
            w_in = hy_w_in[j].astype(BF16)
            w_out = hy_w_out[j].astype(BF16)
            hw = w_out.shape[0]
            filt = lambda n: _hyena_filter(n, hy_f_w1[j], hy_f_b1[j], hy_f_w2[j], hy_f_b2[j], hy_f_w3[j],
                                           hy_f_b3[j], hy_f_w4[j], hy_sin_freq[j])
            in_hy = functools.partial(_inproj_call, _inproj_hyena_kernel, "inproj_hyena", mrow=3, grow=2)
            x0, u = in_hy(xs, m_i, g, w_in, hy_short_w[j], [(hw, F32, False)] * 2, seq_len=seq,
                          batch_of=lat_batch)
            mix_lat = ([_long_conv_gate(u, x0, filt(seq), hy_bias[j], batch, seq)], w_out)
            if upd_ctx:
                x0_c, u_c = in_hy(cs, m_i, g, w_in, hy_short_w[j], [(hw, F32, False)] * 2, seq_len=ctx_len,
                                  batch_of=ctx_batch)
                mix_ctx = ([_long_conv_gate_short(u_c, x0_c, filt(ctx_len), hy_bias[j], batch, ctx_len)], w_out)

        xs = _ffn(xs, m_i, g, *ffn_w, (i, 1), mrow=6, grow=4, batch_of=lat_batch, mixer=mix_lat,
                  mix_mrow=5, mix_grow=3)
        if upd_ctx:
            cs = _ffn(cs, m_i, g, *ffn_w, (i, 1), mrow=6, grow=4, batch_of=ctx_batch, mixer=mix_ctx,
                      mix_mrow=5, mix_grow=3)

    return xs.reshape(batch, seq, d)
```
